```python
import math
import jax, jax.numpy as jnp
from jax import lax
import numpy as np

D_MODEL = 4096
BATCH = 1
SEQ = 8192
DEPTH = 4

N_MIXERS = 4
BLK = 128
EPS = 1e-6

SB_HEADS = 32
SB_HEAD_DIM = 128

MLA_HEADS = 32
MLA_NOPE = 128
MLA_ROPE = 64
MLA_V = 128
MLA_Q_LORA = 1024
MLA_KV_LORA = 512
ROPE_THETA = 10000.0

SWA_HEADS = 32
SWA_KV_HEADS = 8
SWA_HEAD_DIM = 128
SWA_WINDOW = 128

DIL_PATTERNS = ((128, 1), (512, 4), (2048, 16))
DIL_HEADS = 16
DIL_HEAD_DIM = 128

D_FF = 11008
CONV_W = 3

kernel_name = "hybrid_interleaved_sb_mla_swa_dilated"


def _n_uses(m):
    return (DEPTH - m + N_MIXERS - 1) // N_MIXERS


def rmsnorm(x, g):
    xf = x.astype(jnp.float32)
    y = xf * lax.rsqrt(jnp.mean(xf * xf, axis=-1, keepdims=True) + EPS)
    return (y * g.astype(jnp.float32)).astype(x.dtype)


def alibi_slopes(n):
    return jnp.asarray(2.0 ** (-8.0 * np.arange(1, n + 1) / n), dtype=jnp.float32)


def rope(x, pos):
    half = x.shape[-1] // 2
    inv = ROPE_THETA ** (-jnp.arange(half, dtype=jnp.float32) / half)
    ang = pos.astype(jnp.float32)[..., None] * inv
    cos = jnp.cos(ang)[:, :, None, :]
    sin = jnp.sin(ang)[:, :, None, :]
    xf = x.astype(jnp.float32)
    x1, x2 = xf[..., :half], xf[..., half:]
    return jnp.concatenate([x1 * cos - x2 * sin, x2 * cos + x1 * sin], axis=-1).astype(x.dtype)


def _q_blocks(t):
    b, s = t.shape[:2]
    return jnp.moveaxis(t.reshape(b, s // BLK, BLK, *t.shape[2:]), 1, 0)


def _merge_blocks(t):
    nb, b = t.shape[:2]
    return jnp.moveaxis(t, 0, 1).reshape(b, nb * BLK, *t.shape[3:])


def stick_breaking_mixer(x, w_qkv, w_o):
    b, s, _ = x.shape
    qkv = (x @ w_qkv).reshape(b, s, 3, SB_HEADS, SB_HEAD_DIM)
    q, k, v = qkv[:, :, 0], qkv[:, :, 1], qkv[:, :, 2]
    scale = SB_HEAD_DIM ** -0.5
    kidx = jnp.arange(s)

    def block(args):
        qb, n = args
        qidx = n * BLK + jnp.arange(BLK)
        z = jnp.einsum('bqhd,bkhd->bhqk', qb, k).astype(jnp.float32) * scale
        causal = kidx[None, :] < qidx[:, None]
        log_beta = jax.nn.log_sigmoid(z)
        log_1mb = jnp.where(causal, jax.nn.log_sigmoid(-z), 0.0)
        tail = lax.cumsum(log_1mb, axis=3, reverse=True) - log_1mb
        a = jnp.where(causal, jnp.exp(log_beta + tail), 0.0)
        return jnp.einsum('bhqk,bkhd->bqhd', a.astype(v.dtype), v)

    o = _merge_blocks(lax.map(block, (_q_blocks(q), jnp.arange(s // BLK))))
    return o.reshape(b, s, SB_HEADS * SB_HEAD_DIM) @ w_o


def mla_mixer(x, pos, w_dq, g_q, w_uq, w_dkv, g_kv, w_ukv, w_o):
    b, s, _ = x.shape
    cq = rmsnorm(x @ w_dq, g_q)
    q = (cq @ w_uq).reshape(b, s, MLA_HEADS, MLA_NOPE + MLA_ROPE)
    q_nope = q[..., :MLA_NOPE]
    q_rope = rope(q[..., MLA_NOPE:], pos)
    ckv_kr = x @ w_dkv
    ckv = rmsnorm(ckv_kr[..., :MLA_KV_LORA], g_kv)
    k_rope = rope(ckv_kr[:, :, None, MLA_KV_LORA:], pos)[:, :, 0]
    kv = (ckv @ w_ukv).reshape(b, s, MLA_HEADS, MLA_NOPE + MLA_V)
    k_nope, v = kv[..., :MLA_NOPE], kv[..., MLA_NOPE:]
    scale = (MLA_NOPE + MLA_ROPE) ** -0.5
    kidx = jnp.arange(s)

    def block(args):
        qn, qr, n = args
        qidx = n * BLK + jnp.arange(BLK)
        sc = (jnp.einsum('bqhd,bkhd->bhqk', qn, k_nope)
              + jnp.einsum('bqhr,bkr->bhqk', qr, k_rope)).astype(jnp.float32) * scale
        sc = jnp.where(kidx[None, :] <= qidx[:, None], sc, -jnp.inf)
        p = jax.nn.softmax(sc, axis=-1)
        return jnp.einsum('bhqk,bkhd->bqhd', p.astype(v.dtype), v)

    o = _merge_blocks(lax.map(block, (_q_blocks(q_nope), _q_blocks(q_rope), jnp.arange(s // BLK))))
    return o.reshape(b, s, MLA_HEADS * MLA_V) @ w_o


def banded_attention(q, k, v, pos, window, slopes, sink=None):
    b, l, h, dh = q.shape
    g = k.shape[2]
    r = h // g
    nb = -(-l // BLK)
    pad = nb * BLK - l

    def padf(t):
        return jnp.pad(t, [(0, 0), (0, pad)] + [(0, 0)] * (t.ndim - 2))

    q, k, v, pos = padf(q), padf(k), padf(v), padf(pos)

    def band(t):
        tb = t.reshape(b, nb, BLK, *t.shape[2:])
        prev = jnp.pad(tb[:, :-1], [(0, 0), (1, 0)] + [(0, 0)] * (tb.ndim - 2))
        return jnp.concatenate([prev, tb], axis=2)

    qb = q.reshape(b, nb, BLK, g, r, dh)
    kb, vb, pk = band(k), band(v), band(pos)
    pq = pos.reshape(b, nb, BLK)
    sc = jnp.einsum('bnqgrd,bnkgd->bngrqk', qb, kb).astype(jnp.float32) * (dh ** -0.5)
    dist = (pq[:, :, :, None] - pk[:, :, None, :]).astype(jnp.float32)
    sc = sc - slopes.reshape(g, r)[None, None, :, :, None, None] * dist[:, :, None, None]
    qi = jnp.arange(BLK)[:, None]
    kj = jnp.arange(2 * BLK)[None, :]
    delta = qi + BLK - kj
    valid = (delta >= 0) & (delta <= window)
    has_prev = (jnp.arange(nb)[:, None, None] > 0) | (kj >= BLK)[None]
    mask = valid[None] & has_prev
    sc = jnp.where(mask[None, :, None, None], sc, -jnp.inf)
    m = jnp.max(sc, axis=-1, keepdims=True)
    if sink is not None:
        sk = sink.astype(jnp.float32).reshape(g, r)[None, None, :, :, None, None]
        m = jnp.maximum(m, sk)
    p = jnp.exp(sc - m)
    den = jnp.sum(p, axis=-1, keepdims=True)
    if sink is not None:
        den = den + jnp.exp(sk - m)
    o = jnp.einsum('bngrqk,bnkgd->bnqgrd', (p / den).astype(v.dtype), vb)
    lse = (m + jnp.log(den))[..., 0]
    o = o.reshape(b, nb * BLK, h, dh)[:, :l]
    lse = jnp.moveaxis(lse, 4, 2).reshape(b, nb * BLK, h)[:, :l]
    return o, lse


def swa_mixer(x, pos, w_qkv, sinks, w_o):
    b, s, _ = x.shape
    hq = SWA_HEADS * SWA_HEAD_DIM
    hk = SWA_KV_HEADS * SWA_HEAD_DIM
    qkv = x @ w_qkv
    q = qkv[..., :hq].reshape(b, s, SWA_HEADS, SWA_HEAD_DIM)
    k = qkv[..., hq:hq + hk].reshape(b, s, SWA_KV_HEADS, SWA_HEAD_DIM)
    v = qkv[..., hq + hk:].reshape(b, s, SWA_KV_HEADS, SWA_HEAD_DIM)
    o, _ = banded_attention(q, k, v, pos, SWA_WINDOW - 1, alibi_slopes(SWA_HEADS), sinks)
    return o.reshape(b, s, hq) @ w_o


def _strided(t, dil):
    b, s = t.shape[:2]
    tt = t.reshape(b, s // dil, dil, *t.shape[2:])
    return jnp.moveaxis(tt, 2, 1).reshape(b * dil, s // dil, *t.shape[2:])


def _unstrided(t, dil, b):
    n = t.shape[1]
    tt = t.reshape(b, dil, n, *t.shape[2:])
    return jnp.moveaxis(tt, 1, 2).reshape(b, n * dil, *t.shape[2:])


def dilated_mixer(x, pos, w_qkv, w_o):
    b, s, _ = x.shape
    n_groups = len(DIL_PATTERNS)
    qkv = (x @ w_qkv).reshape(b, s, n_groups, 3, DIL_HEADS, DIL_HEAD_DIM)
    slopes_all = alibi_slopes(n_groups * DIL_HEADS)
    outs, lses = [], []
    for gi, (win, dil) in enumerate(DIL_PATTERNS):
        q, k, v = qkv[:, :, gi, 0], qkv[:, :, gi, 1], qkv[:, :, gi, 2]
        o, lse = banded_attention(_strided(q, dil), _strided(k, dil), _strided(v, dil),
                                  _strided(pos, dil), win // dil,
                                  slopes_all[gi * DIL_HEADS:(gi + 1) * DIL_HEADS])
        outs.append(_unstrided(o, dil, b))
        lses.append(_unstrided(lse, dil, b))
    wgt = jax.nn.softmax(jnp.stack(lses, axis=0), axis=0)
    o = jnp.einsum('pbsh,pbshd->bshd', wgt.astype(x.dtype), jnp.stack(outs, axis=0))
    return o.reshape(b, s, DIL_HEADS * DIL_HEAD_DIM) @ w_o


def conv_ffn(x, w_in, conv_w, conv_b, w_out):
    s = x.shape[1]
    h = x @ w_in
    gate, up = h[..., :D_FF], h[..., D_FF:]
    gp = jnp.pad(gate, [(0, 0), (CONV_W - 1, 0), (0, 0)])
    gate = sum(conv_w[j] * gp[:, j:j + s] for j in range(CONV_W)) + conv_b
    return (jax.nn.silu(gate) * up) @ w_out


def setup_inputs(seed: int = 0) -> dict:
    key = jax.random.key(seed)
    ks = jax.random.split(key, 24)
    f32 = jnp.float32

    def w(k, shape, fan_in):
        return jax.random.normal(k, shape, f32) * fan_in ** -0.5

    na, nbm, nc, nd = (_n_uses(m) for m in range(N_MIXERS))
    n_groups = len(DIL_PATTERNS)
    x = jax.random.normal(ks[0], (BATCH, SEQ, D_MODEL), f32)
    offset = jax.random.randint(ks[1], (BATCH, 1), 0, 1024, dtype=jnp.int32)
    positions = offset + jnp.arange(SEQ, dtype=jnp.int32)[None, :]
    norm_g = 1.0 + 0.02 * jax.random.normal(ks[2], (DEPTH, 4, D_MODEL), f32)
    a_w_qkv = w(ks[3], (na, D_MODEL, 3 * SB_HEADS * SB_HEAD_DIM), D_MODEL)
    a_w_o = w(ks[4], (na, SB_HEADS * SB_HEAD_DIM, D_MODEL), SB_HEADS * SB_HEAD_DIM)
    b_w_dq = w(ks[5], (nbm, D_MODEL, MLA_Q_LORA), D_MODEL)
    b_g_q = 1.0 + 0.02 * jax.random.normal(ks[6], (nbm, MLA_Q_LORA), f32)
    b_w_uq = w(ks[7], (nbm, MLA_Q_LORA, MLA_HEADS * (MLA_NOPE + MLA_ROPE)), MLA_Q_LORA)
    b_w_dkv = w(ks[8], (nbm, D_MODEL, MLA_KV_LORA + MLA_ROPE), D_MODEL)
    b_g_kv = 1.0 + 0.02 * jax.random.normal(ks[9], (nbm, MLA_KV_LORA), f32)
    b_w_ukv = w(ks[10], (nbm, MLA_KV_LORA, MLA_HEADS * (MLA_NOPE + MLA_V)), MLA_KV_LORA)
    b_w_o = w(ks[11], (nbm, MLA_HEADS * MLA_V, D_MODEL), MLA_HEADS * MLA_V)
    c_w_qkv = w(ks[12], (nc, D_MODEL, (SWA_HEADS + 2 * SWA_KV_HEADS) * SWA_HEAD_DIM), D_MODEL)
    c_sinks = 0.5 * jax.random.normal(ks[13], (nc, SWA_HEADS), f32)
    c_w_o = w(ks[14], (nc, SWA_HEADS * SWA_HEAD_DIM, D_MODEL), SWA_HEADS * SWA_HEAD_DIM)
    d_w_qkv = w(ks[15], (nd, D_MODEL, n_groups * 3 * DIL_HEADS * DIL_HEAD_DIM), D_MODEL)
    d_w_o = w(ks[16], (nd, DIL_HEADS * DIL_HEAD_DIM, D_MODEL), DIL_HEADS * DIL_HEAD_DIM)
    ffn_w_in = w(ks[17], (DEPTH, D_MODEL, 2 * D_FF), D_MODEL)
    ffn_conv_w = w(ks[18], (DEPTH, CONV_W, D_FF), CONV_W)
    ffn_conv_b = 0.02 * jax.random.normal(ks[19], (DEPTH, D_FF), f32)
    ffn_w_out = w(ks[20], (DEPTH, D_FF, D_MODEL), D_FF)
    return {"x": x, "positions": positions, "norm_g": norm_g,
            "a_w_qkv": a_w_qkv, "a_w_o": a_w_o,
            "b_w_dq": b_w_dq, "b_g_q": b_g_q, "b_w_uq": b_w_uq, "b_w_dkv": b_w_dkv,
            "b_g_kv": b_g_kv, "b_w_ukv": b_w_ukv, "b_w_o": b_w_o,
            "c_w_qkv": c_w_qkv, "c_sinks": c_sinks, "c_w_o": c_w_o,
            "d_w_qkv": d_w_qkv, "d_w_o": d_w_o,
            "ffn_w_in": ffn_w_in, "ffn_conv_w": ffn_conv_w, "ffn_conv_b": ffn_conv_b,
            "ffn_w_out": ffn_w_out}


def reference(x, positions, norm_g, a_w_qkv, a_w_o, b_w_dq, b_g_q, b_w_uq, b_w_dkv, b_g_kv,
              b_w_ukv, b_w_o, c_w_qkv, c_sinks, c_w_o, d_w_qkv, d_w_o, ffn_w_in, ffn_conv_w,
              ffn_conv_b, ffn_w_out):
    for i in range(DEPTH):
        m, j = i % N_MIXERS, i // N_MIXERS
        hin = rmsnorm(x, norm_g[i, 0])
        if m == 0:
            h = stick_breaking_mixer(hin, a_w_qkv[j], a_w_o[j])
        elif m == 1:
            h = mla_mixer(hin, positions, b_w_dq[j], b_g_q[j], b_w_uq[j], b_w_dkv[j], b_g_kv[j],
                          b_w_ukv[j], b_w_o[j])
        elif m == 2:
            h = swa_mixer(hin, positions, c_w_qkv[j], c_sinks[j], c_w_o[j])
        else:
            h = dilated_mixer(hin, positions, d_w_qkv[j], d_w_o[j])
        x = x + rmsnorm(h, norm_g[i, 1])
        h = conv_ffn(rmsnorm(x, norm_g[i, 2]), ffn_w_in[i], ffn_conv_w[i], ffn_conv_b[i], ffn_w_out[i])
        x = x + rmsnorm(h, norm_g[i, 3])
    return x
```

```python
import functools

import numpy as np
import jax
import jax.numpy as jnp
from jax import lax
from jax.experimental import pallas as pl
from jax.experimental.pallas import tpu as pltpu

F32 = jnp.float32
BF16 = jnp.bfloat16

V7X_VMEM_BYTES = 64 * 2**20
V7X_LANES = 128
V7X_MXU_DIM = 256

EPS = 1e-6
BLK = 128
SB_HEADS, SB_HEAD_DIM = 32, 128
MLA_HEADS, MLA_NOPE, MLA_ROPE, MLA_V = 32, 128, 64, 128
MLA_Q_LORA, MLA_KV_LORA = 1024, 512
ROPE_THETA = 10000.0
SWA_HEADS, SWA_KV_HEADS, SWA_HEAD_DIM, SWA_WINDOW = 32, 8, 128, 128
DIL_PATTERNS = ((128, 1), (512, 4), (2048, 16))
DIL_HEADS, DIL_HEAD_DIM = 16, 128
CONV_W = 3

MASKED = -1e30
EXP_ZERO_BELOW = -105.0

_NT = (((1,), (1,)), ((), ()))


def _params(semantics, vmem_bytes):
    limit = min(int(vmem_bytes) + (8 << 20), V7X_VMEM_BYTES - (6 << 20))
    return pltpu.CompilerParams(dimension_semantics=semantics, vmem_limit_bytes=limit)


def _rms(x, g):
    return (x * lax.rsqrt(jnp.mean(x * x, axis=-1, keepdims=True) + EPS)) * g


def _mm_body(x_ref, w_ref, o_ref):
    acc = jnp.dot(x_ref[...], w_ref[...].astype(BF16), preferred_element_type=F32)
    o_ref[...] = acc.astype(o_ref.dtype)


def _mm_add_body(x_ref, w_ref, a_ref, o_ref):
    acc = jnp.dot(x_ref[...], w_ref[...].astype(BF16), preferred_element_type=F32)
    o_ref[...] = (a_ref[...] + acc).astype(o_ref.dtype)


def _mm_rope_body(x_ref, w_ref, cs_ref, o_ref):
    acc = jnp.dot(x_ref[...], w_ref[...].astype(BF16), preferred_element_type=F32)
    cs = cs_ref[...]
    for c in range(acc.shape[1] // (2 * V7X_LANES)):
        lo = 2 * c * V7X_LANES
        o_ref[:, lo:lo + V7X_LANES] = acc[:, lo:lo + V7X_LANES].astype(o_ref.dtype)
        o_ref[:, lo + V7X_LANES:lo + 2 * V7X_LANES] = (
            acc[:, lo + V7X_LANES:lo + 2 * V7X_LANES] * cs).astype(o_ref.dtype)


def matmul(x, w, *, bm, bn, out_dtype, layer=0, k_slice=None, add=None, rope_cs=None):
    m = x.shape[0]
    n = w.shape[2]
    kb, tk = (0, x.shape[1]) if k_slice is None else k_slice
    bm, bn = min(bm, m), min(bn, n)
    assert m % bm == 0 and n % bn == 0
    in_specs = [pl.BlockSpec((bm, tk), lambda i, j: (i, kb)),
                pl.BlockSpec((None, tk, bn), lambda i, j: (layer, kb, j))]
    args = [x, w]
    body = _mm_body
    aliases = {}
    extra = 0
    if add is not None:
        in_specs.append(pl.BlockSpec((bm, bn), lambda i, j: (i, j)))
        args.append(add)
        body = _mm_add_body
        aliases = {2: 0}
        extra = 2 * bm * bn * add.dtype.itemsize
    if rope_cs is not None:
        in_specs.append(pl.BlockSpec((bm, V7X_LANES), lambda i, j: (i, 0)))
        args.append(rope_cs)
        body = _mm_rope_body
    osize = jnp.dtype(out_dtype).itemsize
    vmem = (2 * bm * tk * 2 + 2 * tk * bn * w.dtype.itemsize + tk * bn * 2
            + 2 * bm * bn * osize + bm * bn * 4 + extra)
    return pl.pallas_call(
        body,
        grid=(m // bm, n // bn),
        in_specs=in_specs,
        out_specs=pl.BlockSpec((bm, bn), lambda i, j: (i, j)),
        out_shape=jax.ShapeDtypeStruct((m, n), out_dtype),
        input_output_aliases=aliases,
        compiler_params=_params(("arbitrary", "arbitrary"), vmem),
        name="matmul",
    )(*args)


def _norm_body(x_ref, g_ref, o_ref):
    o_ref[...] = _rms(x_ref[...], g_ref[...]).astype(o_ref.dtype)


def norm_rows(x, g, *, tm=256):
    m, d = x.shape
    tm = min(tm, m)
    return pl.pallas_call(
        _norm_body,
        grid=(m // tm,),
        in_specs=[pl.BlockSpec((tm, d), lambda i: (i, 0)),
                  pl.BlockSpec((1, d), lambda i: (0, 0))],
        out_specs=pl.BlockSpec((tm, d), lambda i: (i, 0)),
        out_shape=jax.ShapeDtypeStruct((m, d), BF16),
        compiler_params=_params(("arbitrary",), 2 * tm * d * 6 + 3 * tm * d * 4),
        name="norm_rows",
    )(x, g.reshape(1, d))


def _resid_norm_body(h_ref, x_ref, ga_ref, gb_ref, xo_ref, no_ref):
    xn = x_ref[...] + _rms(h_ref[...], ga_ref[...])
    xo_ref[...] = xn
    no_ref[...] = _rms(xn, gb_ref[...]).astype(no_ref.dtype)


def _resid_body(h_ref, x_ref, ga_ref, xo_ref):
    xo_ref[...] = x_ref[...] + _rms(h_ref[...], ga_ref[...])


def resid_norm(h, x, g_a, g_b=None, *, tm=128):
    m, d = x.shape
    tm = min(tm, m)
    row = pl.BlockSpec((tm, d), lambda i: (i, 0))
    gain = pl.BlockSpec((1, d), lambda i: (0, 0))
    if g_b is None:
        return pl.pallas_call(
            _resid_body,
            grid=(m // tm,),
            in_specs=[row, row, gain],
            out_specs=row,
            out_shape=jax.ShapeDtypeStruct((m, d), F32),
            compiler_params=_params(("arbitrary",), 2 * tm * d * 12 + 3 * tm * d * 4),
            name="resid",
        )(h, x, g_a.reshape(1, d))
    return pl.pallas_call(
        _resid_norm_body,
        grid=(m // tm,),
        in_specs=[row, row, gain, gain],
        out_specs=[row, row],
        out_shape=[jax.ShapeDtypeStruct((m, d), F32), jax.ShapeDtypeStruct((m, d), BF16)],
        compiler_params=_params(("arbitrary",), 2 * tm * d * 14 + 4 * tm * d * 4),
        name="resid_norm",
    )(h, x, g_a.reshape(1, d), g_b.reshape(1, d))


_HALO = 8


def _ffn_in_body(x_ref, wg_ref, wu_ref, cw_ref, cb_ref, o_ref, wgb_ref, wub_ref, gs_ref, *, bm):
    i = pl.program_id(1)

    @pl.when(i == 0)
    def _():
        wgb_ref[...] = wg_ref[...].astype(BF16)
        wub_ref[...] = wu_ref[...].astype(BF16)
        gs_ref[0:_HALO, :] = jnp.zeros((_HALO, gs_ref.shape[1]), F32)

    x = x_ref[...]
    g = jnp.dot(x, wgb_ref[...], preferred_element_type=F32)
    u = jnp.dot(x, wub_ref[...], preferred_element_type=F32)
    gs_ref[_HALO:_HALO + bm, :] = g
    g1 = gs_ref[_HALO - 1:_HALO - 1 + bm, :]
    g2 = gs_ref[_HALO - 2:_HALO - 2 + bm, :]
    cw = cw_ref[...]
    gate = cw[0:1, :] * g2 + cw[1:2, :] * g1 + cw[2:3, :] * g + cb_ref[...]
    o_ref[...] = ((gate / (1.0 + jnp.exp(-gate))) * u).astype(o_ref.dtype)
    gs_ref[0:_HALO, :] = gs_ref[bm:bm + _HALO, :]


def ffn_in(xn, w_in, layer, conv_w, conv_b, *, bm=1024, bn=V7X_MXU_DIM):
    m, k = xn.shape
    d_ff = w_in.shape[2] // 2
    bm = min(bm, m)
    bn = min(bn, d_ff)
    assert m % bm == 0 and d_ff % bn == 0
    nj = d_ff // bn
    vmem = (2 * bm * k * 2 + 2 * 2 * k * bn * 4 + 2 * k * bn * 2 + 2 * bm * bn * 2
            + (bm + _HALO) * bn * 4 + 6 * bm * bn * 4)
    return pl.pallas_call(
        functools.partial(_ffn_in_body, bm=bm),
        grid=(nj, m // bm),
        in_specs=[pl.BlockSpec((bm, k), lambda j, i: (i, 0)),
                  pl.BlockSpec((None, k, bn), lambda j, i: (layer, 0, j)),
                  pl.BlockSpec((None, k, bn), lambda j, i: (layer, 0, nj + j)),
                  pl.BlockSpec((CONV_W, bn), lambda j, i: (0, j)),
                  pl.BlockSpec((1, bn), lambda j, i: (0, j))],
        out_specs=pl.BlockSpec((bm, bn), lambda j, i: (i, j)),
        out_shape=jax.ShapeDtypeStruct((m, d_ff), BF16),
        scratch_shapes=[pltpu.VMEM((k, bn), BF16), pltpu.VMEM((k, bn), BF16),
                        pltpu.VMEM((bm + _HALO, bn), F32)],
        compiler_params=_params(("arbitrary", "arbitrary"), vmem),
        name="ffn_in",
    )(xn, w_in, w_in, conv_w, conv_b.reshape(1, d_ff))


def ffn_out(act, w_out, layer, *, bm=1024, bn=V7X_MXU_DIM):
    d_ff = act.shape[1]
    if d_ff % (2 * V7X_LANES) != 0:
        return matmul(act, w_out, bm=bm, bn=bn, out_dtype=F32, layer=layer)
    half = d_ff // 2
    part = matmul(act, w_out, bm=bm, bn=bn, out_dtype=F32, layer=layer, k_slice=(0, half))
    return matmul(act, w_out, bm=bm, bn=bn, out_dtype=F32, layer=layer, k_slice=(1, half),
                  add=part)


def _sb_attn_body(q_ref, k_ref, v_ref, tri_ref, o_ref, acc_ref, c_ref, *, blk, scale):
    qi = pl.program_id(1)
    q = q_ref[...]
    tri = tri_ref[...]
    acc_ref[...] = jnp.zeros(acc_ref.shape, F32)
    c_ref[...] = jnp.zeros(c_ref.shape, F32)
    row = lax.broadcasted_iota(jnp.int32, (blk, blk), 0)
    col = lax.broadcasted_iota(jnp.int32, (blk, blk), 1)
    causal = col < row

    def block(kb, diagonal):
        start = pl.multiple_of(kb * blk, blk)
        z = lax.dot_general(q, k_ref[pl.ds(start, blk), :], _NT,
                            preferred_element_type=F32) * scale
        sp = jnp.log(1.0 + jnp.exp(-jnp.abs(z)))
        log_beta = jnp.minimum(z, 0.0) - sp
        log_1mb = jnp.minimum(-z, 0.0) - sp
        if diagonal:
            log_1mb = jnp.where(causal, log_1mb, 0.0)
        hi = log_1mb.astype(BF16)
        lo = (log_1mb - hi.astype(F32)).astype(BF16)
        tail = (jnp.dot(hi, tri, preferred_element_type=F32)
                + jnp.dot(lo, tri, preferred_element_type=F32))
        c = c_ref[...]
        a = jnp.exp(log_beta + tail + c)
        if diagonal:
            a = jnp.where(causal, a, 0.0)
        acc_ref[...] += jnp.dot(a.astype(BF16), v_ref[pl.ds(start, blk), :],
                                preferred_element_type=F32)
        c_new = c + tail[:, 0:1] + log_1mb[:, 0:1]
        c_ref[...] = c_new
        return jnp.max(c_new)

    top = block(qi, True)
    lax.while_loop(lambda st: (st[0] >= 0) & (st[1] > EXP_ZERO_BELOW),
                   lambda st: (st[0] - 1, block(st[0], False)),
                   (qi - 1, top))
    o_ref[...] = acc_ref[...].astype(o_ref.dtype)


def sb_attention(qkv, *, blk=256):
    s = qkv.shape[0]
    h, dh = SB_HEADS, SB_HEAD_DIM
    blk = min(blk, s)
    tri = (np.arange(blk)[:, None] > np.arange(blk)[None, :]).astype(np.float32)
    vmem = 2 * 2 * s * dh * 2 + 4 * blk * dh * 2 + 2 * blk * blk * 2 + 12 * blk * blk * 4
    return pl.pallas_call(
        functools.partial(_sb_attn_body, blk=blk, scale=dh ** -0.5),
        grid=(h, s // blk),
        in_specs=[pl.BlockSpec((blk, dh), lambda hh, i: (i, hh)),
                  pl.BlockSpec((s, dh), lambda hh, i: (0, h + hh)),
                  pl.BlockSpec((s, dh), lambda hh, i: (0, 2 * h + hh)),
                  pl.BlockSpec((blk, blk), lambda hh, i: (0, 0))],
        out_specs=pl.BlockSpec((blk, dh), lambda hh, i: (i, hh)),
        out_shape=jax.ShapeDtypeStruct((s, h * dh), BF16),
        scratch_shapes=[pltpu.VMEM((blk, dh), F32), pltpu.VMEM((blk, 1), F32)],
        compiler_params=_params(("arbitrary", "arbitrary"), vmem),
        name="sb_attention",
    )(qkv, qkv, qkv, jnp.asarray(tri, BF16))


def _mla_prep_body(y_ref, pos_ref, inv_ref, gq_ref, gkv_ref, cq_ref, ckv_ref, kr_ref, cs_ref):
    ql, kvl = MLA_Q_LORA, MLA_KV_LORA
    cq_ref[...] = _rms(y_ref[:, 0:ql], gq_ref[...]).astype(cq_ref.dtype)
    ckv_ref[...] = _rms(y_ref[:, ql:ql + kvl], gkv_ref[...]).astype(ckv_ref.dtype)
    ang = pos_ref[...] * inv_ref[...]
    lane = lax.broadcasted_iota(jnp.int32, ang.shape, 1)
    cs = jnp.where(lane < MLA_ROPE, jnp.cos(ang), jnp.sin(ang))
    cs_ref[...] = cs
    w = y_ref[:, ql + kvl:ql + kvl + V7X_LANES] * cs
    kr_ref[...] = (w + pltpu.roll(w, MLA_ROPE, axis=1)).astype(kr_ref.dtype)


def mla_prep(y, pos_col, g_q, g_kv, *, tm=256):
    m, width = y.shape
    tm = min(tm, m)
    half = MLA_ROPE // 2
    inv = ROPE_THETA ** (-jnp.arange(half, dtype=F32) / half)
    inv4 = jnp.tile(inv, 4).reshape(1, V7X_LANES)
    row = lambda w: pl.BlockSpec((tm, w), lambda i: (i, 0))
    const = lambda w: pl.BlockSpec((1, w), lambda i: (0, 0))
    return pl.pallas_call(
        _mla_prep_body,
        grid=(m // tm,),
        in_specs=[row(width), row(1), const(V7X_LANES), const(MLA_Q_LORA), const(MLA_KV_LORA)],
        out_specs=[row(MLA_Q_LORA), row(MLA_KV_LORA), row(V7X_LANES), row(V7X_LANES)],
        out_shape=[jax.ShapeDtypeStruct((m, MLA_Q_LORA), BF16),
                   jax.ShapeDtypeStruct((m, MLA_KV_LORA), BF16),
                   jax.ShapeDtypeStruct((m, V7X_LANES), BF16),
                   jax.ShapeDtypeStruct((m, V7X_LANES), F32)],
        compiler_params=_params(("arbitrary",), 6 * tm * width * 4),
        name="mla_prep",
    )(y, pos_col, inv4, g_q.reshape(1, -1), g_kv.reshape(1, -1))


def _mla_attn_body(q_ref, kn_ref, kr_ref, v_ref, o_ref, m_ref, l_ref, acc_ref, *, blk, scale):
    qi = pl.program_id(1)
    q = q_ref[...]
    m_ref[...] = jnp.full(m_ref.shape, MASKED, F32)
    l_ref[...] = jnp.zeros(l_ref.shape, F32)
    acc_ref[...] = jnp.zeros(acc_ref.shape, F32)
    row = lax.broadcasted_iota(jnp.int32, (blk, blk), 0)
    col = lax.broadcasted_iota(jnp.int32, (blk, blk), 1)

    def step(kb, diagonal):
        start = pl.multiple_of(kb * blk, blk)
        kcat = jnp.concatenate([kn_ref[pl.ds(start, blk), :], kr_ref[pl.ds(start, blk), :]],
                               axis=1)
        s = lax.dot_general(q, kcat, _NT, preferred_element_type=F32) * scale
        if diagonal:
            s = jnp.where(col <= row, s, MASKED)
        m_prev = m_ref[...]
        m_new = jnp.maximum(m_prev, jnp.max(s, axis=1, keepdims=True))
        p = jnp.exp(s - m_new)
        alpha = jnp.exp(m_prev - m_new)
        l_ref[...] = alpha * l_ref[...] + jnp.sum(p, axis=1, keepdims=True)
        acc_ref[...] = alpha * acc_ref[...] + jnp.dot(
            p.astype(BF16), v_ref[pl.ds(start, blk), :], preferred_element_type=F32)
        m_ref[...] = m_new

    def off_diagonal(kb, carry):
        step(kb, False)
        return carry

    lax.fori_loop(0, qi, off_diagonal, 0)
    step(qi, True)
    o_ref[...] = (acc_ref[...] / l_ref[...]).astype(o_ref.dtype)


def mla_attention(q, kv, kr, *, blk=512):
    s = q.shape[0]
    h = MLA_HEADS
    blk = min(blk, s)
    width = MLA_NOPE + V7X_LANES
    vmem = (2 * blk * width * 2 + 2 * 3 * s * V7X_LANES * 2 + 2 * blk * MLA_V * 2
            + 3 * blk * V7X_LANES * 4 + 10 * blk * blk * 4)
    return pl.pallas_call(
        functools.partial(_mla_attn_body, blk=blk, scale=(MLA_NOPE + MLA_ROPE) ** -0.5),
        grid=(h, s // blk),
        in_specs=[pl.BlockSpec((blk, width), lambda hh, i: (i, hh)),
                  pl.BlockSpec((s, MLA_NOPE), lambda hh, i: (0, 2 * hh)),
                  pl.BlockSpec((s, V7X_LANES), lambda hh, i: (0, 0)),
                  pl.BlockSpec((s, MLA_V), lambda hh, i: (0, 2 * hh + 1))],
        out_specs=pl.BlockSpec((blk, MLA_V), lambda hh, i: (i, hh)),
        out_shape=jax.ShapeDtypeStruct((s, h * MLA_V), BF16),
        scratch_shapes=[pltpu.VMEM((blk, 1), F32), pltpu.VMEM((blk, 1), F32),
                        pltpu.VMEM((blk, MLA_V), F32)],
        compiler_params=_params(("arbitrary", "arbitrary"), vmem),
        name="mla_attention",
    )(q, kv, kr, kv)


def _band_attn_body(*refs, tile, rep, window, scale, has_sink, want_lse):
    refs = list(refs)
    slope_ref = refs.pop(0)
    sink_ref = refs.pop(0) if has_sink else None
    q_ref, k_ref, v_ref, pq_ref, pk_ref, o_ref = refs[:6]
    lse_ref = refs[6] if want_lse else None
    hbase = pl.program_id(1) * rep
    t = pl.program_id(2)
    nb = tile // BLK
    qi = lax.broadcasted_iota(jnp.int32, (BLK, 2 * BLK), 0)
    kj = lax.broadcasted_iota(jnp.int32, (BLK, 2 * BLK), 1)
    for b in range(nb):
        gb = t * nb + b
        pb = jnp.maximum(gb - 1, 0)
        start = pl.multiple_of(pb * BLK, BLK)
        kwin = k_ref[pl.ds(start, 2 * BLK), :]
        vwin = v_ref[pl.ds(start, 2 * BLK), :]
        pk = jnp.concatenate([pk_ref[pb], pk_ref[pb + 1]], axis=1)
        dist = pq_ref[b * BLK:(b + 1) * BLK, :] - pk
        delta = (gb - pb) * BLK + qi - kj
        valid = (delta >= 0) & (delta <= window)
        for r in range(rep):
            rows = slice(b * BLK, (b + 1) * BLK)
            cols = slice(r * BLK, (r + 1) * BLK)
            sc = lax.dot_general(q_ref[rows, cols], kwin, _NT, preferred_element_type=F32) * scale
            sc = sc - slope_ref[hbase + r] * dist
            sc = jnp.where(valid, sc, MASKED)
            m = jnp.max(sc, axis=1, keepdims=True)
            if has_sink:
                sk = sink_ref[hbase + r]
                m = jnp.maximum(m, sk)
            p = jnp.exp(sc - m)
            den = jnp.sum(p, axis=1, keepdims=True)
            if has_sink:
                den = den + jnp.exp(sk - m)
            o = jnp.dot((p * (1.0 / den)).astype(BF16), vwin, preferred_element_type=F32)
            o_ref[rows, cols] = o.astype(o_ref.dtype)
            if want_lse:
                lse_ref[rows, cols] = jnp.broadcast_to(m + jnp.log(den), (BLK, BLK))


def band_attention(arr, col_q, col_k, col_v, out_col, pos_col, pos_row, slopes, sinks, *,
                   n_seq, n_kv, rep, seq_len, window, out_cols, out_dtype, want_lse):
    dh = BLK
    tile = min(seq_len, 4 * BLK)
    has_sink = sinks is not None
    smem = pl.BlockSpec(memory_space=pltpu.SMEM)
    in_specs = [smem] + ([smem] if has_sink else []) + [
        pl.BlockSpec((tile, rep * dh), lambda r, g, t: (t, col_q(r, g))),
        pl.BlockSpec((seq_len, dh), lambda r, g, t: (0, col_k(r, g))),
        pl.BlockSpec((seq_len, dh), lambda r, g, t: (0, col_v(r, g))),
        pl.BlockSpec((None, tile, 1), lambda r, g, t: (r, t, 0)),
        pl.BlockSpec((None, seq_len // BLK, 1, BLK), lambda r, g, t: (r, 0, 0, 0)),
    ]
    args = [slopes] + ([sinks] if has_sink else []) + [arr, arr, arr, pos_col, pos_row]
    out_block = pl.BlockSpec((tile, rep * dh), lambda r, g, t: (t, out_col(r, g)))
    out_sds = jax.ShapeDtypeStruct((seq_len, out_cols), out_dtype)
    osize = jnp.dtype(out_dtype).itemsize
    vmem = (2 * tile * rep * dh * 2 + 2 * 2 * seq_len * dh * 2 + 4 * tile * rep * dh * osize
            + 2 * tile * V7X_LANES * 4 + 2 * seq_len * 4 * 8 + 16 * BLK * 2 * BLK * 4)
    return pl.pallas_call(
        functools.partial(_band_attn_body, tile=tile, rep=rep, window=window,
                          scale=dh ** -0.5, has_sink=has_sink, want_lse=want_lse),
        grid=(n_seq, n_kv, seq_len // tile),
        in_specs=in_specs,
        out_specs=[out_block, out_block] if want_lse else out_block,
        out_shape=[out_sds, jax.ShapeDtypeStruct((seq_len, out_cols), F32)] if want_lse else out_sds,
        compiler_params=_params(("arbitrary", "arbitrary", "arbitrary"), vmem),
        name="band_attention",
    )(*args)


def _alibi_slopes(n):
    return jnp.asarray(2.0 ** (-8.0 * np.arange(1, n + 1) / n), dtype=F32)


def _strided_positions(pos_f32, dil):
    s = pos_f32.shape[0]
    ps = pos_f32.reshape(s // dil, dil).T
    return ps.reshape(dil, s // dil, 1), ps.reshape(dil, s // dil // BLK, 1, BLK)


def _dil_merge_body(o0, o1, o2, l0, l1, l2, out_ref):
    a, b, c = l0[...], l1[...], l2[...]
    m = jnp.maximum(jnp.maximum(a, b), c)
    ea, eb, ec = jnp.exp(a - m), jnp.exp(b - m), jnp.exp(c - m)
    inv = 1.0 / (ea + eb + ec)
    out_ref[...] = ((ea * inv) * o0[...] + (eb * inv) * o1[...] + (ec * inv) * o2[...]
                    ).astype(out_ref.dtype)


def dil_merge(outs, lses, *, tm=256):
    m, d = outs[0].shape
    tm = min(tm, m)
    row = pl.BlockSpec((tm, d), lambda i: (i, 0))
    return pl.pallas_call(
        _dil_merge_body,
        grid=(m // tm,),
        in_specs=[row] * 6,
        out_specs=row,
        out_shape=jax.ShapeDtypeStruct((m, d), BF16),
        compiler_params=_params(("arbitrary",), 2 * 6 * tm * d * 4 + 8 * tm * d * 4),
        name="dil_merge",
    )(*outs, *lses)


def stick_breaking_mixer(hin, w_qkv, w_o, j):
    qkv = matmul(hin, w_qkv, bm=1024, bn=512, out_dtype=BF16, layer=j)
    o = sb_attention(qkv)
    return matmul(o, w_o, bm=1024, bn=512, out_dtype=F32, layer=j)


def mla_mixer(hin, pos_f32, w_dq, g_q, w_uq, w_dkv, g_kv, w_ukv, w_o, j):
    h, nope, rope = MLA_HEADS, MLA_NOPE, MLA_ROPE
    half = rope // 2
    w_dq, w_uq, w_dkv, g_q, g_kv = w_dq[j], w_uq[j], w_dkv[j], g_q[j], g_kv[j]
    kx1 = w_dkv[:, MLA_KV_LORA:MLA_KV_LORA + half]
    kx2 = w_dkv[:, MLA_KV_LORA + half:]
    w_down = jnp.concatenate(
        [w_dq, w_dkv[:, :MLA_KV_LORA], kx1, kx2, -kx2, kx1,
         jnp.zeros((w_dq.shape[0], V7X_LANES), w_dq.dtype)], axis=1)
    y = matmul(hin, w_down[None], bm=1024, bn=V7X_MXU_DIM, out_dtype=F32)
    cq, ckv, kr, cs = mla_prep(y, pos_f32.reshape(-1, 1), g_q, g_kv)
    wq = w_uq.reshape(MLA_Q_LORA, h, nope + rope)
    qx1, qx2 = wq[:, :, nope:nope + half], wq[:, :, nope + half:]
    wq = jnp.concatenate([wq[:, :, :nope], qx1, qx2, -qx2, qx1], axis=2)
    q = matmul(cq, wq.reshape(1, MLA_Q_LORA, h * 2 * V7X_LANES), bm=1024, bn=1024,
               out_dtype=BF16, rope_cs=cs)
    kv = matmul(ckv, w_ukv, bm=1024, bn=1024, out_dtype=BF16, layer=j)
    o = mla_attention(q, kv, kr)
    return matmul(o, w_o, bm=1024, bn=512, out_dtype=F32, layer=j)


def swa_mixer(hin, pos_f32, w_qkv, sinks, w_o, j):
    s = hin.shape[0]
    rep = SWA_HEADS // SWA_KV_HEADS
    sinks = sinks[j]
    qkv = matmul(hin, w_qkv, bm=1024, bn=512, out_dtype=BF16, layer=j)
    o = band_attention(
        qkv,
        lambda r, g: g,
        lambda r, g: SWA_HEADS + g,
        lambda r, g: SWA_HEADS + SWA_KV_HEADS + g,
        lambda r, g: g,
        pos_f32.reshape(1, s, 1), pos_f32.reshape(1, s // BLK, 1, BLK),
        _alibi_slopes(SWA_HEADS), sinks,
        n_seq=1, n_kv=SWA_KV_HEADS, rep=rep, seq_len=s, window=SWA_WINDOW - 1,
        out_cols=SWA_HEADS * SWA_HEAD_DIM, out_dtype=BF16, want_lse=False)
    return matmul(o, w_o, bm=1024, bn=512, out_dtype=F32, layer=j)


def dilated_mixer(hin, pos_f32, w_qkv, w_o, j):
    s = hin.shape[0]
    nh = DIL_HEADS
    n_groups = len(DIL_PATTERNS)
    qkv = matmul(hin, w_qkv, bm=1024, bn=512, out_dtype=BF16, layer=j)
    width = qkv.shape[1] // BLK
    slopes_all = _alibi_slopes(n_groups * nh)
    outs, lses = [], []
    for gi, (win, dil) in enumerate(DIL_PATTERNS):
        base = gi * 3 * nh
        pos_col, pos_row = _strided_positions(pos_f32, dil)
        o, lse = band_attention(
            qkv.reshape(s // dil, dil * qkv.shape[1]),
            lambda r, g, base=base: r * width + base + g,
            lambda r, g, base=base: r * width + base + nh + g,
            lambda r, g, base=base: r * width + base + 2 * nh + g,
            lambda r, g: r * nh + g,
            pos_col, pos_row, slopes_all[gi * nh:(gi + 1) * nh], None,
            n_seq=dil, n_kv=nh, rep=1, seq_len=s // dil, window=win // dil,
            out_cols=dil * nh * DIL_HEAD_DIM, out_dtype=F32, want_lse=True)
        outs.append(o.reshape(s, nh * DIL_HEAD_DIM))
        lses.append(lse.reshape(s, nh * DIL_HEAD_DIM))
    o = dil_merge(outs, lses)
    return matmul(o, w_o, bm=1024, bn=512, out_dtype=F32, layer=j)


def kernel(x, positions, norm_g, a_w_qkv, a_w_o, b_w_dq, b_g_q, b_w_uq, b_w_dkv, b_g_kv,
           b_w_ukv, b_w_o, c_w_qkv, c_sinks, c_w_o, d_w_qkv, d_w_o, ffn_w_in, ffn_conv_w,
           ffn_conv_b, ffn_w_out):
    batch, s, d = x.shape
    depth = norm_g.shape[0]
    n_mixers = 4
    outs = []
    for bi in range(batch):
        xb = x[bi]
        pos_f32 = positions[bi].astype(F32)
        hin = norm_rows(xb, norm_g[0, 0])
        for i in range(depth):
            mixer, j = i % n_mixers, i // n_mixers
            if mixer == 0:
                h = stick_breaking_mixer(hin, a_w_qkv, a_w_o, j)
            elif mixer == 1:
                h = mla_mixer(hin, pos_f32, b_w_dq, b_g_q, b_w_uq, b_w_dkv, b_g_kv, b_w_ukv,
                              b_w_o, j)
            elif mixer == 2:
                h = swa_mixer(hin, pos_f32, c_w_qkv, c_sinks, c_w_o, j)
            else:
                h = dilated_mixer(hin, pos_f32, d_w_qkv, d_w_o, j)
            xb, xn = resid_norm(h, xb, norm_g[i, 1], norm_g[i, 2])
            act = ffn_in(xn, ffn_w_in, i, ffn_conv_w[i], ffn_conv_b[i])
            h = ffn_out(act, ffn_w_out, i)
            if i + 1 < depth:
                xb, hin = resid_norm(h, xb, norm_g[i, 3], norm_g[i + 1, 0])
            else:
                xb = resid_norm(h, xb, norm_g[i, 3])
        outs.append(xb)
    return jnp.stack(outs, axis=0)
```

```python
import functools

import numpy as np
import jax
import jax.numpy as jnp
from jax import lax
from jax.experimental import pallas as pl
from jax.experimental.pallas import tpu as pltpu

F32 = jnp.float32
BF16 = jnp.bfloat16

V7X_VMEM_BYTES = 64 * 2**20
V7X_LANES = 128
V7X_MXU_DIM = 256

EPS = 1e-6
BLK = 128
SB_HEADS, SB_HEAD_DIM = 32, 128
MLA_HEADS, MLA_NOPE, MLA_ROPE, MLA_V = 32, 128, 64, 128
MLA_Q_LORA, MLA_KV_LORA = 1024, 512
ROPE_THETA = 10000.0
SWA_HEADS, SWA_KV_HEADS, SWA_HEAD_DIM, SWA_WINDOW = 32, 8, 128, 128
DIL_PATTERNS = ((128, 1), (512, 4), (2048, 16))
DIL_HEADS, DIL_HEAD_DIM = 16, 128
CONV_W = 3

MASKED = -1e30
EXP_ZERO_BELOW = -105.0

_NT = (((1,), (1,)), ((), ()))


def _params(semantics, vmem_bytes):
    limit = min(int(vmem_bytes) + (8 << 20), V7X_VMEM_BYTES - (6 << 20))
    return pltpu.CompilerParams(dimension_semantics=semantics, vmem_limit_bytes=limit)


def _rms(x, g):
    return (x * lax.rsqrt(jnp.mean(x * x, axis=-1, keepdims=True) + EPS)) * g


def _mm_body(*refs, mode, resident):
    refs = list(refs)
    wb_ref = refs.pop() if resident else None
    x_ref, w_ref, o_ref = refs[0], refs[1], refs[-1]
    if resident:
        @pl.when(pl.program_id(1) == 0)
        def _():
            wb_ref[...] = w_ref[...].astype(BF16)
        acc = jnp.dot(x_ref[...], wb_ref[...], preferred_element_type=F32)
    else:
        acc = jnp.dot(x_ref[...], w_ref[...].astype(BF16), preferred_element_type=F32)
    if mode == "plain":
        o_ref[...] = acc.astype(o_ref.dtype)
    elif mode == "add":
        o_ref[...] = (refs[2][...] + acc).astype(o_ref.dtype)
    else:
        cs = refs[2][...]
        for c in range(acc.shape[1] // (2 * V7X_LANES)):
            lo = 2 * c * V7X_LANES
            o_ref[:, lo:lo + V7X_LANES] = acc[:, lo:lo + V7X_LANES].astype(o_ref.dtype)
            o_ref[:, lo + V7X_LANES:lo + 2 * V7X_LANES] = (
                acc[:, lo + V7X_LANES:lo + 2 * V7X_LANES] * cs).astype(o_ref.dtype)


def matmul(x, w, *, bm, bn, out_dtype, layer=0, k_slice=None, n_slice=None, row_groups=1,
           resident=True, add=None, rope_cs=None):
    n0, n = (0, w.shape[2]) if n_slice is None else n_slice
    if row_groups == 1:
        m, kx = x.shape
    else:
        assert k_slice is None
        m, kx = x.shape[0] * row_groups, x.shape[1] // row_groups
    kb, tk = (0, kx) if k_slice is None else k_slice
    bm, bn = min(bm, m // row_groups), min(bn, n)
    assert (m // row_groups) % bm == 0 and n % bn == 0 and n0 % bn == 0
    per_group = m // row_groups // bm
    order = (lambda f: lambda j, i: f(i, j)) if resident else (lambda f: f)
    in_specs = [pl.BlockSpec((bm, tk), order(lambda i, j: (i % per_group, i // per_group + kb))),
                pl.BlockSpec((None, tk, bn), order(lambda i, j: (layer, kb, n0 // bn + j)))]
    args = [x, w]
    mode = "plain"
    aliases = {}
    extra = 0
    if add is not None:
        in_specs.append(pl.BlockSpec((bm, bn), order(lambda i, j: (i, j))))
        args.append(add)
        mode = "add"
        aliases = {2: 0}
        extra = 2 * bm * bn * add.dtype.itemsize
    if rope_cs is not None:
        in_specs.append(pl.BlockSpec((bm, V7X_LANES), order(lambda i, j: (i, 0))))
        args.append(rope_cs)
        mode = "rope"
    osize = jnp.dtype(out_dtype).itemsize
    vmem = (2 * bm * tk * 2 + 2 * tk * bn * w.dtype.itemsize + tk * bn * 2
            + 2 * bm * bn * osize + bm * bn * 4 + extra)
    return pl.pallas_call(
        functools.partial(_mm_body, mode=mode, resident=resident),
        grid=(n // bn, m // bm) if resident else (m // bm, n // bn),
        in_specs=in_specs,
        out_specs=pl.BlockSpec((bm, bn), order(lambda i, j: (i, j))),
        out_shape=jax.ShapeDtypeStruct((m, n), out_dtype),
        scratch_shapes=[pltpu.VMEM((tk, bn), BF16)] if resident else [],
        input_output_aliases=aliases,
        compiler_params=_params(("arbitrary", "arbitrary"), vmem),
        name="matmul",
    )(*args)


def _mm_nt_body(w_ref, x_ref, o_ref):
    o_ref[...] = lax.dot_general(w_ref[...].astype(BF16), x_ref[...], _NT,
                                 preferred_element_type=F32).astype(o_ref.dtype)


def matmul_nt_tiled(w_t, x, *, bm, bn, out_dtype):
    n, k = w_t.shape
    m = x.shape[0]
    bm, bn = min(bm, m), min(bn, n)
    assert m % bm == 0 and n % bn == 0
    vmem = 3 * bn * k * 4 + 2 * bm * k * 2 + 2 * bn * bm * jnp.dtype(out_dtype).itemsize + bn * bm * 4
    return pl.pallas_call(
        _mm_nt_body,
        grid=(n // bn, m // bm),
        in_specs=[pl.BlockSpec((bn, k), lambda j, i: (j, 0)),
                  pl.BlockSpec((bm, k), lambda j, i: (i, 0))],
        out_specs=pl.BlockSpec((None, bn, bm), lambda j, i: (i, j, 0)),
        out_shape=jax.ShapeDtypeStruct((m // bm, n, bm), out_dtype),
        compiler_params=_params(("arbitrary", "arbitrary"), vmem),
        name="matmul_nt",
    )(w_t, x)


def _norm_body(x_ref, g_ref, o_ref):
    o_ref[...] = _rms(x_ref[...], g_ref[...]).astype(o_ref.dtype)


def norm_rows(x, g, *, tm=256):
    m, d = x.shape
    tm = min(tm, m)
    return pl.pallas_call(
        _norm_body,
        grid=(m // tm,),
        in_specs=[pl.BlockSpec((tm, d), lambda i: (i, 0)),
                  pl.BlockSpec((1, d), lambda i: (0, 0))],
        out_specs=pl.BlockSpec((tm, d), lambda i: (i, 0)),
        out_shape=jax.ShapeDtypeStruct((m, d), BF16),
        compiler_params=_params(("arbitrary",), 2 * tm * d * 6 + 3 * tm * d * 4),
        name="norm_rows",
    )(x, g.reshape(1, d))


def _resid_norm_body(h_ref, x_ref, ga_ref, gb_ref, xo_ref, no_ref):
    xn = x_ref[...] + _rms(h_ref[...], ga_ref[...])
    xo_ref[...] = xn
    no_ref[...] = _rms(xn, gb_ref[...]).astype(no_ref.dtype)


def _resid_body(h_ref, x_ref, ga_ref, xo_ref):
    xo_ref[...] = x_ref[...] + _rms(h_ref[...], ga_ref[...])


def resid_norm(h, x, g_a, g_b=None, *, tm=128):
    m, d = x.shape
    tm = min(tm, m)
    row = pl.BlockSpec((tm, d), lambda i: (i, 0))
    gain = pl.BlockSpec((1, d), lambda i: (0, 0))
    if g_b is None:
        return pl.pallas_call(
            _resid_body,
            grid=(m // tm,),
            in_specs=[row, row, gain],
            out_specs=row,
            out_shape=jax.ShapeDtypeStruct((m, d), F32),
            compiler_params=_params(("arbitrary",), 2 * tm * d * 12 + 3 * tm * d * 4),
            name="resid",
        )(h, x, g_a.reshape(1, d))
    return pl.pallas_call(
        _resid_norm_body,
        grid=(m // tm,),
        in_specs=[row, row, gain, gain],
        out_specs=[row, row],
        out_shape=[jax.ShapeDtypeStruct((m, d), F32), jax.ShapeDtypeStruct((m, d), BF16)],
        compiler_params=_params(("arbitrary",), 2 * tm * d * 14 + 4 * tm * d * 4),
        name="resid_norm",
    )(h, x, g_a.reshape(1, d), g_b.reshape(1, d))


_HALO = 8


def _ffn_in_body(x_ref, wg_ref, wu_ref, cw_ref, cb_ref, o_ref, wgb_ref, wub_ref, gs_ref, *,
                 bm, cm):
    i = pl.program_id(1)

    @pl.when(i == 0)
    def _():
        wgb_ref[...] = wg_ref[...].astype(BF16)
        wub_ref[...] = wu_ref[...].astype(BF16)
        gs_ref[0:_HALO, :] = jnp.zeros((_HALO, gs_ref.shape[1]), F32)

    cw = cw_ref[...]
    cb = cb_ref[...]
    for c in range(bm // cm):
        x = x_ref[c * cm:(c + 1) * cm, :]
        g = jnp.dot(x, wgb_ref[...], preferred_element_type=F32)
        u = jnp.dot(x, wub_ref[...], preferred_element_type=F32)
        lo = _HALO + c * cm
        gs_ref[lo:lo + cm, :] = g
        g1 = gs_ref[lo - 1:lo - 1 + cm, :]
        g2 = gs_ref[lo - 2:lo - 2 + cm, :]
        gate = cw[0:1, :] * g2 + cw[1:2, :] * g1 + cw[2:3, :] * g + cb
        o_ref[c * cm:(c + 1) * cm, :] = ((gate / (1.0 + jnp.exp(-gate))) * u).astype(o_ref.dtype)
    gs_ref[0:_HALO, :] = gs_ref[bm:bm + _HALO, :]


def ffn_in(xn, w_in, layer, conv_w, conv_b, *, bm=1024, bn=V7X_MXU_DIM, cm=256):
    m, k = xn.shape
    d_ff = w_in.shape[2] // 2
    bm = min(bm, m)
    bn = min(bn, d_ff)
    assert m % bm == 0 and d_ff % bn == 0
    nj = d_ff // bn
    cm = min(cm, bm)
    assert bm % cm == 0
    vmem = (2 * bm * k * 2 + 2 * 2 * k * bn * 4 + 2 * k * bn * 2 + 2 * bm * bn * 2
            + (bm + _HALO) * bn * 4 + 6 * bm * bn * 4)
    return pl.pallas_call(
        functools.partial(_ffn_in_body, bm=bm, cm=cm),
        grid=(nj, m // bm),
        in_specs=[pl.BlockSpec((bm, k), lambda j, i: (i, 0)),
                  pl.BlockSpec((None, k, bn), lambda j, i: (layer, 0, j)),
                  pl.BlockSpec((None, k, bn), lambda j, i: (layer, 0, nj + j)),
                  pl.BlockSpec((CONV_W, bn), lambda j, i: (0, j)),
                  pl.BlockSpec((1, bn), lambda j, i: (0, j))],
        out_specs=pl.BlockSpec((bm, bn), lambda j, i: (i, j)),
        out_shape=jax.ShapeDtypeStruct((m, d_ff), BF16),
        scratch_shapes=[pltpu.VMEM((k, bn), BF16), pltpu.VMEM((k, bn), BF16),
                        pltpu.VMEM((bm + _HALO, bn), F32)],
        compiler_params=_params(("arbitrary", "arbitrary"), vmem),
        name="ffn_in",
    )(xn, w_in, w_in, conv_w, conv_b.reshape(1, d_ff))


def ffn_out(act, w_out, layer, *, bm=1024, bn=V7X_MXU_DIM):
    d_ff = act.shape[1]
    if d_ff % (2 * V7X_LANES) != 0:
        return matmul(act, w_out, bm=bm, bn=bn, out_dtype=F32, layer=layer)
    half = d_ff // 2
    part = matmul(act, w_out, bm=bm, bn=bn, out_dtype=F32, layer=layer, k_slice=(0, half),
                  resident=False)
    return matmul(act, w_out, bm=bm, bn=bn, out_dtype=F32, layer=layer, k_slice=(1, half),
                  resident=False, add=part)


def _sb_attn_body(q_ref, k_ref, v_ref, tri_ref, o_ref, acc_ref, c_ref, *, blk, scale):
    qi = pl.program_id(1)
    q = q_ref[...]
    tri = tri_ref[...]
    acc_ref[...] = jnp.zeros(acc_ref.shape, F32)
    c_ref[...] = jnp.zeros(c_ref.shape, F32)
    row = lax.broadcasted_iota(jnp.int32, (blk, blk), 0)
    col = lax.broadcasted_iota(jnp.int32, (blk, blk), 1)
    causal = col < row

    def block(kb, diagonal):
        start = pl.multiple_of(kb * blk, blk)
        z = lax.dot_general(q, k_ref[pl.ds(start, blk), :], _NT,
                            preferred_element_type=F32) * scale
        sp = jnp.log(1.0 + jnp.exp(-jnp.abs(z)))
        log_beta = jnp.minimum(z, 0.0) - sp
        log_1mb = jnp.minimum(-z, 0.0) - sp
        if diagonal:
            log_1mb = jnp.where(causal, log_1mb, 0.0)
        hi = log_1mb.astype(BF16)
        lo = (log_1mb - hi.astype(F32)).astype(BF16)
        tail = (jnp.dot(hi, tri, preferred_element_type=F32)
                + jnp.dot(lo, tri, preferred_element_type=F32))
        c = c_ref[...]
        a = jnp.exp(log_beta + tail + c)
        if diagonal:
            a = jnp.where(causal, a, 0.0)
        acc_ref[...] += jnp.dot(a.astype(BF16), v_ref[pl.ds(start, blk), :],
                                preferred_element_type=F32)
        c_new = c + tail[:, 0:1] + log_1mb[:, 0:1]
        c_ref[...] = c_new
        return jnp.max(c_new)

    top = block(qi, True)
    lax.while_loop(lambda st: (st[0] >= 0) & (st[1] > EXP_ZERO_BELOW),
                   lambda st: (st[0] - 1, block(st[0], False)),
                   (qi - 1, top))
    o_ref[...] = acc_ref[...].astype(o_ref.dtype)


def sb_attention(qkv, *, blk=256):
    s = qkv.shape[0]
    h, dh = SB_HEADS, SB_HEAD_DIM
    blk = min(blk, s)
    tri = (np.arange(blk)[:, None] > np.arange(blk)[None, :]).astype(np.float32)
    vmem = 2 * 2 * s * dh * 2 + 4 * blk * dh * 2 + 2 * blk * blk * 2 + 12 * blk * blk * 4
    return pl.pallas_call(
        functools.partial(_sb_attn_body, blk=blk, scale=dh ** -0.5),
        grid=(h, s // blk),
        in_specs=[pl.BlockSpec((blk, dh), lambda hh, i: (i, hh)),
                  pl.BlockSpec((s, dh), lambda hh, i: (0, h + hh)),
                  pl.BlockSpec((s, dh), lambda hh, i: (0, 2 * h + hh)),
                  pl.BlockSpec((blk, blk), lambda hh, i: (0, 0))],
        out_specs=pl.BlockSpec((blk, dh), lambda hh, i: (i, hh)),
        out_shape=jax.ShapeDtypeStruct((s, h * dh), BF16),
        scratch_shapes=[pltpu.VMEM((blk, dh), F32), pltpu.VMEM((blk, 1), F32)],
        compiler_params=_params(("arbitrary", "arbitrary"), vmem),
        name="sb_attention",
    )(qkv, qkv, qkv, jnp.asarray(tri, BF16))


def _mla_prep_body(y_ref, pos_ref, inv_ref, gq_ref, gkv_ref, cq_ref, ckv_ref, kr_ref, cs_ref):
    ql, kvl = MLA_Q_LORA, MLA_KV_LORA
    cq_ref[...] = _rms(y_ref[:, 0:ql], gq_ref[...]).astype(cq_ref.dtype)
    ckv_ref[...] = _rms(y_ref[:, ql:ql + kvl], gkv_ref[...]).astype(ckv_ref.dtype)
    ang = pos_ref[...] * inv_ref[...]
    lane = lax.broadcasted_iota(jnp.int32, ang.shape, 1)
    cs = jnp.where(lane < MLA_ROPE, jnp.cos(ang), jnp.sin(ang))
    cs_ref[...] = cs
    w = y_ref[:, ql + kvl:ql + kvl + V7X_LANES] * cs
    kr_ref[...] = (w + pltpu.roll(w, MLA_ROPE, axis=1)).astype(kr_ref.dtype)


def mla_prep(y, pos_col, g_q, g_kv, *, tm=256):
    m, width = y.shape
    tm = min(tm, m)
    half = MLA_ROPE // 2
    inv = ROPE_THETA ** (-jnp.arange(half, dtype=F32) / half)
    inv4 = jnp.tile(inv, 4).reshape(1, V7X_LANES)
    row = lambda w: pl.BlockSpec((tm, w), lambda i: (i, 0))
    const = lambda w: pl.BlockSpec((1, w), lambda i: (0, 0))
    return pl.pallas_call(
        _mla_prep_body,
        grid=(m // tm,),
        in_specs=[row(width), row(1), const(V7X_LANES), const(MLA_Q_LORA), const(MLA_KV_LORA)],
        out_specs=[row(MLA_Q_LORA), row(MLA_KV_LORA), row(V7X_LANES), row(V7X_LANES)],
        out_shape=[jax.ShapeDtypeStruct((m, MLA_Q_LORA), BF16),
                   jax.ShapeDtypeStruct((m, MLA_KV_LORA), BF16),
                   jax.ShapeDtypeStruct((m, V7X_LANES), BF16),
                   jax.ShapeDtypeStruct((m, V7X_LANES), F32)],
        compiler_params=_params(("arbitrary",), 6 * tm * width * 4),
        name="mla_prep",
    )(y, pos_col, inv4, g_q.reshape(1, -1), g_kv.reshape(1, -1))


_MLA_HEADS_PER_STEP = 2


def _mla_attn_body(q_ref, kn_ref, kr_ref, vt_ref, o_ref, m_ref, l_ref, acc_ref, *, blk, scale):
    qi = pl.program_id(1)
    nh = _MLA_HEADS_PER_STEP
    wq = MLA_NOPE + V7X_LANES
    m_ref[...] = jnp.full(m_ref.shape, MASKED, F32)
    l_ref[...] = jnp.zeros(l_ref.shape, F32)
    acc_ref[...] = jnp.zeros(acc_ref.shape, F32)
    key = lax.broadcasted_iota(jnp.int32, (blk, blk), 0)
    qry = lax.broadcasted_iota(jnp.int32, (blk, blk), 1)

    def step(kb, diagonal):
        start = pl.multiple_of(kb * blk, blk)
        kr = kr_ref[pl.ds(start, blk), :]
        for hd in range(nh):
            kcat = jnp.concatenate(
                [kn_ref[pl.ds(start, blk), hd * MLA_NOPE:(hd + 1) * MLA_NOPE], kr], axis=1)
            st = lax.dot_general(kcat, q_ref[:, hd * wq:(hd + 1) * wq], _NT,
                                 preferred_element_type=F32) * scale
            if diagonal:
                st = jnp.where(key <= qry, st, MASKED)
            m_prev = m_ref[hd]
            m_new = jnp.maximum(m_prev, jnp.max(st, axis=0, keepdims=True))
            p = jnp.exp(st - m_new)
            alpha = jnp.exp(m_prev - m_new)
            l_ref[hd] = alpha * l_ref[hd] + jnp.sum(p, axis=0, keepdims=True)
            acc_ref[hd] = alpha * acc_ref[hd] + jnp.dot(
                vt_ref[kb, hd * MLA_V:(hd + 1) * MLA_V, :], p.astype(BF16),
                preferred_element_type=F32)
            m_ref[hd] = m_new

    def off_diagonal(kb, carry):
        step(kb, False)
        return carry

    lax.fori_loop(0, qi, off_diagonal, 0)
    step(qi, True)
    for hd in range(nh):
        o_ref[:, hd * MLA_V:(hd + 1) * MLA_V] = (acc_ref[hd] / l_ref[hd]).T.astype(o_ref.dtype)


def mla_attention(q, kn, kr, vt, *, blk):
    s = q.shape[0]
    h = MLA_HEADS
    nh = _MLA_HEADS_PER_STEP
    wq = MLA_NOPE + V7X_LANES
    assert vt.shape == (s // blk, h * MLA_V, blk)
    vmem = (2 * blk * nh * wq * 2 + 2 * s * nh * MLA_NOPE * 2 + 2 * s * V7X_LANES * 2
            + 2 * s * nh * MLA_V * 2 + 2 * blk * nh * MLA_V * 2 + nh * (MLA_V + 16) * blk * 4
            + nh * 6 * blk * blk * 4)
    return pl.pallas_call(
        functools.partial(_mla_attn_body, blk=blk, scale=(MLA_NOPE + MLA_ROPE) ** -0.5),
        grid=(h // nh, s // blk),
        in_specs=[pl.BlockSpec((blk, nh * wq), lambda hp, i: (i, hp)),
                  pl.BlockSpec((s, nh * MLA_NOPE), lambda hp, i: (0, hp)),
                  pl.BlockSpec((s, V7X_LANES), lambda hp, i: (0, 0)),
                  pl.BlockSpec((s // blk, nh * MLA_V, blk), lambda hp, i: (0, hp, 0))],
        out_specs=pl.BlockSpec((blk, nh * MLA_V), lambda hp, i: (i, hp)),
        out_shape=jax.ShapeDtypeStruct((s, h * MLA_V), BF16),
        scratch_shapes=[pltpu.VMEM((nh, 1, blk), F32), pltpu.VMEM((nh, 1, blk), F32),
                        pltpu.VMEM((nh, MLA_V, blk), F32)],
        compiler_params=_params(("arbitrary", "arbitrary"), vmem),
        name="mla_attention",
    )(q, kn, kr, vt)


def _band_attn_body(*refs, tile, rep, window, scale, has_sink, want_lse):
    refs = list(refs)
    slope_ref = refs.pop(0)
    sink_ref = refs.pop(0) if has_sink else None
    q_ref, k_ref, v_ref, pq_ref, pk_ref, o_ref = refs[:6]
    lse_ref = refs[6] if want_lse else None
    hbase = pl.program_id(1) * rep
    t = pl.program_id(2)
    nb = tile // BLK
    qi = lax.broadcasted_iota(jnp.int32, (BLK, 2 * BLK), 0)
    kj = lax.broadcasted_iota(jnp.int32, (BLK, 2 * BLK), 1)
    for b in range(nb):
        gb = t * nb + b
        pb = jnp.maximum(gb - 1, 0)
        start = pl.multiple_of(pb * BLK, BLK)
        kwin = k_ref[pl.ds(start, 2 * BLK), :]
        vwin = v_ref[pl.ds(start, 2 * BLK), :]
        pk = jnp.concatenate([pk_ref[pb], pk_ref[pb + 1]], axis=1)
        dist = pq_ref[b * BLK:(b + 1) * BLK, :] - pk
        delta = (gb - pb) * BLK + qi - kj
        valid = (delta >= 0) & (delta <= window)
        for r in range(rep):
            rows = slice(b * BLK, (b + 1) * BLK)
            cols = slice(r * BLK, (r + 1) * BLK)
            sc = lax.dot_general(q_ref[rows, cols], kwin, _NT, preferred_element_type=F32) * scale
            sc = sc - slope_ref[hbase + r] * dist
            sc = jnp.where(valid, sc, MASKED)
            m = jnp.max(sc, axis=1, keepdims=True)
            if has_sink:
                sk = sink_ref[hbase + r]
                m = jnp.maximum(m, sk)
            p = jnp.exp(sc - m)
            den = jnp.sum(p, axis=1, keepdims=True)
            if has_sink:
                den = den + jnp.exp(sk - m)
            o = jnp.dot((p * (1.0 / den)).astype(BF16), vwin, preferred_element_type=F32)
            o_ref[rows, cols] = o.astype(o_ref.dtype)
            if want_lse:
                lse_ref[rows, cols] = jnp.broadcast_to(m + jnp.log(den), (BLK, BLK))


def band_attention(arr, col_q, col_k, col_v, pos_col, pos_row, slopes, sinks, *,
                   n_seq, n_kv, rep, seq_len, window, out_dtype, want_lse):
    dh = BLK
    tile = min(seq_len, 4 * BLK)
    tiles = seq_len // tile
    has_sink = sinks is not None
    smem = pl.BlockSpec(memory_space=pltpu.SMEM)
    in_specs = [smem] + ([smem] if has_sink else []) + [
        pl.BlockSpec((tile, rep * dh), lambda r, g, t: (r * tiles + t, col_q(g))),
        pl.BlockSpec((seq_len, dh), lambda r, g, t: (r, col_k(g))),
        pl.BlockSpec((seq_len, dh), lambda r, g, t: (r, col_v(g))),
        pl.BlockSpec((None, tile, 1), lambda r, g, t: (r, t, 0)),
        pl.BlockSpec((None, seq_len // BLK, 1, BLK), lambda r, g, t: (r, 0, 0, 0)),
    ]
    args = [slopes] + ([sinks] if has_sink else []) + [arr, arr, arr, pos_col, pos_row]
    out_block = pl.BlockSpec((tile, rep * dh), lambda r, g, t: (t, r * n_kv + g))
    out_cols = n_seq * n_kv * rep * dh
    out_sds = jax.ShapeDtypeStruct((seq_len, out_cols), out_dtype)
    osize = jnp.dtype(out_dtype).itemsize
    vmem = (2 * tile * rep * dh * 2 + 2 * 2 * seq_len * dh * 2 + 4 * tile * rep * dh * osize
            + 2 * tile * V7X_LANES * 4 + 2 * seq_len * 4 * 8 + 16 * BLK * 2 * BLK * 4)
    return pl.pallas_call(
        functools.partial(_band_attn_body, tile=tile, rep=rep, window=window,
                          scale=dh ** -0.5, has_sink=has_sink, want_lse=want_lse),
        grid=(n_seq, n_kv, seq_len // tile),
        in_specs=in_specs,
        out_specs=[out_block, out_block] if want_lse else out_block,
        out_shape=[out_sds, jax.ShapeDtypeStruct((seq_len, out_cols), F32)] if want_lse else out_sds,
        compiler_params=_params(("arbitrary", "arbitrary", "arbitrary"), vmem),
        name="band_attention",
    )(*args)


def _alibi_slopes(n):
    return jnp.asarray(2.0 ** (-8.0 * np.arange(1, n + 1) / n), dtype=F32)


def _strided_positions(pos_f32, dil):
    s = pos_f32.shape[0]
    ps = pos_f32.reshape(s // dil, dil).T
    return ps.reshape(dil, s // dil, 1), ps.reshape(dil, s // dil // BLK, 1, BLK)


def _dil_merge_body(o0, o1, o2, l0, l1, l2, out_ref):
    a, b, c = l0[...], l1[...], l2[...]
    m = jnp.maximum(jnp.maximum(a, b), c)
    ea, eb, ec = jnp.exp(a - m), jnp.exp(b - m), jnp.exp(c - m)
    inv = 1.0 / (ea + eb + ec)
    out_ref[...] = ((ea * inv) * o0[...] + (eb * inv) * o1[...] + (ec * inv) * o2[...]
                    ).astype(out_ref.dtype)


def dil_merge(outs, lses, *, tm=256):
    m, d = outs[0].shape
    tm = min(tm, m)
    row = pl.BlockSpec((tm, d), lambda i: (i, 0))
    return pl.pallas_call(
        _dil_merge_body,
        grid=(m // tm,),
        in_specs=[row] * 6,
        out_specs=row,
        out_shape=jax.ShapeDtypeStruct((m, d), BF16),
        compiler_params=_params(("arbitrary",), 2 * 6 * tm * d * 4 + 8 * tm * d * 4),
        name="dil_merge",
    )(*outs, *lses)


def stick_breaking_mixer(hin, w_qkv, w_o, j):
    qkv = matmul(hin, w_qkv, bm=1024, bn=512, out_dtype=BF16, layer=j)
    o = sb_attention(qkv)
    return matmul(o, w_o, bm=1024, bn=512, out_dtype=F32, layer=j)


def mla_mixer(hin, pos_f32, w_dq, g_q, w_uq, w_dkv, g_kv, w_ukv, w_o, j):
    h, nope, rope = MLA_HEADS, MLA_NOPE, MLA_ROPE
    half = rope // 2
    w_dq, w_uq, w_dkv, g_q, g_kv = w_dq[j], w_uq[j], w_dkv[j], g_q[j], g_kv[j]
    kx1 = w_dkv[:, MLA_KV_LORA:MLA_KV_LORA + half]
    kx2 = w_dkv[:, MLA_KV_LORA + half:]
    w_down = jnp.concatenate(
        [w_dq, w_dkv[:, :MLA_KV_LORA], kx1, kx2, -kx2, kx1,
         jnp.zeros((w_dq.shape[0], V7X_LANES), w_dq.dtype)], axis=1)
    y = matmul(hin, w_down[None], bm=1024, bn=V7X_MXU_DIM, out_dtype=F32)
    cq, ckv, kr, cs = mla_prep(y, pos_f32.reshape(-1, 1), g_q, g_kv)
    wq = w_uq.reshape(MLA_Q_LORA, h, nope + rope)
    qx1, qx2 = wq[:, :, nope:nope + half], wq[:, :, nope + half:]
    wq = jnp.concatenate([wq[:, :, :nope], qx1, qx2, -qx2, qx1], axis=2)
    q = matmul(cq, wq.reshape(1, MLA_Q_LORA, h * 2 * V7X_LANES), bm=1024, bn=1024,
               out_dtype=BF16, rope_cs=cs)
    wkv = w_ukv[j].reshape(MLA_KV_LORA, h, nope + MLA_V)
    w_k = wkv[:, :, :nope].reshape(1, MLA_KV_LORA, h * nope)
    w_vt = wkv[:, :, nope:].reshape(MLA_KV_LORA, h * MLA_V).T
    blk = min(512, hin.shape[0])
    kn = matmul(ckv, w_k, bm=1024, bn=1024, out_dtype=BF16)
    vt = matmul_nt_tiled(w_vt, ckv, bm=blk, bn=1024, out_dtype=BF16)
    o = mla_attention(q, kn, kr, vt, blk=blk)
    return matmul(o, w_o, bm=1024, bn=512, out_dtype=F32, layer=j)


def swa_mixer(hin, pos_f32, w_qkv, sinks, w_o, j):
    s = hin.shape[0]
    rep = SWA_HEADS // SWA_KV_HEADS
    sinks = sinks[j]
    qkv = matmul(hin, w_qkv, bm=1024, bn=512, out_dtype=BF16, layer=j)
    o = band_attention(
        qkv,
        lambda g: g,
        lambda g: SWA_HEADS + g,
        lambda g: SWA_HEADS + SWA_KV_HEADS + g,
        pos_f32.reshape(1, s, 1), pos_f32.reshape(1, s // BLK, 1, BLK),
        _alibi_slopes(SWA_HEADS), sinks,
        n_seq=1, n_kv=SWA_KV_HEADS, rep=rep, seq_len=s, window=SWA_WINDOW - 1,
        out_dtype=BF16, want_lse=False)
    return matmul(o, w_o, bm=1024, bn=512, out_dtype=F32, layer=j)


def dilated_mixer(hin, pos_f32, w_qkv, w_o, j):
    s, d_model = hin.shape
    nh = DIL_HEADS
    n_groups = len(DIL_PATTERNS)
    gw = 3 * nh * DIL_HEAD_DIM
    slopes_all = _alibi_slopes(n_groups * nh)
    outs, lses = [], []
    for gi, (win, dil) in enumerate(DIL_PATTERNS):
        pos_col, pos_row = _strided_positions(pos_f32, dil)
        qkv = matmul(hin.reshape(s // dil, dil * d_model), w_qkv, bm=1024, bn=512, out_dtype=BF16,
                     layer=j, n_slice=(gi * gw, gw), row_groups=dil)
        o, lse = band_attention(
            qkv,
            lambda g: g,
            lambda g: nh + g,
            lambda g: 2 * nh + g,
            pos_col, pos_row, slopes_all[gi * nh:(gi + 1) * nh], None,
            n_seq=dil, n_kv=nh, rep=1, seq_len=s // dil, window=win // dil,
            out_dtype=F32, want_lse=True)
        outs.append(o.reshape(s, nh * DIL_HEAD_DIM))
        lses.append(lse.reshape(s, nh * DIL_HEAD_DIM))
    o = dil_merge(outs, lses)
    return matmul(o, w_o, bm=1024, bn=512, out_dtype=F32, layer=j)


def kernel(x, positions, norm_g, a_w_qkv, a_w_o, b_w_dq, b_g_q, b_w_uq, b_w_dkv, b_g_kv,
           b_w_ukv, b_w_o, c_w_qkv, c_sinks, c_w_o, d_w_qkv, d_w_o, ffn_w_in, ffn_conv_w,
           ffn_conv_b, ffn_w_out):
    batch, s, d = x.shape
    depth = norm_g.shape[0]
    n_mixers = 4
    outs = []
    for bi in range(batch):
        xb = x[bi]
        pos_f32 = positions[bi].astype(F32)
        hin = norm_rows(xb, norm_g[0, 0])
        for i in range(depth):
            mixer, j = i % n_mixers, i // n_mixers
            if mixer == 0:
                h = stick_breaking_mixer(hin, a_w_qkv, a_w_o, j)
            elif mixer == 1:
                h = mla_mixer(hin, pos_f32, b_w_dq, b_g_q, b_w_uq, b_w_dkv, b_g_kv, b_w_ukv,
                              b_w_o, j)
            elif mixer == 2:
                h = swa_mixer(hin, pos_f32, c_w_qkv, c_sinks, c_w_o, j)
            else:
                h = dilated_mixer(hin, pos_f32, d_w_qkv, d_w_o, j)
            xb, xn = resid_norm(h, xb, norm_g[i, 1], norm_g[i, 2])
            act = ffn_in(xn, ffn_w_in, i, ffn_conv_w[i], ffn_conv_b[i])
            h = ffn_out(act, ffn_w_out, i)
            if i + 1 < depth:
                xb, hin = resid_norm(h, xb, norm_g[i, 3], norm_g[i + 1, 0])
            else:
                xb = resid_norm(h, xb, norm_g[i, 3])
        outs.append(xb)
    return jnp.stack(outs, axis=0)
```

```python
import functools

import numpy as np
import jax
import jax.numpy as jnp
from jax import lax
from jax.experimental import pallas as pl
from jax.experimental.pallas import tpu as pltpu

F32 = jnp.float32
BF16 = jnp.bfloat16

V7X_VMEM_BYTES = 64 * 2**20
V7X_LANES = 128
V7X_MXU_DIM = 256

EPS = 1e-6
BLK = 128
SB_HEADS, SB_HEAD_DIM = 32, 128
MLA_HEADS, MLA_NOPE, MLA_ROPE, MLA_V = 32, 128, 64, 128
MLA_Q_LORA, MLA_KV_LORA = 1024, 512
ROPE_THETA = 10000.0
SWA_HEADS, SWA_KV_HEADS, SWA_HEAD_DIM, SWA_WINDOW = 32, 8, 128, 128
DIL_PATTERNS = ((128, 1), (512, 4), (2048, 16))
DIL_HEADS, DIL_HEAD_DIM = 16, 128
CONV_W = 3

MASKED = -1e30
EXP_ZERO_BELOW = -105.0

_NT = (((1,), (1,)), ((), ()))


def _params(semantics, vmem_bytes):
    limit = min(int(vmem_bytes) + (8 << 20), V7X_VMEM_BYTES - (6 << 20))
    return pltpu.CompilerParams(dimension_semantics=semantics, vmem_limit_bytes=limit)


def _rms(x, g):
    return (x * lax.rsqrt(jnp.mean(x * x, axis=-1, keepdims=True) + EPS)) * g


def _mm_body(*refs, mode, out_scale):
    x_ref, w_ref, o_ref = refs[0], refs[1], refs[-1]
    acc = jnp.dot(x_ref[...], w_ref[...].astype(BF16), preferred_element_type=F32)
    if mode == "plain":
        o_ref[...] = acc.astype(o_ref.dtype)
    elif mode == "add":
        o_ref[...] = (refs[2][...] + acc).astype(o_ref.dtype)
    else:
        cs = refs[2][...] * out_scale
        for c in range(acc.shape[1] // (2 * V7X_LANES)):
            lo = 2 * c * V7X_LANES
            o_ref[:, lo:lo + V7X_LANES] = (acc[:, lo:lo + V7X_LANES] * out_scale
                                           ).astype(o_ref.dtype)
            o_ref[:, lo + V7X_LANES:lo + 2 * V7X_LANES] = (
                acc[:, lo + V7X_LANES:lo + 2 * V7X_LANES] * cs).astype(o_ref.dtype)


def matmul(x, w, *, bm, bn, out_dtype, layer=0, k_slice=None, n_slice=None, row_groups=1,
           add=None, rope_cs=None, out_scale=1.0):
    n0, n = (0, w.shape[2]) if n_slice is None else n_slice
    if row_groups == 1:
        m, kx = x.shape
    else:
        assert k_slice is None
        m, kx = x.shape[0] * row_groups, x.shape[1] // row_groups
    kb, tk = (0, kx) if k_slice is None else k_slice
    bm, bn = min(bm, m // row_groups), min(bn, n)
    assert (m // row_groups) % bm == 0 and n % bn == 0 and n0 % bn == 0
    per_group = m // row_groups // bm
    in_specs = [pl.BlockSpec((bm, tk), lambda i, j: (i % per_group, i // per_group + kb)),
                pl.BlockSpec((None, tk, bn), lambda i, j: (layer, kb, n0 // bn + j))]
    args = [x, w]
    mode = "plain"
    aliases = {}
    extra = 0
    if add is not None:
        in_specs.append(pl.BlockSpec((bm, bn), lambda i, j: (i, j)))
        args.append(add)
        mode = "add"
        aliases = {2: 0}
        extra = 2 * bm * bn * add.dtype.itemsize
    if rope_cs is not None:
        in_specs.append(pl.BlockSpec((bm, V7X_LANES), lambda i, j: (i, 0)))
        args.append(rope_cs)
        mode = "rope"
    osize = jnp.dtype(out_dtype).itemsize
    vmem = (2 * bm * tk * 2 + 2 * tk * bn * w.dtype.itemsize + tk * bn * 2
            + 2 * bm * bn * osize + bm * bn * 4 + extra)
    return pl.pallas_call(
        functools.partial(_mm_body, mode=mode, out_scale=out_scale),
        grid=(m // bm, n // bn),
        in_specs=in_specs,
        out_specs=pl.BlockSpec((bm, bn), lambda i, j: (i, j)),
        out_shape=jax.ShapeDtypeStruct((m, n), out_dtype),
        input_output_aliases=aliases,
        compiler_params=_params(("arbitrary", "arbitrary"), vmem),
        name="matmul",
    )(*args)


def _mm_nt_body(w_ref, x_ref, o_ref):
    o_ref[...] = lax.dot_general(w_ref[...].astype(BF16), x_ref[...], _NT,
                                 preferred_element_type=F32).astype(o_ref.dtype)


def matmul_nt_tiled(w_t, x, *, bm, bn, out_dtype):
    n, k = w_t.shape
    m = x.shape[0]
    bm, bn = min(bm, m), min(bn, n)
    assert m % bm == 0 and n % bn == 0
    vmem = 3 * bn * k * 4 + 2 * bm * k * 2 + 2 * bn * bm * jnp.dtype(out_dtype).itemsize + bn * bm * 4
    return pl.pallas_call(
        _mm_nt_body,
        grid=(n // bn, m // bm),
        in_specs=[pl.BlockSpec((bn, k), lambda j, i: (j, 0)),
                  pl.BlockSpec((bm, k), lambda j, i: (i, 0))],
        out_specs=pl.BlockSpec((None, bn, bm), lambda j, i: (i, j, 0)),
        out_shape=jax.ShapeDtypeStruct((m // bm, n, bm), out_dtype),
        compiler_params=_params(("arbitrary", "arbitrary"), vmem),
        name="matmul_nt",
    )(w_t, x)


def _norm_body(x_ref, g_ref, o_ref):
    o_ref[...] = _rms(x_ref[...], g_ref[...]).astype(o_ref.dtype)


def norm_rows(x, g, *, tm=256):
    m, d = x.shape
    tm = min(tm, m)
    return pl.pallas_call(
        _norm_body,
        grid=(m // tm,),
        in_specs=[pl.BlockSpec((tm, d), lambda i: (i, 0)),
                  pl.BlockSpec((1, d), lambda i: (0, 0))],
        out_specs=pl.BlockSpec((tm, d), lambda i: (i, 0)),
        out_shape=jax.ShapeDtypeStruct((m, d), BF16),
        compiler_params=_params(("arbitrary",), 2 * tm * d * 6 + 3 * tm * d * 4),
        name="norm_rows",
    )(x, g.reshape(1, d))


def _resid_norm_body(h_ref, x_ref, ga_ref, gb_ref, xo_ref, no_ref):
    xn = x_ref[...] + _rms(h_ref[...], ga_ref[...])
    xo_ref[...] = xn
    no_ref[...] = _rms(xn, gb_ref[...]).astype(no_ref.dtype)


def _resid_body(h_ref, x_ref, ga_ref, xo_ref):
    xo_ref[...] = x_ref[...] + _rms(h_ref[...], ga_ref[...])


def resid_norm(h, x, g_a, g_b=None, *, tm=128):
    m, d = x.shape
    tm = min(tm, m)
    row = pl.BlockSpec((tm, d), lambda i: (i, 0))
    gain = pl.BlockSpec((1, d), lambda i: (0, 0))
    if g_b is None:
        return pl.pallas_call(
            _resid_body,
            grid=(m // tm,),
            in_specs=[row, row, gain],
            out_specs=row,
            out_shape=jax.ShapeDtypeStruct((m, d), F32),
            compiler_params=_params(("arbitrary",), 2 * tm * d * 12 + 3 * tm * d * 4),
            name="resid",
        )(h, x, g_a.reshape(1, d))
    return pl.pallas_call(
        _resid_norm_body,
        grid=(m // tm,),
        in_specs=[row, row, gain, gain],
        out_specs=[row, row],
        out_shape=[jax.ShapeDtypeStruct((m, d), F32), jax.ShapeDtypeStruct((m, d), BF16)],
        compiler_params=_params(("arbitrary",), 2 * tm * d * 14 + 4 * tm * d * 4),
        name="resid_norm",
    )(h, x, g_a.reshape(1, d), g_b.reshape(1, d))


_HALO = 8


def _ffn_in_body(x_ref, wg_ref, wu_ref, cw_ref, cb_ref, o_ref, wb_ref, gs_ref, *, bm, cm):
    i = pl.program_id(1)
    bn = o_ref.shape[1]

    @pl.when(i == 0)
    def _():
        wb_ref[:, 0:bn] = wg_ref[...].astype(BF16)
        wb_ref[:, bn:2 * bn] = wu_ref[...].astype(BF16)
        gs_ref[0:_HALO, :] = jnp.zeros((_HALO, gs_ref.shape[1]), F32)

    cw = cw_ref[...]
    cb = cb_ref[...]
    for c in range(bm // cm):
        gu = jnp.dot(x_ref[c * cm:(c + 1) * cm, :], wb_ref[...], preferred_element_type=F32)
        g = gu[:, 0:bn]
        u = gu[:, bn:2 * bn]
        lo = _HALO + c * cm
        gs_ref[lo:lo + cm, :] = g
        g1 = gs_ref[lo - 1:lo - 1 + cm, :]
        g2 = gs_ref[lo - 2:lo - 2 + cm, :]
        gate = cw[0:1, :] * g2 + cw[1:2, :] * g1 + cw[2:3, :] * g + cb
        o_ref[c * cm:(c + 1) * cm, :] = ((gate / (1.0 + jnp.exp(-gate))) * u).astype(o_ref.dtype)
    gs_ref[0:_HALO, :] = gs_ref[bm:bm + _HALO, :]


def ffn_in(xn, w_in, layer, conv_w, conv_b, *, bm=1024, bn=V7X_MXU_DIM, cm=512):
    m, k = xn.shape
    d_ff = w_in.shape[2] // 2
    bm = min(bm, m)
    bn = min(bn, d_ff)
    assert m % bm == 0 and d_ff % bn == 0
    nj = d_ff // bn
    cm = min(cm, bm)
    assert bm % cm == 0
    vmem = (2 * bm * k * 2 + 2 * 2 * k * bn * 4 + 2 * k * bn * 2 + 2 * bm * bn * 2
            + (bm + _HALO) * bn * 4 + 6 * bm * bn * 4)
    return pl.pallas_call(
        functools.partial(_ffn_in_body, bm=bm, cm=cm),
        grid=(nj, m // bm),
        in_specs=[pl.BlockSpec((bm, k), lambda j, i: (i, 0)),
                  pl.BlockSpec((None, k, bn), lambda j, i: (layer, 0, j)),
                  pl.BlockSpec((None, k, bn), lambda j, i: (layer, 0, nj + j)),
                  pl.BlockSpec((CONV_W, bn), lambda j, i: (0, j)),
                  pl.BlockSpec((1, bn), lambda j, i: (0, j))],
        out_specs=pl.BlockSpec((bm, bn), lambda j, i: (i, j)),
        out_shape=jax.ShapeDtypeStruct((m, d_ff), BF16),
        scratch_shapes=[pltpu.VMEM((k, 2 * bn), BF16), pltpu.VMEM((bm + _HALO, bn), F32)],
        compiler_params=_params(("arbitrary", "arbitrary"), vmem),
        name="ffn_in",
    )(xn, w_in, w_in, conv_w, conv_b.reshape(1, d_ff))


def ffn_out(act, w_out, layer, *, bm=1024, bn=V7X_MXU_DIM):
    d_ff = act.shape[1]
    if d_ff % (2 * V7X_LANES) != 0:
        return matmul(act, w_out, bm=bm, bn=bn, out_dtype=F32, layer=layer)
    half = d_ff // 2
    part = matmul(act, w_out, bm=bm, bn=bn, out_dtype=F32, layer=layer, k_slice=(0, half))
    return matmul(act, w_out, bm=bm, bn=bn, out_dtype=F32, layer=layer, k_slice=(1, half),
                  add=part)


_SB_HEADS_PER_STEP = 4


def _sb_attn_body(q_ref, k_ref, v_ref, tri_ref, o_ref, acc_ref, c_ref, *, blk, scale):
    qi = pl.program_id(1)
    nh = _SB_HEADS_PER_STEP
    dh = SB_HEAD_DIM
    tri = tri_ref[...]
    acc_ref[...] = jnp.zeros(acc_ref.shape, F32)
    c_ref[...] = jnp.zeros(c_ref.shape, F32)
    key = lax.broadcasted_iota(jnp.int32, (blk, blk), 0)
    qry = lax.broadcasted_iota(jnp.int32, (blk, blk), 1)
    causal = key < qry

    def block(kb, diagonal):
        start = pl.multiple_of(kb * blk, blk)
        heads = range(nh)
        cols = [slice(hd * dh, (hd + 1) * dh) for hd in heads]
        z = [lax.dot_general(k_ref[pl.ds(start, blk), cols[hd]], q_ref[:, cols[hd]], _NT,
                             preferred_element_type=F32) * scale for hd in heads]
        vt = [v_ref[pl.ds(start, blk), cols[hd]].astype(F32).T.astype(BF16) for hd in heads]
        log_beta, log_1mb, hi, lo = [], [], [], []
        for hd in heads:
            sp = jnp.log(1.0 + jnp.exp(-jnp.abs(z[hd])))
            log_beta.append(jnp.minimum(z[hd], 0.0) - sp)
            l1 = jnp.minimum(-z[hd], 0.0) - sp
            if diagonal:
                l1 = jnp.where(causal, l1, 0.0)
            log_1mb.append(l1)
            hi.append(l1.astype(BF16))
            lo.append((l1 - hi[hd].astype(F32)).astype(BF16))
        tail = [jnp.dot(tri, hi[hd], preferred_element_type=F32)
                + jnp.dot(tri, lo[hd], preferred_element_type=F32) for hd in heads]
        c = [c_ref[hd] for hd in heads]
        a = []
        for hd in heads:
            w = jnp.exp(log_beta[hd] + tail[hd] + c[hd])
            if diagonal:
                w = jnp.where(causal, w, 0.0)
            a.append(w.astype(BF16))
        top = None
        for hd in heads:
            acc_ref[hd] += jnp.dot(vt[hd], a[hd], preferred_element_type=F32)
            c_new = c[hd] + tail[hd][0:1, :] + log_1mb[hd][0:1, :]
            c_ref[hd] = c_new
            top = jnp.max(c_new) if top is None else jnp.maximum(top, jnp.max(c_new))
        return top

    top = block(qi, True)
    lax.while_loop(lambda st: (st[0] >= 0) & (st[1] > EXP_ZERO_BELOW),
                   lambda st: (st[0] - 1, block(st[0], False)),
                   (qi - 1, top))
    for hd in range(nh):
        o_ref[:, hd * dh:(hd + 1) * dh] = acc_ref[hd].T.astype(o_ref.dtype)


def sb_attention(qkv, *, blk=256):
    s = qkv.shape[0]
    h, dh = SB_HEADS, SB_HEAD_DIM
    nh = _SB_HEADS_PER_STEP
    blk = min(blk, s)
    groups = h // nh
    tri = (np.arange(blk)[:, None] < np.arange(blk)[None, :]).astype(np.float32)
    vmem = (2 * 2 * s * nh * dh * 2 + 4 * blk * nh * dh * 2 + 2 * blk * blk * 2
            + nh * (dh + 8) * blk * 4 + nh * 12 * blk * blk * 4)
    return pl.pallas_call(
        functools.partial(_sb_attn_body, blk=blk, scale=dh ** -0.5),
        grid=(groups, s // blk),
        in_specs=[pl.BlockSpec((blk, nh * dh), lambda g, i: (i, g)),
                  pl.BlockSpec((s, nh * dh), lambda g, i: (0, groups + g)),
                  pl.BlockSpec((s, nh * dh), lambda g, i: (0, 2 * groups + g)),
                  pl.BlockSpec((blk, blk), lambda g, i: (0, 0))],
        out_specs=pl.BlockSpec((blk, nh * dh), lambda g, i: (i, g)),
        out_shape=jax.ShapeDtypeStruct((s, h * dh), BF16),
        scratch_shapes=[pltpu.VMEM((nh, dh, blk), F32), pltpu.VMEM((nh, 1, blk), F32)],
        compiler_params=_params(("arbitrary", "arbitrary"), vmem),
        name="sb_attention",
    )(qkv, qkv, qkv, jnp.asarray(tri, BF16))


def _mla_prep_body(y_ref, pos_ref, inv_ref, gq_ref, gkv_ref, cq_ref, ckv_ref, kr_ref, cs_ref):
    ql, kvl = MLA_Q_LORA, MLA_KV_LORA
    cq_ref[...] = _rms(y_ref[:, 0:ql], gq_ref[...]).astype(cq_ref.dtype)
    ckv_ref[...] = _rms(y_ref[:, ql:ql + kvl], gkv_ref[...]).astype(ckv_ref.dtype)
    ang = pos_ref[...] * inv_ref[...]
    lane = lax.broadcasted_iota(jnp.int32, ang.shape, 1)
    cs = jnp.where(lane < MLA_ROPE, jnp.cos(ang), jnp.sin(ang))
    cs_ref[...] = cs
    w = y_ref[:, ql + kvl:ql + kvl + V7X_LANES] * cs
    kr_ref[...] = (w + pltpu.roll(w, MLA_ROPE, axis=1)).astype(kr_ref.dtype)


def mla_prep(y, pos_col, g_q, g_kv, *, tm=256):
    m, width = y.shape
    tm = min(tm, m)
    half = MLA_ROPE // 2
    inv = ROPE_THETA ** (-jnp.arange(half, dtype=F32) / half)
    inv4 = jnp.tile(inv, 4).reshape(1, V7X_LANES)
    row = lambda w: pl.BlockSpec((tm, w), lambda i: (i, 0))
    const = lambda w: pl.BlockSpec((1, w), lambda i: (0, 0))
    return pl.pallas_call(
        _mla_prep_body,
        grid=(m // tm,),
        in_specs=[row(width), row(1), const(V7X_LANES), const(MLA_Q_LORA), const(MLA_KV_LORA)],
        out_specs=[row(MLA_Q_LORA), row(MLA_KV_LORA), row(V7X_LANES), row(V7X_LANES)],
        out_shape=[jax.ShapeDtypeStruct((m, MLA_Q_LORA), BF16),
                   jax.ShapeDtypeStruct((m, MLA_KV_LORA), BF16),
                   jax.ShapeDtypeStruct((m, V7X_LANES), BF16),
                   jax.ShapeDtypeStruct((m, V7X_LANES), F32)],
        compiler_params=_params(("arbitrary",), 6 * tm * width * 4),
        name="mla_prep",
    )(y, pos_col, inv4, g_q.reshape(1, -1), g_kv.reshape(1, -1))


_MLA_HEADS_PER_STEP = 2


def _mla_attn_body(q_ref, kn_ref, kr_ref, vt_ref, o_ref, s_ref, m_ref, l_ref, acc_ref, *, blk):
    qi = pl.program_id(1)
    nh = _MLA_HEADS_PER_STEP
    wq = MLA_NOPE + V7X_LANES
    m_ref[...] = jnp.full(m_ref.shape, MASKED, F32)
    l_ref[...] = jnp.zeros(l_ref.shape, F32)
    acc_ref[...] = jnp.zeros(acc_ref.shape, F32)
    key = lax.broadcasted_iota(jnp.int32, (blk, blk), 0)
    qry = lax.broadcasted_iota(jnp.int32, (blk, blk), 1)

    def scores(kb, slot):
        start = pl.multiple_of(kb * blk, blk)
        kr = kr_ref[pl.ds(start, blk), :]
        for hd in range(nh):
            kcat = jnp.concatenate(
                [kn_ref[pl.ds(start, blk), hd * MLA_NOPE:(hd + 1) * MLA_NOPE], kr], axis=1)
            s_ref[slot, hd] = lax.dot_general(kcat, q_ref[:, hd * wq:(hd + 1) * wq], _NT,
                                              preferred_element_type=F32)

    def reduce(kb, slot, diagonal):
        heads = range(nh)
        st = [s_ref[slot, hd] for hd in heads]
        if diagonal:
            st = [jnp.where(key <= qry, t, MASKED) for t in st]
        m_prev = [m_ref[hd] for hd in heads]
        m_new = [jnp.maximum(m_prev[hd], jnp.max(st[hd], axis=0, keepdims=True)) for hd in heads]
        p = [jnp.exp2(st[hd] - m_new[hd]) for hd in heads]
        pv = [jnp.dot(vt_ref[kb, hd * MLA_V:(hd + 1) * MLA_V, :], p[hd].astype(BF16),
                      preferred_element_type=F32) for hd in heads]
        for hd in heads:
            alpha = jnp.exp2(m_prev[hd] - m_new[hd])
            l_ref[hd] = alpha * l_ref[hd] + jnp.sum(p[hd], axis=0, keepdims=True)
            acc_ref[hd] = alpha * acc_ref[hd] + pv[hd]
            m_ref[hd] = m_new[hd]

    def off_diagonal_pair(t, carry):
        scores(2 * t + 1, 1)
        reduce(2 * t, 0, False)
        scores(2 * t + 2, 0)
        reduce(2 * t + 1, 1, False)
        return carry

    scores(0, 0)
    lax.fori_loop(0, qi // 2, off_diagonal_pair, 0)

    @pl.when(qi % 2 == 0)
    def _():
        reduce(qi, 0, True)

    @pl.when(qi % 2 == 1)
    def _():
        scores(qi, 1)
        reduce(qi - 1, 0, False)
        reduce(qi, 1, True)

    for hd in range(nh):
        o_ref[:, hd * MLA_V:(hd + 1) * MLA_V] = (acc_ref[hd] / l_ref[hd]).T.astype(o_ref.dtype)


def mla_attention(q, kn, kr, vt, *, blk):
    s = q.shape[0]
    h = MLA_HEADS
    nh = _MLA_HEADS_PER_STEP
    wq = MLA_NOPE + V7X_LANES
    assert vt.shape == (s // blk, h * MLA_V, blk)
    vmem = (2 * blk * nh * wq * 2 + 2 * s * nh * MLA_NOPE * 2 + 2 * s * V7X_LANES * 2
            + 2 * s * nh * MLA_V * 2 + 2 * blk * nh * MLA_V * 2 + nh * (MLA_V + 16) * blk * 4
            + nh * 8 * blk * blk * 4)
    return pl.pallas_call(
        functools.partial(_mla_attn_body, blk=blk),
        grid=(h // nh, s // blk),
        in_specs=[pl.BlockSpec((blk, nh * wq), lambda hp, i: (i, hp)),
                  pl.BlockSpec((s, nh * MLA_NOPE), lambda hp, i: (0, hp)),
                  pl.BlockSpec((s, V7X_LANES), lambda hp, i: (0, 0)),
                  pl.BlockSpec((s // blk, nh * MLA_V, blk), lambda hp, i: (0, hp, 0))],
        out_specs=pl.BlockSpec((blk, nh * MLA_V), lambda hp, i: (i, hp)),
        out_shape=jax.ShapeDtypeStruct((s, h * MLA_V), BF16),
        scratch_shapes=[pltpu.VMEM((2, nh, blk, blk), F32),
                        pltpu.VMEM((nh, 1, blk), F32), pltpu.VMEM((nh, 1, blk), F32),
                        pltpu.VMEM((nh, MLA_V, blk), F32)],
        compiler_params=_params(("arbitrary", "arbitrary"), vmem),
        name="mla_attention",
    )(q, kn, kr, vt)


def _band_attn_body(*refs, tile, rep, window, scale, has_sink, want_lse):
    refs = list(refs)
    slope_ref = refs.pop(0)
    sink_ref = refs.pop(0) if has_sink else None
    q_ref, k_ref, v_ref, pq_ref, pk_ref, o_ref = refs[:6]
    lse_ref = refs[6] if want_lse else None
    hbase = pl.program_id(1) * rep
    t = pl.program_id(2)
    nb = tile // BLK
    qi = lax.broadcasted_iota(jnp.int32, (BLK, 2 * BLK), 0)
    kj = lax.broadcasted_iota(jnp.int32, (BLK, 2 * BLK), 1)
    for b in range(nb):
        gb = t * nb + b
        pb = jnp.maximum(gb - 1, 0)
        start = pl.multiple_of(pb * BLK, BLK)
        kwin = k_ref[pl.ds(start, 2 * BLK), :]
        vwin = v_ref[pl.ds(start, 2 * BLK), :]
        pk = jnp.concatenate([pk_ref[pb], pk_ref[pb + 1]], axis=1)
        dist = pq_ref[b * BLK:(b + 1) * BLK, :] - pk
        delta = (gb - pb) * BLK + qi - kj
        valid = (delta >= 0) & (delta <= window)
        for r in range(rep):
            rows = slice(b * BLK, (b + 1) * BLK)
            cols = slice(r * BLK, (r + 1) * BLK)
            sc = lax.dot_general(q_ref[rows, cols], kwin, _NT, preferred_element_type=F32) * scale
            sc = sc - slope_ref[hbase + r] * dist
            sc = jnp.where(valid, sc, MASKED)
            m = jnp.max(sc, axis=1, keepdims=True)
            if has_sink:
                sk = sink_ref[hbase + r]
                m = jnp.maximum(m, sk)
            p = jnp.exp(sc - m)
            den = jnp.sum(p, axis=1, keepdims=True)
            if has_sink:
                den = den + jnp.exp(sk - m)
            o = jnp.dot((p * (1.0 / den)).astype(BF16), vwin, preferred_element_type=F32)
            o_ref[rows, cols] = o.astype(o_ref.dtype)
            if want_lse:
                lse_ref[rows, cols] = jnp.broadcast_to(m + jnp.log(den), (BLK, BLK))


def band_attention(arr, col_q, col_k, col_v, pos_col, pos_row, slopes, sinks, *,
                   n_seq, n_kv, rep, seq_len, window, out_dtype, want_lse):
    dh = BLK
    tile = min(seq_len, 4 * BLK)
    tiles = seq_len // tile
    has_sink = sinks is not None
    smem = pl.BlockSpec(memory_space=pltpu.SMEM)
    in_specs = [smem] + ([smem] if has_sink else []) + [
        pl.BlockSpec((tile, rep * dh), lambda r, g, t: (r * tiles + t, col_q(g))),
        pl.BlockSpec((seq_len, dh), lambda r, g, t: (r, col_k(g))),
        pl.BlockSpec((seq_len, dh), lambda r, g, t: (r, col_v(g))),
        pl.BlockSpec((None, tile, 1), lambda r, g, t: (r, t, 0)),
        pl.BlockSpec((None, seq_len // BLK, 1, BLK), lambda r, g, t: (r, 0, 0, 0)),
    ]
    args = [slopes] + ([sinks] if has_sink else []) + [arr, arr, arr, pos_col, pos_row]
    out_block = pl.BlockSpec((tile, rep * dh), lambda r, g, t: (t, r * n_kv + g))
    out_cols = n_seq * n_kv * rep * dh
    out_sds = jax.ShapeDtypeStruct((seq_len, out_cols), out_dtype)
    osize = jnp.dtype(out_dtype).itemsize
    vmem = (2 * tile * rep * dh * 2 + 2 * 2 * seq_len * dh * 2 + 4 * tile * rep * dh * osize
            + 2 * tile * V7X_LANES * 4 + 2 * seq_len * 4 * 8 + 16 * BLK * 2 * BLK * 4)
    return pl.pallas_call(
        functools.partial(_band_attn_body, tile=tile, rep=rep, window=window,
                          scale=dh ** -0.5, has_sink=has_sink, want_lse=want_lse),
        grid=(n_seq, n_kv, seq_len // tile),
        in_specs=in_specs,
        out_specs=[out_block, out_block] if want_lse else out_block,
        out_shape=[out_sds, jax.ShapeDtypeStruct((seq_len, out_cols), F32)] if want_lse else out_sds,
        compiler_params=_params(("arbitrary", "arbitrary", "arbitrary"), vmem),
        name="band_attention",
    )(*args)


def _alibi_slopes(n):
    return jnp.asarray(2.0 ** (-8.0 * np.arange(1, n + 1) / n), dtype=F32)


def _strided_positions(pos_f32, dil):
    s = pos_f32.shape[0]
    ps = pos_f32.reshape(s // dil, dil).T
    return ps.reshape(dil, s // dil, 1), ps.reshape(dil, s // dil // BLK, 1, BLK)


def _dil_merge_body(o0, o1, o2, l0, l1, l2, out_ref):
    a, b, c = l0[...], l1[...], l2[...]
    m = jnp.maximum(jnp.maximum(a, b), c)
    ea, eb, ec = jnp.exp(a - m), jnp.exp(b - m), jnp.exp(c - m)
    inv = 1.0 / (ea + eb + ec)
    out_ref[...] = ((ea * inv) * o0[...] + (eb * inv) * o1[...] + (ec * inv) * o2[...]
                    ).astype(out_ref.dtype)


def dil_merge(outs, lses, *, tm=256):
    m, d = outs[0].shape
    tm = min(tm, m)
    row = pl.BlockSpec((tm, d), lambda i: (i, 0))
    return pl.pallas_call(
        _dil_merge_body,
        grid=(m // tm,),
        in_specs=[row] * 6,
        out_specs=row,
        out_shape=jax.ShapeDtypeStruct((m, d), BF16),
        compiler_params=_params(("arbitrary",), 2 * 6 * tm * d * 4 + 8 * tm * d * 4),
        name="dil_merge",
    )(*outs, *lses)


def stick_breaking_mixer(hin, w_qkv, w_o, j):
    qkv = matmul(hin, w_qkv, bm=1024, bn=512, out_dtype=BF16, layer=j)
    o = sb_attention(qkv)
    return matmul(o, w_o, bm=1024, bn=512, out_dtype=F32, layer=j)


def mla_mixer(hin, pos_f32, w_dq, g_q, w_uq, w_dkv, g_kv, w_ukv, w_o, j):
    h, nope, rope = MLA_HEADS, MLA_NOPE, MLA_ROPE
    half = rope // 2
    w_dq, w_uq, w_dkv, g_q, g_kv = w_dq[j], w_uq[j], w_dkv[j], g_q[j], g_kv[j]
    kx1 = w_dkv[:, MLA_KV_LORA:MLA_KV_LORA + half]
    kx2 = w_dkv[:, MLA_KV_LORA + half:]
    w_down = jnp.concatenate(
        [w_dq, w_dkv[:, :MLA_KV_LORA], kx1, kx2, -kx2, kx1,
         jnp.zeros((w_dq.shape[0], V7X_LANES), w_dq.dtype)], axis=1)
    y = matmul(hin, w_down[None], bm=1024, bn=V7X_MXU_DIM, out_dtype=F32)
    cq, ckv, kr, cs = mla_prep(y, pos_f32.reshape(-1, 1), g_q, g_kv)
    wq = w_uq.reshape(MLA_Q_LORA, h, nope + rope)
    qx1, qx2 = wq[:, :, nope:nope + half], wq[:, :, nope + half:]
    wq = jnp.concatenate([wq[:, :, :nope], qx1, qx2, -qx2, qx1], axis=2)
    q_scale = (nope + rope) ** -0.5 * float(np.log2(np.e))
    q = matmul(cq, wq.reshape(1, MLA_Q_LORA, h * 2 * V7X_LANES), bm=1024, bn=1024,
               out_dtype=BF16, rope_cs=cs, out_scale=q_scale)
    wkv = w_ukv[j].reshape(MLA_KV_LORA, h, nope + MLA_V)
    w_k = wkv[:, :, :nope].reshape(1, MLA_KV_LORA, h * nope)
    w_vt = wkv[:, :, nope:].reshape(MLA_KV_LORA, h * MLA_V).T
    blk = min(512, hin.shape[0])
    kn = matmul(ckv, w_k, bm=1024, bn=1024, out_dtype=BF16)
    vt = matmul_nt_tiled(w_vt, ckv, bm=blk, bn=1024, out_dtype=BF16)
    o = mla_attention(q, kn, kr, vt, blk=blk)
    return matmul(o, w_o, bm=1024, bn=512, out_dtype=F32, layer=j)


def swa_mixer(hin, pos_f32, w_qkv, sinks, w_o, j):
    s = hin.shape[0]
    rep = SWA_HEADS // SWA_KV_HEADS
    sinks = sinks[j]
    qkv = matmul(hin, w_qkv, bm=1024, bn=512, out_dtype=BF16, layer=j)
    o = band_attention(
        qkv,
        lambda g: g,
        lambda g: SWA_HEADS + g,
        lambda g: SWA_HEADS + SWA_KV_HEADS + g,
        pos_f32.reshape(1, s, 1), pos_f32.reshape(1, s // BLK, 1, BLK),
        _alibi_slopes(SWA_HEADS), sinks,
        n_seq=1, n_kv=SWA_KV_HEADS, rep=rep, seq_len=s, window=SWA_WINDOW - 1,
        out_dtype=BF16, want_lse=False)
    return matmul(o, w_o, bm=1024, bn=512, out_dtype=F32, layer=j)


def dilated_mixer(hin, pos_f32, w_qkv, w_o, j):
    s, d_model = hin.shape
    nh = DIL_HEADS
    n_groups = len(DIL_PATTERNS)
    gw = 3 * nh * DIL_HEAD_DIM
    slopes_all = _alibi_slopes(n_groups * nh)
    outs, lses = [], []
    for gi, (win, dil) in enumerate(DIL_PATTERNS):
        pos_col, pos_row = _strided_positions(pos_f32, dil)
        qkv = matmul(hin.reshape(s // dil, dil * d_model), w_qkv, bm=1024, bn=512, out_dtype=BF16,
                     layer=j, n_slice=(gi * gw, gw), row_groups=dil)
        o, lse = band_attention(
            qkv,
            lambda g: g,
            lambda g: nh + g,
            lambda g: 2 * nh + g,
            pos_col, pos_row, slopes_all[gi * nh:(gi + 1) * nh], None,
            n_seq=dil, n_kv=nh, rep=1, seq_len=s // dil, window=win // dil,
            out_dtype=F32, want_lse=True)
        outs.append(o.reshape(s, nh * DIL_HEAD_DIM))
        lses.append(lse.reshape(s, nh * DIL_HEAD_DIM))
    o = dil_merge(outs, lses)
    return matmul(o, w_o, bm=1024, bn=512, out_dtype=F32, layer=j)


def kernel(x, positions, norm_g, a_w_qkv, a_w_o, b_w_dq, b_g_q, b_w_uq, b_w_dkv, b_g_kv,
           b_w_ukv, b_w_o, c_w_qkv, c_sinks, c_w_o, d_w_qkv, d_w_o, ffn_w_in, ffn_conv_w,
           ffn_conv_b, ffn_w_out):
    batch, s, d = x.shape
    depth = norm_g.shape[0]
    n_mixers = 4
    outs = []
    for bi in range(batch):
        xb = x[bi]
        pos_f32 = positions[bi].astype(F32)
        hin = norm_rows(xb, norm_g[0, 0])
        for i in range(depth):
            mixer, j = i % n_mixers, i // n_mixers
            if mixer == 0:
                h = stick_breaking_mixer(hin, a_w_qkv, a_w_o, j)
            elif mixer == 1:
                h = mla_mixer(hin, pos_f32, b_w_dq, b_g_q, b_w_uq, b_w_dkv, b_g_kv, b_w_ukv,
                              b_w_o, j)
            elif mixer == 2:
                h = swa_mixer(hin, pos_f32, c_w_qkv, c_sinks, c_w_o, j)
            else:
                h = dilated_mixer(hin, pos_f32, d_w_qkv, d_w_o, j)
            xb, xn = resid_norm(h, xb, norm_g[i, 1], norm_g[i, 2])
            act = ffn_in(xn, ffn_w_in, i, ffn_conv_w[i], ffn_conv_b[i])
            h = ffn_out(act, ffn_w_out, i)
            if i + 1 < depth:
                xb, hin = resid_norm(h, xb, norm_g[i, 3], norm_g[i + 1, 0])
            else:
                xb = resid_norm(h, xb, norm_g[i, 3])
        outs.append(xb)
    return jnp.stack(outs, axis=0)
```

```python
import functools

import numpy as np
import jax
import jax.numpy as jnp
from jax import lax
from jax.experimental import pallas as pl
from jax.experimental.pallas import tpu as pltpu

F32 = jnp.float32
BF16 = jnp.bfloat16

V7X_VMEM_BYTES = 64 * 2**20
V7X_LANES = 128
V7X_MXU_DIM = 256

EPS = 1e-6
BLK = 128
SB_HEADS, SB_HEAD_DIM = 32, 128
MLA_HEADS, MLA_NOPE, MLA_ROPE, MLA_V = 32, 128, 64, 128
MLA_Q_LORA, MLA_KV_LORA = 1024, 512
ROPE_THETA = 10000.0
SWA_HEADS, SWA_KV_HEADS, SWA_HEAD_DIM, SWA_WINDOW = 32, 8, 128, 128
DIL_PATTERNS = ((128, 1), (512, 4), (2048, 16))
DIL_HEADS, DIL_HEAD_DIM = 16, 128
CONV_W = 3

MASKED = -1e30
EXP_ZERO_BELOW = -105.0

_NT = (((1,), (1,)), ((), ()))


def _params(semantics, vmem_bytes):
    limit = min(int(vmem_bytes) + (8 << 20), V7X_VMEM_BYTES - (6 << 20))
    return pltpu.CompilerParams(dimension_semantics=semantics, vmem_limit_bytes=limit)


def _rms(x, g):
    return (x * lax.rsqrt(jnp.mean(x * x, axis=-1, keepdims=True) + EPS)) * g


def _mm_body(*refs, mode, out_scale):
    x_ref, w_ref, o_ref = refs[0], refs[1], refs[-1]
    acc = jnp.dot(x_ref[...], w_ref[...].astype(BF16), preferred_element_type=F32)
    if mode == "plain":
        o_ref[...] = acc.astype(o_ref.dtype)
    elif mode == "add":
        o_ref[...] = (refs[2][...] + acc).astype(o_ref.dtype)
    else:
        cs = refs[2][...] * out_scale
        for c in range(acc.shape[1] // (2 * V7X_LANES)):
            lo = 2 * c * V7X_LANES
            o_ref[:, lo:lo + V7X_LANES] = (acc[:, lo:lo + V7X_LANES] * out_scale
                                           ).astype(o_ref.dtype)
            o_ref[:, lo + V7X_LANES:lo + 2 * V7X_LANES] = (
                acc[:, lo + V7X_LANES:lo + 2 * V7X_LANES] * cs).astype(o_ref.dtype)


def matmul(x, w, *, bm, bn, out_dtype, layer=0, k_slice=None, n_slice=None, row_groups=1,
           add=None, rope_cs=None, out_scale=1.0):
    n0, n = (0, w.shape[2]) if n_slice is None else n_slice
    if row_groups == 1:
        m, kx = x.shape
    else:
        assert k_slice is None
        m, kx = x.shape[0] * row_groups, x.shape[1] // row_groups
    kb, tk = (0, kx) if k_slice is None else k_slice
    bm, bn = min(bm, m // row_groups), min(bn, n)
    assert (m // row_groups) % bm == 0 and n % bn == 0 and n0 % bn == 0
    per_group = m // row_groups // bm
    in_specs = [pl.BlockSpec((bm, tk), lambda i, j: (i % per_group, i // per_group + kb)),
                pl.BlockSpec((None, tk, bn), lambda i, j: (layer, kb, n0 // bn + j))]
    args = [x, w]
    mode = "plain"
    aliases = {}
    extra = 0
    if add is not None:
        in_specs.append(pl.BlockSpec((bm, bn), lambda i, j: (i, j)))
        args.append(add)
        mode = "add"
        aliases = {2: 0}
        extra = 2 * bm * bn * add.dtype.itemsize
    if rope_cs is not None:
        in_specs.append(pl.BlockSpec((bm, V7X_LANES), lambda i, j: (i, 0)))
        args.append(rope_cs)
        mode = "rope"
    osize = jnp.dtype(out_dtype).itemsize
    vmem = (2 * bm * tk * 2 + 2 * tk * bn * w.dtype.itemsize + tk * bn * 2
            + 2 * bm * bn * osize + bm * bn * 4 + extra)
    return pl.pallas_call(
        functools.partial(_mm_body, mode=mode, out_scale=out_scale),
        grid=(m // bm, n // bn),
        in_specs=in_specs,
        out_specs=pl.BlockSpec((bm, bn), lambda i, j: (i, j)),
        out_shape=jax.ShapeDtypeStruct((m, n), out_dtype),
        input_output_aliases=aliases,
        compiler_params=_params(("arbitrary", "arbitrary"), vmem),
        name="matmul",
    )(*args)


def _mm_nt_body(w_ref, x_ref, o_ref):
    o_ref[...] = lax.dot_general(w_ref[...].astype(BF16), x_ref[...], _NT,
                                 preferred_element_type=F32).astype(o_ref.dtype)


def matmul_nt_tiled(w_t, x, *, bm, bn, out_dtype):
    n, k = w_t.shape
    m = x.shape[0]
    bm, bn = min(bm, m), min(bn, n)
    assert m % bm == 0 and n % bn == 0
    vmem = 3 * bn * k * 4 + 2 * bm * k * 2 + 2 * bn * bm * jnp.dtype(out_dtype).itemsize + bn * bm * 4
    return pl.pallas_call(
        _mm_nt_body,
        grid=(n // bn, m // bm),
        in_specs=[pl.BlockSpec((bn, k), lambda j, i: (j, 0)),
                  pl.BlockSpec((bm, k), lambda j, i: (i, 0))],
        out_specs=pl.BlockSpec((None, bn, bm), lambda j, i: (i, j, 0)),
        out_shape=jax.ShapeDtypeStruct((m // bm, n, bm), out_dtype),
        compiler_params=_params(("arbitrary", "arbitrary"), vmem),
        name="matmul_nt",
    )(w_t, x)


def _norm_body(x_ref, g_ref, o_ref):
    o_ref[...] = _rms(x_ref[...], g_ref[...]).astype(o_ref.dtype)


def norm_rows(x, g, *, tm=256):
    m, d = x.shape
    tm = min(tm, m)
    return pl.pallas_call(
        _norm_body,
        grid=(m // tm,),
        in_specs=[pl.BlockSpec((tm, d), lambda i: (i, 0)),
                  pl.BlockSpec((1, d), lambda i: (0, 0))],
        out_specs=pl.BlockSpec((tm, d), lambda i: (i, 0)),
        out_shape=jax.ShapeDtypeStruct((m, d), BF16),
        compiler_params=_params(("arbitrary",), 2 * tm * d * 6 + 3 * tm * d * 4),
        name="norm_rows",
    )(x, g.reshape(1, d))


def _resid_norm_body(h_ref, x_ref, ga_ref, gb_ref, xo_ref, no_ref, *rest, dils):
    xn = x_ref[...] + _rms(h_ref[...], ga_ref[...])
    xo_ref[...] = xn
    n = _rms(xn, gb_ref[...])
    no_ref[...] = n.astype(no_ref.dtype)
    if dils:
        n_scr = rest[-1]
        tm, dm = n.shape
        chunks = dm // V7X_LANES
        for c in range(chunks):
            n_scr[c] = n[:, c * V7X_LANES:(c + 1) * V7X_LANES]
        for d, ref in zip(dils, rest[:-1]):
            for r in range(d):
                for c in range(chunks):
                    lo = r * dm + c * V7X_LANES
                    ref[:, lo:lo + V7X_LANES] = n_scr[c, pl.ds(r, tm // d, stride=d), :].astype(
                        ref.dtype)


def _resid_body(h_ref, x_ref, ga_ref, xo_ref):
    xo_ref[...] = x_ref[...] + _rms(h_ref[...], ga_ref[...])


def resid_norm(h, x, g_a, g_b=None, *, tm=128, dils=()):
    m, d = x.shape
    if dils:
        tm = 16 * max(dils)
    tm = min(tm, m)
    row = pl.BlockSpec((tm, d), lambda i: (i, 0))
    gain = pl.BlockSpec((1, d), lambda i: (0, 0))
    if g_b is None:
        return pl.pallas_call(
            _resid_body,
            grid=(m // tm,),
            in_specs=[row, row, gain],
            out_specs=row,
            out_shape=jax.ShapeDtypeStruct((m, d), F32),
            compiler_params=_params(("arbitrary",), 2 * tm * d * 12 + 3 * tm * d * 4),
            name="resid",
        )(h, x, g_a.reshape(1, d))
    outs = pl.pallas_call(
        functools.partial(_resid_norm_body, dils=tuple(dils)),
        grid=(m // tm,),
        in_specs=[row, row, gain, gain],
        out_specs=[row, row] + [pl.BlockSpec((tm // dl, dl * d), lambda i: (i, 0)) for dl in dils],
        out_shape=[jax.ShapeDtypeStruct((m, d), F32), jax.ShapeDtypeStruct((m, d), BF16)]
        + [jax.ShapeDtypeStruct((m // dl, dl * d), BF16) for dl in dils],
        scratch_shapes=[pltpu.VMEM((d // V7X_LANES, tm, V7X_LANES), F32)] if dils else [],
        compiler_params=_params(("arbitrary",),
                                2 * tm * d * (14 + 2 * len(dils)) + 5 * tm * d * 4),
        name="resid_norm",
    )(h, x, g_a.reshape(1, d), g_b.reshape(1, d))
    return (outs[0], outs[1]) + ((tuple(outs[2:]),) if dils else ())


_HALO = 8


def _ffn_in_body(x_ref, wg_ref, wu_ref, cw_ref, cb_ref, o_ref, wb_ref, gs_ref, *, bm, cm):
    i = pl.program_id(1)
    bn = o_ref.shape[1]

    @pl.when(i == 0)
    def _():
        wb_ref[:, 0:bn] = wg_ref[...].astype(BF16)
        wb_ref[:, bn:2 * bn] = wu_ref[...].astype(BF16)
        gs_ref[0:_HALO, :] = jnp.zeros((_HALO, gs_ref.shape[1]), F32)

    cw = cw_ref[...]
    cb = cb_ref[...]
    for c in range(bm // cm):
        gu = jnp.dot(x_ref[c * cm:(c + 1) * cm, :], wb_ref[...], preferred_element_type=F32)
        g = gu[:, 0:bn]
        u = gu[:, bn:2 * bn]
        lo = _HALO + c * cm
        gs_ref[lo:lo + cm, :] = g
        g1 = gs_ref[lo - 1:lo - 1 + cm, :]
        g2 = gs_ref[lo - 2:lo - 2 + cm, :]
        gate = cw[0:1, :] * g2 + cw[1:2, :] * g1 + cw[2:3, :] * g + cb
        o_ref[c * cm:(c + 1) * cm, :] = ((gate / (1.0 + jnp.exp(-gate))) * u).astype(o_ref.dtype)
    gs_ref[0:_HALO, :] = gs_ref[bm:bm + _HALO, :]


def ffn_in(xn, w_in, layer, conv_w, conv_b, *, bm=1024, bn=V7X_MXU_DIM, cm=512):
    m, k = xn.shape
    d_ff = w_in.shape[2] // 2
    bm = min(bm, m)
    bn = min(bn, d_ff)
    assert m % bm == 0 and d_ff % bn == 0
    nj = d_ff // bn
    cm = min(cm, bm)
    assert bm % cm == 0
    vmem = (2 * bm * k * 2 + 2 * 2 * k * bn * 4 + 2 * k * bn * 2 + 2 * bm * bn * 2
            + (bm + _HALO) * bn * 4 + 6 * bm * bn * 4)
    return pl.pallas_call(
        functools.partial(_ffn_in_body, bm=bm, cm=cm),
        grid=(nj, m // bm),
        in_specs=[pl.BlockSpec((bm, k), lambda j, i: (i, 0)),
                  pl.BlockSpec((None, k, bn), lambda j, i: (layer, 0, j)),
                  pl.BlockSpec((None, k, bn), lambda j, i: (layer, 0, nj + j)),
                  pl.BlockSpec((CONV_W, bn), lambda j, i: (0, j)),
                  pl.BlockSpec((1, bn), lambda j, i: (0, j))],
        out_specs=pl.BlockSpec((bm, bn), lambda j, i: (i, j)),
        out_shape=jax.ShapeDtypeStruct((m, d_ff), BF16),
        scratch_shapes=[pltpu.VMEM((k, 2 * bn), BF16), pltpu.VMEM((bm + _HALO, bn), F32)],
        compiler_params=_params(("arbitrary", "arbitrary"), vmem),
        name="ffn_in",
    )(xn, w_in, w_in, conv_w, conv_b.reshape(1, d_ff))


def ffn_out(act, w_out, layer, *, bm=1024, bn=V7X_MXU_DIM):
    d_ff = act.shape[1]
    if d_ff % (2 * V7X_LANES) != 0:
        return matmul(act, w_out, bm=bm, bn=bn, out_dtype=F32, layer=layer)
    half = d_ff // 2
    part = matmul(act, w_out, bm=bm, bn=bn, out_dtype=F32, layer=layer, k_slice=(0, half))
    return matmul(act, w_out, bm=bm, bn=bn, out_dtype=F32, layer=layer, k_slice=(1, half),
                  add=part)


_SB_HEADS_PER_STEP = 4


def _sb_attn_body(q_ref, k_ref, v_ref, tri_ref, o_ref, acc_ref, c_ref, *, blk, scale):
    qi = pl.program_id(1)
    nh = _SB_HEADS_PER_STEP
    dh = SB_HEAD_DIM
    tri = tri_ref[...]
    acc_ref[...] = jnp.zeros(acc_ref.shape, F32)
    c_ref[...] = jnp.zeros(c_ref.shape, F32)
    key = lax.broadcasted_iota(jnp.int32, (blk, blk), 0)
    qry = lax.broadcasted_iota(jnp.int32, (blk, blk), 1)
    causal = key < qry

    def block(kb, diagonal):
        start = pl.multiple_of(kb * blk, blk)
        heads = range(nh)
        cols = [slice(hd * dh, (hd + 1) * dh) for hd in heads]
        z = [lax.dot_general(k_ref[pl.ds(start, blk), cols[hd]], q_ref[:, cols[hd]], _NT,
                             preferred_element_type=F32) * scale for hd in heads]
        vt = [v_ref[pl.ds(start, blk), cols[hd]].astype(F32).T.astype(BF16) for hd in heads]
        log_beta, log_1mb, hi, lo = [], [], [], []
        for hd in heads:
            sp = jnp.log(1.0 + jnp.exp(-jnp.abs(z[hd])))
            log_beta.append(jnp.minimum(z[hd], 0.0) - sp)
            l1 = jnp.minimum(-z[hd], 0.0) - sp
            if diagonal:
                l1 = jnp.where(causal, l1, 0.0)
            log_1mb.append(l1)
            hi.append(l1.astype(BF16))
            lo.append((l1 - hi[hd].astype(F32)).astype(BF16))
        tail = [jnp.dot(tri, hi[hd], preferred_element_type=F32)
                + jnp.dot(tri, lo[hd], preferred_element_type=F32) for hd in heads]
        c = [c_ref[hd] for hd in heads]
        a = []
        for hd in heads:
            w = jnp.exp(log_beta[hd] + tail[hd] + c[hd])
            if diagonal:
                w = jnp.where(causal, w, 0.0)
            a.append(w.astype(BF16))
        top = None
        for hd in heads:
            acc_ref[hd] += jnp.dot(vt[hd], a[hd], preferred_element_type=F32)
            c_new = c[hd] + tail[hd][0:1, :] + log_1mb[hd][0:1, :]
            c_ref[hd] = c_new
            top = jnp.max(c_new) if top is None else jnp.maximum(top, jnp.max(c_new))
        return top

    top = block(qi, True)
    lax.while_loop(lambda st: (st[0] >= 0) & (st[1] > EXP_ZERO_BELOW),
                   lambda st: (st[0] - 1, block(st[0], False)),
                   (qi - 1, top))
    for hd in range(nh):
        o_ref[:, hd * dh:(hd + 1) * dh] = acc_ref[hd].T.astype(o_ref.dtype)


def sb_attention(qkv, *, blk=256):
    s = qkv.shape[0]
    h, dh = SB_HEADS, SB_HEAD_DIM
    nh = _SB_HEADS_PER_STEP
    blk = min(blk, s)
    groups = h // nh
    tri = (np.arange(blk)[:, None] < np.arange(blk)[None, :]).astype(np.float32)
    vmem = (2 * 2 * s * nh * dh * 2 + 4 * blk * nh * dh * 2 + 2 * blk * blk * 2
            + nh * (dh + 8) * blk * 4 + nh * 12 * blk * blk * 4)
    return pl.pallas_call(
        functools.partial(_sb_attn_body, blk=blk, scale=dh ** -0.5),
        grid=(groups, s // blk),
        in_specs=[pl.BlockSpec((blk, nh * dh), lambda g, i: (i, g)),
                  pl.BlockSpec((s, nh * dh), lambda g, i: (0, groups + g)),
                  pl.BlockSpec((s, nh * dh), lambda g, i: (0, 2 * groups + g)),
                  pl.BlockSpec((blk, blk), lambda g, i: (0, 0))],
        out_specs=pl.BlockSpec((blk, nh * dh), lambda g, i: (i, g)),
        out_shape=jax.ShapeDtypeStruct((s, h * dh), BF16),
        scratch_shapes=[pltpu.VMEM((nh, dh, blk), F32), pltpu.VMEM((nh, 1, blk), F32)],
        compiler_params=_params(("arbitrary", "arbitrary"), vmem),
        name="sb_attention",
    )(qkv, qkv, qkv, jnp.asarray(tri, BF16))


def _mla_prep_body(y_ref, pos_ref, inv_ref, gq_ref, gkv_ref, cq_ref, ckv_ref, kr_ref, cs_ref):
    ql, kvl = MLA_Q_LORA, MLA_KV_LORA
    cq_ref[...] = _rms(y_ref[:, 0:ql], gq_ref[...]).astype(cq_ref.dtype)
    ckv_ref[...] = _rms(y_ref[:, ql:ql + kvl], gkv_ref[...]).astype(ckv_ref.dtype)
    ang = pos_ref[...] * inv_ref[...]
    lane = lax.broadcasted_iota(jnp.int32, ang.shape, 1)
    cs = jnp.where(lane < MLA_ROPE, jnp.cos(ang), jnp.sin(ang))
    cs_ref[...] = cs
    w = y_ref[:, ql + kvl:ql + kvl + V7X_LANES] * cs
    kr_ref[...] = (w + pltpu.roll(w, MLA_ROPE, axis=1)).astype(kr_ref.dtype)


def mla_prep(y, pos_col, g_q, g_kv, *, tm=256):
    m, width = y.shape
    tm = min(tm, m)
    half = MLA_ROPE // 2
    inv = ROPE_THETA ** (-jnp.arange(half, dtype=F32) / half)
    inv4 = jnp.tile(inv, 4).reshape(1, V7X_LANES)
    row = lambda w: pl.BlockSpec((tm, w), lambda i: (i, 0))
    const = lambda w: pl.BlockSpec((1, w), lambda i: (0, 0))
    return pl.pallas_call(
        _mla_prep_body,
        grid=(m // tm,),
        in_specs=[row(width), row(1), const(V7X_LANES), const(MLA_Q_LORA), const(MLA_KV_LORA)],
        out_specs=[row(MLA_Q_LORA), row(MLA_KV_LORA), row(V7X_LANES), row(V7X_LANES)],
        out_shape=[jax.ShapeDtypeStruct((m, MLA_Q_LORA), BF16),
                   jax.ShapeDtypeStruct((m, MLA_KV_LORA), BF16),
                   jax.ShapeDtypeStruct((m, V7X_LANES), BF16),
                   jax.ShapeDtypeStruct((m, V7X_LANES), F32)],
        compiler_params=_params(("arbitrary",), 6 * tm * width * 4),
        name="mla_prep",
    )(y, pos_col, inv4, g_q.reshape(1, -1), g_kv.reshape(1, -1))


_MLA_HEADS_PER_STEP = 2


def _mla_attn_body(q_ref, kn_ref, kr_ref, vt_ref, o_ref, s_ref, m_ref, l_ref, acc_ref, *, blk):
    qi = pl.program_id(1)
    nh = _MLA_HEADS_PER_STEP
    wq = MLA_NOPE + V7X_LANES
    m_ref[...] = jnp.full(m_ref.shape, MASKED, F32)
    l_ref[...] = jnp.zeros(l_ref.shape, F32)
    acc_ref[...] = jnp.zeros(acc_ref.shape, F32)
    key = lax.broadcasted_iota(jnp.int32, (blk, blk), 0)
    qry = lax.broadcasted_iota(jnp.int32, (blk, blk), 1)

    def scores(kb, slot):
        start = pl.multiple_of(kb * blk, blk)
        kr = kr_ref[pl.ds(start, blk), :]
        for hd in range(nh):
            kcat = jnp.concatenate(
                [kn_ref[pl.ds(start, blk), hd * MLA_NOPE:(hd + 1) * MLA_NOPE], kr], axis=1)
            s_ref[slot, hd] = lax.dot_general(kcat, q_ref[:, hd * wq:(hd + 1) * wq], _NT,
                                              preferred_element_type=F32)

    def reduce(kb, slot, diagonal):
        heads = range(nh)
        st = [s_ref[slot, hd] for hd in heads]
        if diagonal:
            st = [jnp.where(key <= qry, t, MASKED) for t in st]
        m_prev = [m_ref[hd] for hd in heads]
        m_new = [jnp.maximum(m_prev[hd], jnp.max(st[hd], axis=0, keepdims=True)) for hd in heads]
        p = [jnp.exp2(st[hd] - m_new[hd]) for hd in heads]
        pv = [jnp.dot(vt_ref[kb, hd * MLA_V:(hd + 1) * MLA_V, :], p[hd].astype(BF16),
                      preferred_element_type=F32) for hd in heads]
        for hd in heads:
            alpha = jnp.exp2(m_prev[hd] - m_new[hd])
            l_ref[hd] = alpha * l_ref[hd] + jnp.sum(p[hd], axis=0, keepdims=True)
            acc_ref[hd] = alpha * acc_ref[hd] + pv[hd]
            m_ref[hd] = m_new[hd]

    def off_diagonal_pair(t, carry):
        scores(2 * t + 1, 1)
        reduce(2 * t, 0, False)
        scores(2 * t + 2, 0)
        reduce(2 * t + 1, 1, False)
        return carry

    scores(0, 0)
    lax.fori_loop(0, qi // 2, off_diagonal_pair, 0)

    @pl.when(qi % 2 == 0)
    def _():
        reduce(qi, 0, True)

    @pl.when(qi % 2 == 1)
    def _():
        scores(qi, 1)
        reduce(qi - 1, 0, False)
        reduce(qi, 1, True)

    for hd in range(nh):
        o_ref[:, hd * MLA_V:(hd + 1) * MLA_V] = (acc_ref[hd] / l_ref[hd]).T.astype(o_ref.dtype)


def mla_attention(q, kn, kr, vt, *, blk):
    s = q.shape[0]
    h = MLA_HEADS
    nh = _MLA_HEADS_PER_STEP
    wq = MLA_NOPE + V7X_LANES
    assert vt.shape == (s // blk, h * MLA_V, blk)
    vmem = (2 * blk * nh * wq * 2 + 2 * s * nh * MLA_NOPE * 2 + 2 * s * V7X_LANES * 2
            + 2 * s * nh * MLA_V * 2 + 2 * blk * nh * MLA_V * 2 + nh * (MLA_V + 16) * blk * 4
            + nh * 8 * blk * blk * 4)
    return pl.pallas_call(
        functools.partial(_mla_attn_body, blk=blk),
        grid=(h // nh, s // blk),
        in_specs=[pl.BlockSpec((blk, nh * wq), lambda hp, i: (i, hp)),
                  pl.BlockSpec((s, nh * MLA_NOPE), lambda hp, i: (0, hp)),
                  pl.BlockSpec((s, V7X_LANES), lambda hp, i: (0, 0)),
                  pl.BlockSpec((s // blk, nh * MLA_V, blk), lambda hp, i: (0, hp, 0))],
        out_specs=pl.BlockSpec((blk, nh * MLA_V), lambda hp, i: (i, hp)),
        out_shape=jax.ShapeDtypeStruct((s, h * MLA_V), BF16),
        scratch_shapes=[pltpu.VMEM((2, nh, blk, blk), F32),
                        pltpu.VMEM((nh, 1, blk), F32), pltpu.VMEM((nh, 1, blk), F32),
                        pltpu.VMEM((nh, MLA_V, blk), F32)],
        compiler_params=_params(("arbitrary", "arbitrary"), vmem),
        name="mla_attention",
    )(q, kn, kr, vt)


def _band_attn_body(*refs, tile, hps, kv_shared, window, scale, has_sink, want_lse):
    refs = list(refs)
    slope_ref = refs.pop(0)
    sink_ref = refs.pop(0) if has_sink else None
    q_ref, k_ref, v_ref, pq_ref, pk_ref, o_ref = refs[:6]
    lse_ref = refs[6] if want_lse else None
    hbase = pl.program_id(1) * hps
    t = pl.program_id(2)
    nb = tile // BLK
    heads = range(hps)
    qi = lax.broadcasted_iota(jnp.int32, (BLK, 2 * BLK), 0)
    kj = lax.broadcasted_iota(jnp.int32, (BLK, 2 * BLK), 1)
    slopes = [slope_ref[hbase + h] for h in heads]
    sinks = [sink_ref[hbase + h] for h in heads] if has_sink else None
    for b in range(nb):
        gb = t * nb + b
        pb = jnp.maximum(gb - 1, 0)
        start = pl.multiple_of(pb * BLK, BLK)
        rows = slice(b * BLK, (b + 1) * BLK)
        pk = jnp.concatenate([pk_ref[pb], pk_ref[pb + 1]], axis=1)
        delta = (gb - pb) * BLK + qi - kj
        valid = (delta >= 0) & (delta <= window)
        dist = jnp.where(valid, pq_ref[rows, :] - pk, -MASKED)
        cols = [slice(h * BLK, (h + 1) * BLK) for h in heads]
        kv_cols = [slice(0, BLK)] * hps if kv_shared else cols
        sc = [lax.dot_general(q_ref[rows, cols[h]], k_ref[pl.ds(start, 2 * BLK), kv_cols[h]], _NT,
                              preferred_element_type=F32) * scale - slopes[h] * dist
              for h in heads]
        m = [jnp.max(sc[h], axis=1, keepdims=True) for h in heads]
        if has_sink:
            m = [jnp.maximum(m[h], sinks[h]) for h in heads]
        p = [jnp.exp(sc[h] - m[h]) for h in heads]
        den = [jnp.sum(p[h], axis=1, keepdims=True) for h in heads]
        if has_sink:
            den = [den[h] + jnp.exp(sinks[h] - m[h]) for h in heads]
        o = [jnp.dot((p[h] * (1.0 / den[h])).astype(BF16),
                     v_ref[pl.ds(start, 2 * BLK), kv_cols[h]], preferred_element_type=F32)
             for h in heads]
        for h in heads:
            o_ref[rows, cols[h]] = o[h].astype(o_ref.dtype)
            if want_lse:
                lse_ref[rows, cols[h]] = jnp.broadcast_to(m[h] + jnp.log(den[h]), (BLK, BLK))


def band_attention(arr, col_q, col_k, col_v, pos_col, pos_row, slopes, sinks, *,
                   n_seq, n_groups, hps, kv_shared, seq_len, window, out_dtype, want_lse):
    dh = BLK
    tile = min(seq_len, 4 * BLK)
    tiles = seq_len // tile
    has_sink = sinks is not None
    kvw = dh if kv_shared else hps * dh
    smem = pl.BlockSpec(memory_space=pltpu.SMEM)
    in_specs = [smem] + ([smem] if has_sink else []) + [
        pl.BlockSpec((tile, hps * dh), lambda r, g, t: (r * tiles + t, col_q(g))),
        pl.BlockSpec((seq_len, kvw), lambda r, g, t: (r, col_k(g))),
        pl.BlockSpec((seq_len, kvw), lambda r, g, t: (r, col_v(g))),
        pl.BlockSpec((None, tile, 1), lambda r, g, t: (r, t, 0)),
        pl.BlockSpec((None, seq_len // BLK, 1, BLK), lambda r, g, t: (r, 0, 0, 0)),
    ]
    args = [slopes] + ([sinks] if has_sink else []) + [arr, arr, arr, pos_col, pos_row]
    out_block = pl.BlockSpec((tile, hps * dh), lambda r, g, t: (t, r * n_groups + g))
    out_cols = n_seq * n_groups * hps * dh
    out_sds = jax.ShapeDtypeStruct((seq_len, out_cols), out_dtype)
    osize = jnp.dtype(out_dtype).itemsize
    vmem = (2 * tile * hps * dh * 2 + 2 * 2 * seq_len * kvw * 2 + 4 * tile * hps * dh * osize
            + 2 * tile * V7X_LANES * 4 + 2 * seq_len * 4 * 8 + hps * 8 * BLK * 2 * BLK * 4)
    return pl.pallas_call(
        functools.partial(_band_attn_body, tile=tile, hps=hps, kv_shared=kv_shared, window=window,
                          scale=dh ** -0.5, has_sink=has_sink, want_lse=want_lse),
        grid=(n_seq, n_groups, seq_len // tile),
        in_specs=in_specs,
        out_specs=[out_block, out_block] if want_lse else out_block,
        out_shape=[out_sds, jax.ShapeDtypeStruct((seq_len, out_cols), F32)] if want_lse else out_sds,
        compiler_params=_params(("arbitrary", "arbitrary", "arbitrary"), vmem),
        name="band_attention",
    )(*args)


def _alibi_slopes(n):
    return jnp.asarray(2.0 ** (-8.0 * np.arange(1, n + 1) / n), dtype=F32)


def _strided_positions(pos_f32, dil):
    s = pos_f32.shape[0]
    ps = pos_f32.reshape(s // dil, dil).T
    return ps.reshape(dil, s // dil, 1), ps.reshape(dil, s // dil // BLK, 1, BLK)


def _dil_merge_body(*refs, dils):
    n = len(dils)
    o_refs, l_refs, out_ref, buf = refs[:n], refs[n:2 * n], refs[2 * n], refs[2 * n + 1]
    tm, w = out_ref.shape

    def token_order(ref, d, c):
        if d == 1:
            return ref[:, c * V7X_LANES:(c + 1) * V7X_LANES]
        for r in range(d):
            lo = r * w + c * V7X_LANES
            buf[pl.ds(r, tm // d, stride=d), :] = ref[:, lo:lo + V7X_LANES]
        return buf[...]

    for c in range(w // V7X_LANES):
        lse = [token_order(l_refs[g], dils[g], c) for g in range(n)]
        m = functools.reduce(jnp.maximum, lse)
        e = [jnp.exp(l - m) for l in lse]
        inv = 1.0 / functools.reduce(lambda a, b: a + b, e)
        acc = None
        for g in range(n):
            term = (e[g] * inv) * token_order(o_refs[g], dils[g], c)
            acc = term if acc is None else acc + term
        out_ref[:, c * V7X_LANES:(c + 1) * V7X_LANES] = acc.astype(out_ref.dtype)


def dil_merge(outs, lses, dils, *, tm=256):
    m = outs[0].shape[0] * dils[0]
    w = outs[0].shape[1] // dils[0]
    tm = min(tm, m)
    specs = [pl.BlockSpec((tm // d, d * w), lambda i: (i, 0)) for d in dils]
    return pl.pallas_call(
        functools.partial(_dil_merge_body, dils=tuple(dils)),
        grid=(m // tm,),
        in_specs=specs + specs,
        out_specs=pl.BlockSpec((tm, w), lambda i: (i, 0)),
        out_shape=jax.ShapeDtypeStruct((m, w), BF16),
        scratch_shapes=[pltpu.VMEM((tm, V7X_LANES), F32)],
        compiler_params=_params(("arbitrary",), (4 * len(dils) + 4) * tm * w * 4),
        name="dil_merge",
    )(*outs, *lses)


def stick_breaking_mixer(hin, w_qkv, w_o, j):
    qkv = matmul(hin, w_qkv, bm=1024, bn=512, out_dtype=BF16, layer=j)
    o = sb_attention(qkv)
    return matmul(o, w_o, bm=1024, bn=512, out_dtype=F32, layer=j)


def mla_mixer(hin, pos_f32, w_dq, g_q, w_uq, w_dkv, g_kv, w_ukv, w_o, j):
    h, nope, rope = MLA_HEADS, MLA_NOPE, MLA_ROPE
    half = rope // 2
    w_dq, w_uq, w_dkv, g_q, g_kv = w_dq[j], w_uq[j], w_dkv[j], g_q[j], g_kv[j]
    kx1 = w_dkv[:, MLA_KV_LORA:MLA_KV_LORA + half]
    kx2 = w_dkv[:, MLA_KV_LORA + half:]
    w_down = jnp.concatenate(
        [w_dq, w_dkv[:, :MLA_KV_LORA], kx1, kx2, -kx2, kx1,
         jnp.zeros((w_dq.shape[0], V7X_LANES), w_dq.dtype)], axis=1)
    y = matmul(hin, w_down[None], bm=1024, bn=V7X_MXU_DIM, out_dtype=F32)
    cq, ckv, kr, cs = mla_prep(y, pos_f32.reshape(-1, 1), g_q, g_kv)
    wq = w_uq.reshape(MLA_Q_LORA, h, nope + rope)
    qx1, qx2 = wq[:, :, nope:nope + half], wq[:, :, nope + half:]
    wq = jnp.concatenate([wq[:, :, :nope], qx1, qx2, -qx2, qx1], axis=2)
    q_scale = (nope + rope) ** -0.5 * float(np.log2(np.e))
    q = matmul(cq, wq.reshape(1, MLA_Q_LORA, h * 2 * V7X_LANES), bm=1024, bn=1024,
               out_dtype=BF16, rope_cs=cs, out_scale=q_scale)
    wkv = w_ukv[j].reshape(MLA_KV_LORA, h, nope + MLA_V)
    w_k = wkv[:, :, :nope].reshape(1, MLA_KV_LORA, h * nope)
    w_vt = wkv[:, :, nope:].reshape(MLA_KV_LORA, h * MLA_V).T
    blk = min(512, hin.shape[0])
    kn = matmul(ckv, w_k, bm=1024, bn=1024, out_dtype=BF16)
    vt = matmul_nt_tiled(w_vt, ckv, bm=blk, bn=1024, out_dtype=BF16)
    o = mla_attention(q, kn, kr, vt, blk=blk)
    return matmul(o, w_o, bm=1024, bn=512, out_dtype=F32, layer=j)


def swa_mixer(hin, pos_f32, w_qkv, sinks, w_o, j):
    s = hin.shape[0]
    rep = SWA_HEADS // SWA_KV_HEADS
    sinks = sinks[j]
    qkv = matmul(hin, w_qkv, bm=1024, bn=512, out_dtype=BF16, layer=j)
    o = band_attention(
        qkv,
        lambda g: g,
        lambda g: SWA_HEADS + g,
        lambda g: SWA_HEADS + SWA_KV_HEADS + g,
        pos_f32.reshape(1, s, 1), pos_f32.reshape(1, s // BLK, 1, BLK),
        _alibi_slopes(SWA_HEADS), sinks,
        n_seq=1, n_groups=SWA_KV_HEADS, hps=rep, kv_shared=True, seq_len=s,
        window=SWA_WINDOW - 1, out_dtype=BF16, want_lse=False)
    return matmul(o, w_o, bm=1024, bn=512, out_dtype=F32, layer=j)


_DIL_VIEWS = tuple(sorted({dil for _, dil in DIL_PATTERNS if dil > 1}))


def dilated_mixer(hin, hin_views, pos_f32, w_qkv, w_o, j):
    s, d_model = hin.shape
    nh = DIL_HEADS
    n_groups = len(DIL_PATTERNS)
    gw = 3 * nh * DIL_HEAD_DIM
    slopes_all = _alibi_slopes(n_groups * nh)
    views = {1: hin, **dict(zip(_DIL_VIEWS, hin_views))}
    outs, lses = [], []
    for gi, (win, dil) in enumerate(DIL_PATTERNS):
        pos_col, pos_row = _strided_positions(pos_f32, dil)
        qkv = matmul(views[dil], w_qkv, bm=1024, bn=512, out_dtype=BF16,
                     layer=j, n_slice=(gi * gw, gw), row_groups=dil)
        hps = 4
        ng = nh // hps
        o, lse = band_attention(
            qkv,
            lambda g: g,
            lambda g, ng=ng: ng + g,
            lambda g, ng=ng: 2 * ng + g,
            pos_col, pos_row, slopes_all[gi * nh:(gi + 1) * nh], None,
            n_seq=dil, n_groups=ng, hps=hps, kv_shared=False, seq_len=s // dil,
            window=win // dil, out_dtype=F32, want_lse=True)
        outs.append(o)
        lses.append(lse)
    o = dil_merge(outs, lses, [dil for _, dil in DIL_PATTERNS])
    return matmul(o, w_o, bm=1024, bn=512, out_dtype=F32, layer=j)


def kernel(x, positions, norm_g, a_w_qkv, a_w_o, b_w_dq, b_g_q, b_w_uq, b_w_dkv, b_g_kv,
           b_w_ukv, b_w_o, c_w_qkv, c_sinks, c_w_o, d_w_qkv, d_w_o, ffn_w_in, ffn_conv_w,
           ffn_conv_b, ffn_w_out):
    batch, s, d = x.shape
    depth = norm_g.shape[0]
    n_mixers = 4
    outs = []
    for bi in range(batch):
        xb = x[bi]
        pos_f32 = positions[bi].astype(F32)
        hin = norm_rows(xb, norm_g[0, 0])
        hin_views = ()
        for i in range(depth):
            mixer, j = i % n_mixers, i // n_mixers
            if mixer == 0:
                h = stick_breaking_mixer(hin, a_w_qkv, a_w_o, j)
            elif mixer == 1:
                h = mla_mixer(hin, pos_f32, b_w_dq, b_g_q, b_w_uq, b_w_dkv, b_g_kv, b_w_ukv,
                              b_w_o, j)
            elif mixer == 2:
                h = swa_mixer(hin, pos_f32, c_w_qkv, c_sinks, c_w_o, j)
            else:
                h = dilated_mixer(hin, hin_views, pos_f32, d_w_qkv, d_w_o, j)
            xb, xn = resid_norm(h, xb, norm_g[i, 1], norm_g[i, 2])
            act = ffn_in(xn, ffn_w_in, i, ffn_conv_w[i], ffn_conv_b[i])
            h = ffn_out(act, ffn_w_out, i)
            if i + 1 == depth:
                xb = resid_norm(h, xb, norm_g[i, 3])
            elif (i + 1) % n_mixers == 3:
                xb, hin, hin_views = resid_norm(h, xb, norm_g[i, 3], norm_g[i + 1, 0],
                                                dils=_DIL_VIEWS)
            else:
                xb, hin = resid_norm(h, xb, norm_g[i, 3], norm_g[i + 1, 0])
        outs.append(xb)
    return jnp.stack(outs, axis=0)
```

```python
import functools

import numpy as np
import jax
import jax.numpy as jnp
from jax import lax
from jax.experimental import pallas as pl
from jax.experimental.pallas import tpu as pltpu

F32 = jnp.float32
BF16 = jnp.bfloat16

V7X_VMEM_BYTES = 64 * 2**20
V7X_LANES = 128
V7X_MXU_DIM = 256

EPS = 1e-6
BLK = 128
SB_HEADS, SB_HEAD_DIM = 32, 128
MLA_HEADS, MLA_NOPE, MLA_ROPE, MLA_V = 32, 128, 64, 128
MLA_Q_LORA, MLA_KV_LORA = 1024, 512
ROPE_THETA = 10000.0
SWA_HEADS, SWA_KV_HEADS, SWA_HEAD_DIM, SWA_WINDOW = 32, 8, 128, 128
DIL_PATTERNS = ((128, 1), (512, 4), (2048, 16))
DIL_HEADS, DIL_HEAD_DIM = 16, 128
CONV_W = 3

MASKED = -1e30
EXP_ZERO_BELOW = -105.0

_NT = (((1,), (1,)), ((), ()))


def _params(semantics, vmem_bytes):
    limit = min(int(vmem_bytes) + (8 << 20), V7X_VMEM_BYTES - (6 << 20))
    return pltpu.CompilerParams(dimension_semantics=semantics, vmem_limit_bytes=limit)


def _rms(x, g):
    return (x * lax.rsqrt(jnp.mean(x * x, axis=-1, keepdims=True) + EPS)) * g


def _mm_body(*refs, mode, out_scale, sub):
    x_ref, w_ref, o_ref = refs[0], refs[1], refs[-1]
    if sub > 1:
        wb = w_ref[...].astype(BF16)
        rows, k = x_ref.shape[0], x_ref.shape[1] // sub
        for r in range(sub):
            o_ref[r * rows:(r + 1) * rows, :] = jnp.dot(
                x_ref[:, r * k:(r + 1) * k], wb, preferred_element_type=F32).astype(o_ref.dtype)
        return
    acc = jnp.dot(x_ref[...], w_ref[...].astype(BF16), preferred_element_type=F32)
    if mode == "plain":
        o_ref[...] = acc.astype(o_ref.dtype)
    elif mode == "add":
        o_ref[...] = (refs[2][...] + acc).astype(o_ref.dtype)
    else:
        cs = refs[2][...] * out_scale
        for c in range(acc.shape[1] // (2 * V7X_LANES)):
            lo = 2 * c * V7X_LANES
            o_ref[:, lo:lo + V7X_LANES] = (acc[:, lo:lo + V7X_LANES] * out_scale
                                           ).astype(o_ref.dtype)
            o_ref[:, lo + V7X_LANES:lo + 2 * V7X_LANES] = (
                acc[:, lo + V7X_LANES:lo + 2 * V7X_LANES] * cs).astype(o_ref.dtype)


def matmul(x, w, *, bm, bn, out_dtype, layer=0, k_slice=None, n_slice=None, row_groups=1,
           add=None, rope_cs=None, out_scale=1.0):
    n0, n = (0, w.shape[2]) if n_slice is None else n_slice
    if row_groups == 1:
        m, kx = x.shape
    else:
        assert k_slice is None
        m, kx = x.shape[0] * row_groups, x.shape[1] // row_groups
    kb, tk = (0, kx) if k_slice is None else k_slice
    group_rows = m // row_groups
    bm, bn = min(bm, m), min(bn, n)
    sub = max(bm // group_rows, 1)
    assert n % bn == 0 and n0 % bn == 0 and m % bm == 0
    assert (group_rows % bm == 0) if sub == 1 else (bm == sub * group_rows and add is None
                                                    and rope_cs is None)
    per_group = max(group_rows // bm, 1)
    in_specs = [pl.BlockSpec((bm // sub, sub * tk),
                             lambda i, j: (i % per_group, i // per_group + kb)),
                pl.BlockSpec((None, tk, bn), lambda i, j: (layer, kb, n0 // bn + j))]
    args = [x, w]
    mode = "plain"
    aliases = {}
    extra = 0
    if add is not None:
        in_specs.append(pl.BlockSpec((bm, bn), lambda i, j: (i, j)))
        args.append(add)
        mode = "add"
        aliases = {2: 0}
        extra = 2 * bm * bn * add.dtype.itemsize
    if rope_cs is not None:
        in_specs.append(pl.BlockSpec((bm, V7X_LANES), lambda i, j: (i, 0)))
        args.append(rope_cs)
        mode = "rope"
    osize = jnp.dtype(out_dtype).itemsize
    vmem = (2 * bm * tk * 2 + 2 * tk * bn * w.dtype.itemsize + tk * bn * 2
            + 2 * bm * bn * osize + bm * bn * 4 + extra)
    return pl.pallas_call(
        functools.partial(_mm_body, mode=mode, out_scale=out_scale, sub=sub),
        grid=(m // bm, n // bn),
        in_specs=in_specs,
        out_specs=pl.BlockSpec((bm, bn), lambda i, j: (i, j)),
        out_shape=jax.ShapeDtypeStruct((m, n), out_dtype),
        input_output_aliases=aliases,
        compiler_params=_params(("arbitrary", "arbitrary"), vmem),
        name="matmul",
    )(*args)


def _mm_nt_body(w_ref, x_ref, o_ref):
    o_ref[...] = lax.dot_general(w_ref[...].astype(BF16), x_ref[...], _NT,
                                 preferred_element_type=F32).astype(o_ref.dtype)


def matmul_nt_tiled(w_t, x, *, bm, bn, out_dtype):
    n, k = w_t.shape
    m = x.shape[0]
    bm, bn = min(bm, m), min(bn, n)
    assert m % bm == 0 and n % bn == 0
    vmem = 3 * bn * k * 4 + 2 * bm * k * 2 + 2 * bn * bm * jnp.dtype(out_dtype).itemsize + bn * bm * 4
    return pl.pallas_call(
        _mm_nt_body,
        grid=(n // bn, m // bm),
        in_specs=[pl.BlockSpec((bn, k), lambda j, i: (j, 0)),
                  pl.BlockSpec((bm, k), lambda j, i: (i, 0))],
        out_specs=pl.BlockSpec((None, bn, bm), lambda j, i: (i, j, 0)),
        out_shape=jax.ShapeDtypeStruct((m // bm, n, bm), out_dtype),
        compiler_params=_params(("arbitrary", "arbitrary"), vmem),
        name="matmul_nt",
    )(w_t, x)


def _norm_body(x_ref, g_ref, o_ref):
    o_ref[...] = _rms(x_ref[...], g_ref[...]).astype(o_ref.dtype)


def norm_rows(x, g, *, tm=256):
    m, d = x.shape
    tm = min(tm, m)
    return pl.pallas_call(
        _norm_body,
        grid=(m // tm,),
        in_specs=[pl.BlockSpec((tm, d), lambda i: (i, 0)),
                  pl.BlockSpec((1, d), lambda i: (0, 0))],
        out_specs=pl.BlockSpec((tm, d), lambda i: (i, 0)),
        out_shape=jax.ShapeDtypeStruct((m, d), BF16),
        compiler_params=_params(("arbitrary",), 2 * tm * d * 6 + 3 * tm * d * 4),
        name="norm_rows",
    )(x, g.reshape(1, d))


def _resid_norm_body(h_ref, x_ref, ga_ref, gb_ref, xo_ref, no_ref, *rest, dils):
    xn = x_ref[...] + _rms(h_ref[...], ga_ref[...])
    xo_ref[...] = xn
    n = _rms(xn, gb_ref[...])
    no_ref[...] = n.astype(no_ref.dtype)
    if dils:
        n_scr = rest[-1]
        tm, dm = n.shape
        chunks = dm // V7X_LANES
        for c in range(chunks):
            n_scr[c] = n[:, c * V7X_LANES:(c + 1) * V7X_LANES]
        for d, ref in zip(dils, rest[:-1]):
            for r in range(d):
                for c in range(chunks):
                    lo = r * dm + c * V7X_LANES
                    ref[:, lo:lo + V7X_LANES] = n_scr[c, pl.ds(r, tm // d, stride=d), :].astype(
                        ref.dtype)


def _resid_body(h_ref, x_ref, ga_ref, xo_ref):
    xo_ref[...] = x_ref[...] + _rms(h_ref[...], ga_ref[...])


def resid_norm(h, x, g_a, g_b=None, *, tm=128, dils=()):
    m, d = x.shape
    if dils:
        tm = 16 * max(dils)
    tm = min(tm, m)
    row = pl.BlockSpec((tm, d), lambda i: (i, 0))
    gain = pl.BlockSpec((1, d), lambda i: (0, 0))
    if g_b is None:
        return pl.pallas_call(
            _resid_body,
            grid=(m // tm,),
            in_specs=[row, row, gain],
            out_specs=row,
            out_shape=jax.ShapeDtypeStruct((m, d), F32),
            compiler_params=_params(("arbitrary",), 2 * tm * d * 12 + 3 * tm * d * 4),
            name="resid",
        )(h, x, g_a.reshape(1, d))
    outs = pl.pallas_call(
        functools.partial(_resid_norm_body, dils=tuple(dils)),
        grid=(m // tm,),
        in_specs=[row, row, gain, gain],
        out_specs=[row, row] + [pl.BlockSpec((tm // dl, dl * d), lambda i: (i, 0)) for dl in dils],
        out_shape=[jax.ShapeDtypeStruct((m, d), F32), jax.ShapeDtypeStruct((m, d), BF16)]
        + [jax.ShapeDtypeStruct((m // dl, dl * d), BF16) for dl in dils],
        scratch_shapes=[pltpu.VMEM((d // V7X_LANES, tm, V7X_LANES), F32)] if dils else [],
        compiler_params=_params(("arbitrary",),
                                2 * tm * d * (14 + 2 * len(dils)) + 5 * tm * d * 4),
        name="resid_norm",
    )(h, x, g_a.reshape(1, d), g_b.reshape(1, d))
    return (outs[0], outs[1]) + ((tuple(outs[2:]),) if dils else ())


_HALO = 8


def _order_after(x, v):
    zero = (lax.bitcast_convert_type(v, jnp.uint32) >> 16) >> 16
    return pltpu.bitcast(pltpu.bitcast(x, jnp.uint32) | zero, BF16)


def _ffn_in_body(x_ref, wg_ref, wu_ref, cw_ref, cb_ref, o_ref, wb_ref, g_ref, u_ref, t_ref, *,
                 bm, cm, row_tiles):
    s = pl.program_id(0)
    i = s % row_tiles
    bn = o_ref.shape[1]
    nc = bm // cm

    @pl.when(s == 0)
    def _():
        g_ref[...] = jnp.zeros(g_ref.shape, F32)
        u_ref[...] = jnp.zeros(u_ref.shape, F32)

    @pl.when(i == 0)
    def _():
        wb_ref[:, 0:bn] = wg_ref[...].astype(BF16)
        wb_ref[:, bn:2 * bn] = wu_ref[...].astype(BF16)

    cw = cw_ref[...]
    cb = cb_ref[...]
    for c in range(nc):
        rows = slice(c * cm, (c + 1) * cm)
        lo = _HALO + c * cm
        g = g_ref[lo:lo + cm, :]
        g1 = g_ref[lo - 1:lo - 1 + cm, :]
        g2 = g_ref[lo - 2:lo - 2 + cm, :]
        gate = cw[0:1, :] * g2 + cw[1:2, :] * g1 + cw[2:3, :] * g + cb
        act = (gate / (1.0 + jnp.exp(-gate))) * u_ref[rows, :]
        o_ref[rows, :] = act.astype(o_ref.dtype)
        if c == 0:
            g_ref[0:_HALO, :] = jnp.where(i == 0, 0.0, g_ref[bm:bm + _HALO, :])
        else:
            g_ref[lo - _HALO:lo, :] = t_ref[c - 1]
        x0 = _order_after(x_ref[rows, 0:V7X_LANES], act[0:cm // 2, 0:V7X_LANES])
        x = jnp.concatenate([x0, x_ref[rows, V7X_LANES:]], axis=1)
        gu = jnp.dot(x, wb_ref[...], preferred_element_type=F32)
        u_ref[rows, :] = gu[:, bn:2 * bn]
        if c == nc - 1:
            g_ref[lo:lo + cm, :] = gu[:, 0:bn]
        else:
            g_ref[lo:lo + cm - _HALO, :] = gu[0:cm - _HALO, 0:bn]
            t_ref[c] = gu[cm - _HALO:cm, 0:bn]


def ffn_in(xn, w_in, layer, conv_w, conv_b, *, bm=1024, bn=V7X_MXU_DIM, cm=128):
    m, k = xn.shape
    d_ff = w_in.shape[2] // 2
    bm = min(bm, m)
    bn = min(bn, d_ff)
    assert m % bm == 0 and d_ff % bn == 0
    nj, ni = d_ff // bn, m // bm
    tiles = nj * ni
    cm = min(cm, bm)
    assert bm % cm == 0 and cm % (2 * _HALO) == 0

    def mm(f):
        return lambda s: f(jnp.minimum(s, tiles - 1) // ni, jnp.minimum(s, tiles - 1) % ni)

    def ep(f):
        return lambda s: f(jnp.maximum(s - 1, 0) // ni, jnp.maximum(s - 1, 0) % ni)

    vmem = (2 * bm * k * 2 + 2 * 2 * k * bn * 4 + 2 * k * bn * 2 + 2 * bm * bn * 2
            + (2 * bm + _HALO) * bn * 4 + 6 * bm * bn * 4)
    return pl.pallas_call(
        functools.partial(_ffn_in_body, bm=bm, cm=cm, row_tiles=ni),
        grid=(tiles + 1,),
        in_specs=[pl.BlockSpec((bm, k), mm(lambda j, i: (i, 0))),
                  pl.BlockSpec((None, k, bn), mm(lambda j, i: (layer, 0, j))),
                  pl.BlockSpec((None, k, bn), mm(lambda j, i: (layer, 0, nj + j))),
                  pl.BlockSpec((CONV_W, bn), ep(lambda j, i: (0, j))),
                  pl.BlockSpec((1, bn), ep(lambda j, i: (0, j)))],
        out_specs=pl.BlockSpec((bm, bn), ep(lambda j, i: (i, j))),
        out_shape=jax.ShapeDtypeStruct((m, d_ff), BF16),
        scratch_shapes=[pltpu.VMEM((k, 2 * bn), BF16), pltpu.VMEM((bm + _HALO, bn), F32),
                        pltpu.VMEM((bm, bn), F32),
                        pltpu.VMEM((max(bm // cm - 1, 1), _HALO, bn), F32)],
        compiler_params=_params(("arbitrary",), vmem),
        name="ffn_in",
    )(xn, w_in, w_in, conv_w, conv_b.reshape(1, d_ff))


def ffn_out(act, w_out, layer, *, bm=1024, bn=V7X_MXU_DIM):
    d_ff = act.shape[1]
    if d_ff % (2 * V7X_LANES) != 0:
        return matmul(act, w_out, bm=bm, bn=bn, out_dtype=F32, layer=layer)
    half = d_ff // 2
    part = matmul(act, w_out, bm=bm, bn=bn, out_dtype=F32, layer=layer, k_slice=(0, half))
    return matmul(act, w_out, bm=bm, bn=bn, out_dtype=F32, layer=layer, k_slice=(1, half),
                  add=part)


_SB_HEADS_PER_STEP = 4


def _sb_attn_body(q_ref, k_ref, v_ref, tri_ref, o_ref, acc_ref, c_ref, *, blk, scale):
    qi = pl.program_id(1)
    nh = _SB_HEADS_PER_STEP
    dh = SB_HEAD_DIM
    tri = tri_ref[...]
    acc_ref[...] = jnp.zeros(acc_ref.shape, F32)
    c_ref[...] = jnp.zeros(c_ref.shape, F32)
    key = lax.broadcasted_iota(jnp.int32, (blk, blk), 0)
    qry = lax.broadcasted_iota(jnp.int32, (blk, blk), 1)
    causal = key < qry

    def block(kb, diagonal):
        start = pl.multiple_of(kb * blk, blk)
        heads = range(nh)
        cols = [slice(hd * dh, (hd + 1) * dh) for hd in heads]
        z = [lax.dot_general(k_ref[pl.ds(start, blk), cols[hd]], q_ref[:, cols[hd]], _NT,
                             preferred_element_type=F32) * scale for hd in heads]
        vt = [v_ref[pl.ds(start, blk), cols[hd]].astype(F32).T.astype(BF16) for hd in heads]
        log_beta, log_1mb, hi, lo = [], [], [], []
        for hd in heads:
            sp = jnp.log(1.0 + jnp.exp(-jnp.abs(z[hd])))
            log_beta.append(jnp.minimum(z[hd], 0.0) - sp)
            l1 = jnp.minimum(-z[hd], 0.0) - sp
            if diagonal:
                l1 = jnp.where(causal, l1, 0.0)
            log_1mb.append(l1)
            hi.append(l1.astype(BF16))
            lo.append((l1 - hi[hd].astype(F32)).astype(BF16))
        tail = [jnp.dot(tri, hi[hd], preferred_element_type=F32)
                + jnp.dot(tri, lo[hd], preferred_element_type=F32) for hd in heads]
        c = [c_ref[hd] for hd in heads]
        a = []
        for hd in heads:
            w = jnp.exp(log_beta[hd] + tail[hd] + c[hd])
            if diagonal:
                w = jnp.where(causal, w, 0.0)
            a.append(w.astype(BF16))
        top = None
        for hd in heads:
            acc_ref[hd] += jnp.dot(vt[hd], a[hd], preferred_element_type=F32)
            c_new = c[hd] + tail[hd][0:1, :] + log_1mb[hd][0:1, :]
            c_ref[hd] = c_new
            top = jnp.max(c_new) if top is None else jnp.maximum(top, jnp.max(c_new))
        return top

    top = block(qi, True)
    lax.while_loop(lambda st: (st[0] >= 0) & (st[1] > EXP_ZERO_BELOW),
                   lambda st: (st[0] - 1, block(st[0], False)),
                   (qi - 1, top))
    for hd in range(nh):
        o_ref[:, hd * dh:(hd + 1) * dh] = acc_ref[hd].T.astype(o_ref.dtype)


def sb_attention(qkv, *, blk=256):
    s = qkv.shape[0]
    h, dh = SB_HEADS, SB_HEAD_DIM
    nh = _SB_HEADS_PER_STEP
    blk = min(blk, s)
    groups = h // nh
    tri = (np.arange(blk)[:, None] < np.arange(blk)[None, :]).astype(np.float32)
    vmem = (2 * 2 * s * nh * dh * 2 + 4 * blk * nh * dh * 2 + 2 * blk * blk * 2
            + nh * (dh + 8) * blk * 4 + nh * 12 * blk * blk * 4)
    return pl.pallas_call(
        functools.partial(_sb_attn_body, blk=blk, scale=dh ** -0.5),
        grid=(groups, s // blk),
        in_specs=[pl.BlockSpec((blk, nh * dh), lambda g, i: (i, g)),
                  pl.BlockSpec((s, nh * dh), lambda g, i: (0, groups + g)),
                  pl.BlockSpec((s, nh * dh), lambda g, i: (0, 2 * groups + g)),
                  pl.BlockSpec((blk, blk), lambda g, i: (0, 0))],
        out_specs=pl.BlockSpec((blk, nh * dh), lambda g, i: (i, g)),
        out_shape=jax.ShapeDtypeStruct((s, h * dh), BF16),
        scratch_shapes=[pltpu.VMEM((nh, dh, blk), F32), pltpu.VMEM((nh, 1, blk), F32)],
        compiler_params=_params(("arbitrary", "arbitrary"), vmem),
        name="sb_attention",
    )(qkv, qkv, qkv, jnp.asarray(tri, BF16))


def _mla_prep_body(y_ref, pos_ref, inv_ref, gq_ref, gkv_ref, cq_ref, ckv_ref, kr_ref, cs_ref):
    ql, kvl = MLA_Q_LORA, MLA_KV_LORA
    cq_ref[...] = _rms(y_ref[:, 0:ql], gq_ref[...]).astype(cq_ref.dtype)
    ckv_ref[...] = _rms(y_ref[:, ql:ql + kvl], gkv_ref[...]).astype(ckv_ref.dtype)
    ang = pos_ref[...] * inv_ref[...]
    lane = lax.broadcasted_iota(jnp.int32, ang.shape, 1)
    cs = jnp.where(lane < MLA_ROPE, jnp.cos(ang), jnp.sin(ang))
    cs_ref[...] = cs
    w = y_ref[:, ql + kvl:ql + kvl + V7X_LANES] * cs
    kr_ref[...] = (w + pltpu.roll(w, MLA_ROPE, axis=1)).astype(kr_ref.dtype)


def mla_prep(y, pos_col, g_q, g_kv, *, tm=256):
    m, width = y.shape
    tm = min(tm, m)
    half = MLA_ROPE // 2
    inv = ROPE_THETA ** (-jnp.arange(half, dtype=F32) / half)
    inv4 = jnp.tile(inv, 4).reshape(1, V7X_LANES)
    row = lambda w: pl.BlockSpec((tm, w), lambda i: (i, 0))
    const = lambda w: pl.BlockSpec((1, w), lambda i: (0, 0))
    return pl.pallas_call(
        _mla_prep_body,
        grid=(m // tm,),
        in_specs=[row(width), row(1), const(V7X_LANES), const(MLA_Q_LORA), const(MLA_KV_LORA)],
        out_specs=[row(MLA_Q_LORA), row(MLA_KV_LORA), row(V7X_LANES), row(V7X_LANES)],
        out_shape=[jax.ShapeDtypeStruct((m, MLA_Q_LORA), BF16),
                   jax.ShapeDtypeStruct((m, MLA_KV_LORA), BF16),
                   jax.ShapeDtypeStruct((m, V7X_LANES), BF16),
                   jax.ShapeDtypeStruct((m, V7X_LANES), F32)],
        compiler_params=_params(("arbitrary",), 6 * tm * width * 4),
        name="mla_prep",
    )(y, pos_col, inv4, g_q.reshape(1, -1), g_kv.reshape(1, -1))


_MLA_HEADS_PER_STEP = 2


def _mla_attn_body(q_ref, kn_ref, kr_ref, vt_ref, o_ref, s_ref, m_ref, l_ref, acc_ref, *, blk):
    qi = pl.program_id(1)
    nh = _MLA_HEADS_PER_STEP
    wq = MLA_NOPE + V7X_LANES
    m_ref[...] = jnp.full(m_ref.shape, MASKED, F32)
    l_ref[...] = jnp.zeros(l_ref.shape, F32)
    acc_ref[...] = jnp.zeros(acc_ref.shape, F32)
    key = lax.broadcasted_iota(jnp.int32, (blk, blk), 0)
    qry = lax.broadcasted_iota(jnp.int32, (blk, blk), 1)

    def scores(kb, slot):
        start = pl.multiple_of(kb * blk, blk)
        kr = kr_ref[pl.ds(start, blk), :]
        for hd in range(nh):
            kcat = jnp.concatenate(
                [kn_ref[pl.ds(start, blk), hd * MLA_NOPE:(hd + 1) * MLA_NOPE], kr], axis=1)
            s_ref[slot, hd] = lax.dot_general(kcat, q_ref[:, hd * wq:(hd + 1) * wq], _NT,
                                              preferred_element_type=F32)

    def reduce(kb, slot, diagonal):
        heads = range(nh)
        st = [s_ref[slot, hd] for hd in heads]
        if diagonal:
            st = [jnp.where(key <= qry, t, MASKED) for t in st]
        m_prev = [m_ref[hd] for hd in heads]
        m_new = [jnp.maximum(m_prev[hd], jnp.max(st[hd], axis=0, keepdims=True)) for hd in heads]
        p = [jnp.exp2(st[hd] - m_new[hd]) for hd in heads]
        pv = [jnp.dot(vt_ref[kb, hd * MLA_V:(hd + 1) * MLA_V, :], p[hd].astype(BF16),
                      preferred_element_type=F32) for hd in heads]
        for hd in heads:
            alpha = jnp.exp2(m_prev[hd] - m_new[hd])
            l_ref[hd] = alpha * l_ref[hd] + jnp.sum(p[hd], axis=0, keepdims=True)
            acc_ref[hd] = alpha * acc_ref[hd] + pv[hd]
            m_ref[hd] = m_new[hd]

    def off_diagonal_pair(t, carry):
        scores(2 * t + 1, 1)
        reduce(2 * t, 0, False)
        scores(2 * t + 2, 0)
        reduce(2 * t + 1, 1, False)
        return carry

    scores(0, 0)
    lax.fori_loop(0, qi // 2, off_diagonal_pair, 0)

    @pl.when(qi % 2 == 0)
    def _():
        reduce(qi, 0, True)

    @pl.when(qi % 2 == 1)
    def _():
        scores(qi, 1)
        reduce(qi - 1, 0, False)
        reduce(qi, 1, True)

    for hd in range(nh):
        o_ref[:, hd * MLA_V:(hd + 1) * MLA_V] = (acc_ref[hd] / l_ref[hd]).T.astype(o_ref.dtype)


def mla_attention(q, kn, kr, vt, *, blk):
    s = q.shape[0]
    h = MLA_HEADS
    nh = _MLA_HEADS_PER_STEP
    wq = MLA_NOPE + V7X_LANES
    assert vt.shape == (s // blk, h * MLA_V, blk)
    vmem = (2 * blk * nh * wq * 2 + 2 * s * nh * MLA_NOPE * 2 + 2 * s * V7X_LANES * 2
            + 2 * s * nh * MLA_V * 2 + 2 * blk * nh * MLA_V * 2 + nh * (MLA_V + 16) * blk * 4
            + nh * 8 * blk * blk * 4)
    return pl.pallas_call(
        functools.partial(_mla_attn_body, blk=blk),
        grid=(h // nh, s // blk),
        in_specs=[pl.BlockSpec((blk, nh * wq), lambda hp, i: (i, hp)),
                  pl.BlockSpec((s, nh * MLA_NOPE), lambda hp, i: (0, hp)),
                  pl.BlockSpec((s, V7X_LANES), lambda hp, i: (0, 0)),
                  pl.BlockSpec((s // blk, nh * MLA_V, blk), lambda hp, i: (0, hp, 0))],
        out_specs=pl.BlockSpec((blk, nh * MLA_V), lambda hp, i: (i, hp)),
        out_shape=jax.ShapeDtypeStruct((s, h * MLA_V), BF16),
        scratch_shapes=[pltpu.VMEM((2, nh, blk, blk), F32),
                        pltpu.VMEM((nh, 1, blk), F32), pltpu.VMEM((nh, 1, blk), F32),
                        pltpu.VMEM((nh, MLA_V, blk), F32)],
        compiler_params=_params(("arbitrary", "arbitrary"), vmem),
        name="mla_attention",
    )(q, kn, kr, vt)


def _band_attn_body(*refs, tile, hps, kv_shared, window, scale, has_sink, want_lse):
    refs = list(refs)
    slope_ref = refs.pop(0)
    sink_ref = refs.pop(0) if has_sink else None
    q_ref, k_ref, v_ref, pq_ref, pk_ref, o_ref = refs[:6]
    lse_ref = refs[6] if want_lse else None
    hbase = pl.program_id(1) * hps
    t = pl.program_id(2)
    nb = tile // BLK
    heads = range(hps)
    qi = lax.broadcasted_iota(jnp.int32, (BLK, 2 * BLK), 0)
    kj = lax.broadcasted_iota(jnp.int32, (BLK, 2 * BLK), 1)
    slopes = [slope_ref[hbase + h] for h in heads]
    sinks = [sink_ref[hbase + h] for h in heads] if has_sink else None
    for b in range(nb):
        gb = t * nb + b
        pb = jnp.maximum(gb - 1, 0)
        start = pl.multiple_of(pb * BLK, BLK)
        rows = slice(b * BLK, (b + 1) * BLK)
        pk = jnp.concatenate([pk_ref[pb], pk_ref[pb + 1]], axis=1)
        delta = (gb - pb) * BLK + qi - kj
        valid = (delta >= 0) & (delta <= window)
        dist = jnp.where(valid, pq_ref[rows, :] - pk, -MASKED)
        cols = [slice(h * BLK, (h + 1) * BLK) for h in heads]
        kv_cols = [slice(0, BLK)] * hps if kv_shared else cols
        sc = [lax.dot_general(q_ref[rows, cols[h]], k_ref[pl.ds(start, 2 * BLK), kv_cols[h]], _NT,
                              preferred_element_type=F32) * scale - slopes[h] * dist
              for h in heads]
        m = [jnp.max(sc[h], axis=1, keepdims=True) for h in heads]
        if has_sink:
            m = [jnp.maximum(m[h], sinks[h]) for h in heads]
        p = [jnp.exp(sc[h] - m[h]) for h in heads]
        den = [jnp.sum(p[h], axis=1, keepdims=True) for h in heads]
        if has_sink:
            den = [den[h] + jnp.exp(sinks[h] - m[h]) for h in heads]
        o = [jnp.dot((p[h] * (1.0 / den[h])).astype(BF16),
                     v_ref[pl.ds(start, 2 * BLK), kv_cols[h]], preferred_element_type=F32)
             for h in heads]
        for h in heads:
            o_ref[rows, cols[h]] = o[h].astype(o_ref.dtype)
            if want_lse:
                lse_ref[rows, cols[h]] = jnp.broadcast_to(m[h] + jnp.log(den[h]), (BLK, BLK))


def band_attention(arr, col_q, col_k, col_v, pos_col, pos_row, slopes, sinks, *,
                   n_seq, n_groups, hps, kv_shared, seq_len, window, out_dtype, want_lse):
    dh = BLK
    tile = min(seq_len, 4 * BLK)
    tiles = seq_len // tile
    has_sink = sinks is not None
    kvw = dh if kv_shared else hps * dh
    smem = pl.BlockSpec(memory_space=pltpu.SMEM)
    in_specs = [smem] + ([smem] if has_sink else []) + [
        pl.BlockSpec((tile, hps * dh), lambda r, g, t: (r * tiles + t, col_q(g))),
        pl.BlockSpec((seq_len, kvw), lambda r, g, t: (r, col_k(g))),
        pl.BlockSpec((seq_len, kvw), lambda r, g, t: (r, col_v(g))),
        pl.BlockSpec((None, tile, 1), lambda r, g, t: (r, t, 0)),
        pl.BlockSpec((None, seq_len // BLK, 1, BLK), lambda r, g, t: (r, 0, 0, 0)),
    ]
    args = [slopes] + ([sinks] if has_sink else []) + [arr, arr, arr, pos_col, pos_row]
    out_block = pl.BlockSpec((tile, hps * dh), lambda r, g, t: (t, r * n_groups + g))
    out_cols = n_seq * n_groups * hps * dh
    out_sds = jax.ShapeDtypeStruct((seq_len, out_cols), out_dtype)
    osize = jnp.dtype(out_dtype).itemsize
    vmem = (2 * tile * hps * dh * 2 + 2 * 2 * seq_len * kvw * 2 + 4 * tile * hps * dh * osize
            + 2 * tile * V7X_LANES * 4 + 2 * seq_len * 4 * 8 + hps * 8 * BLK * 2 * BLK * 4)
    return pl.pallas_call(
        functools.partial(_band_attn_body, tile=tile, hps=hps, kv_shared=kv_shared, window=window,
                          scale=dh ** -0.5, has_sink=has_sink, want_lse=want_lse),
        grid=(n_seq, n_groups, seq_len // tile),
        in_specs=in_specs,
        out_specs=[out_block, out_block] if want_lse else out_block,
        out_shape=[out_sds, jax.ShapeDtypeStruct((seq_len, out_cols), F32)] if want_lse else out_sds,
        compiler_params=_params(("arbitrary", "arbitrary", "arbitrary"), vmem),
        name="band_attention",
    )(*args)


def _alibi_slopes(n):
    return jnp.asarray(2.0 ** (-8.0 * np.arange(1, n + 1) / n), dtype=F32)


def _strided_positions(pos_f32, dil):
    s = pos_f32.shape[0]
    ps = pos_f32.reshape(s // dil, dil).T
    return ps.reshape(dil, s // dil, 1), ps.reshape(dil, s // dil // BLK, 1, BLK)


def _dil_merge_body(*refs, dils):
    n = len(dils)
    o_refs, l_refs, out_ref, buf = refs[:n], refs[n:2 * n], refs[2 * n], refs[2 * n + 1]
    tm, w = out_ref.shape

    def token_order(ref, d, c):
        if d == 1:
            return ref[:, c * V7X_LANES:(c + 1) * V7X_LANES]
        for r in range(d):
            lo = r * w + c * V7X_LANES
            buf[pl.ds(r, tm // d, stride=d), :] = ref[:, lo:lo + V7X_LANES]
        return buf[...]

    for c in range(w // V7X_LANES):
        lse = [token_order(l_refs[g], dils[g], c) for g in range(n)]
        m = functools.reduce(jnp.maximum, lse)
        e = [jnp.exp(l - m) for l in lse]
        inv = 1.0 / functools.reduce(lambda a, b: a + b, e)
        acc = None
        for g in range(n):
            term = (e[g] * inv) * token_order(o_refs[g], dils[g], c)
            acc = term if acc is None else acc + term
        out_ref[:, c * V7X_LANES:(c + 1) * V7X_LANES] = acc.astype(out_ref.dtype)


def dil_merge(outs, lses, dils, *, tm=256):
    m = outs[0].shape[0] * dils[0]
    w = outs[0].shape[1] // dils[0]
    tm = min(tm, m)
    specs = [pl.BlockSpec((tm // d, d * w), lambda i: (i, 0)) for d in dils]
    return pl.pallas_call(
        functools.partial(_dil_merge_body, dils=tuple(dils)),
        grid=(m // tm,),
        in_specs=specs + specs,
        out_specs=pl.BlockSpec((tm, w), lambda i: (i, 0)),
        out_shape=jax.ShapeDtypeStruct((m, w), BF16),
        scratch_shapes=[pltpu.VMEM((tm, V7X_LANES), F32)],
        compiler_params=_params(("arbitrary",), (4 * len(dils) + 4) * tm * w * 4),
        name="dil_merge",
    )(*outs, *lses)


def stick_breaking_mixer(hin, w_qkv, w_o, j):
    qkv = matmul(hin, w_qkv, bm=1024, bn=512, out_dtype=BF16, layer=j)
    o = sb_attention(qkv)
    return matmul(o, w_o, bm=1024, bn=512, out_dtype=F32, layer=j)


def mla_mixer(hin, pos_f32, w_dq, g_q, w_uq, w_dkv, g_kv, w_ukv, w_o, j):
    h, nope, rope = MLA_HEADS, MLA_NOPE, MLA_ROPE
    half = rope // 2
    w_dq, w_uq, w_dkv, g_q, g_kv = w_dq[j], w_uq[j], w_dkv[j], g_q[j], g_kv[j]
    kx1 = w_dkv[:, MLA_KV_LORA:MLA_KV_LORA + half]
    kx2 = w_dkv[:, MLA_KV_LORA + half:]
    w_down = jnp.concatenate(
        [w_dq, w_dkv[:, :MLA_KV_LORA], kx1, kx2, -kx2, kx1,
         jnp.zeros((w_dq.shape[0], V7X_LANES), w_dq.dtype)], axis=1)
    y = matmul(hin, w_down[None], bm=1024, bn=V7X_MXU_DIM, out_dtype=F32)
    cq, ckv, kr, cs = mla_prep(y, pos_f32.reshape(-1, 1), g_q, g_kv)
    wq = w_uq.reshape(MLA_Q_LORA, h, nope + rope)
    qx1, qx2 = wq[:, :, nope:nope + half], wq[:, :, nope + half:]
    wq = jnp.concatenate([wq[:, :, :nope], qx1, qx2, -qx2, qx1], axis=2)
    q_scale = (nope + rope) ** -0.5 * float(np.log2(np.e))
    q = matmul(cq, wq.reshape(1, MLA_Q_LORA, h * 2 * V7X_LANES), bm=1024, bn=1024,
               out_dtype=BF16, rope_cs=cs, out_scale=q_scale)
    wkv = w_ukv[j].reshape(MLA_KV_LORA, h, nope + MLA_V)
    w_k = wkv[:, :, :nope].reshape(1, MLA_KV_LORA, h * nope)
    w_vt = wkv[:, :, nope:].reshape(MLA_KV_LORA, h * MLA_V).T
    blk = min(512, hin.shape[0])
    kn = matmul(ckv, w_k, bm=1024, bn=1024, out_dtype=BF16)
    vt = matmul_nt_tiled(w_vt, ckv, bm=blk, bn=1024, out_dtype=BF16)
    o = mla_attention(q, kn, kr, vt, blk=blk)
    return matmul(o, w_o, bm=1024, bn=512, out_dtype=F32, layer=j)


def swa_mixer(hin, pos_f32, w_qkv, sinks, w_o, j):
    s = hin.shape[0]
    rep = SWA_HEADS // SWA_KV_HEADS
    sinks = sinks[j]
    qkv = matmul(hin, w_qkv, bm=1024, bn=512, out_dtype=BF16, layer=j)
    o = band_attention(
        qkv,
        lambda g: g,
        lambda g: SWA_HEADS + g,
        lambda g: SWA_HEADS + SWA_KV_HEADS + g,
        pos_f32.reshape(1, s, 1), pos_f32.reshape(1, s // BLK, 1, BLK),
        _alibi_slopes(SWA_HEADS), sinks,
        n_seq=1, n_groups=SWA_KV_HEADS, hps=rep, kv_shared=True, seq_len=s,
        window=SWA_WINDOW - 1, out_dtype=BF16, want_lse=False)
    return matmul(o, w_o, bm=1024, bn=512, out_dtype=F32, layer=j)


_DIL_VIEWS = tuple(sorted({dil for _, dil in DIL_PATTERNS if dil > 1}))


def dilated_mixer(hin, hin_views, pos_f32, w_qkv, w_o, j):
    s, d_model = hin.shape
    nh = DIL_HEADS
    n_groups = len(DIL_PATTERNS)
    gw = 3 * nh * DIL_HEAD_DIM
    slopes_all = _alibi_slopes(n_groups * nh)
    views = {1: hin, **dict(zip(_DIL_VIEWS, hin_views))}
    outs, lses = [], []
    for gi, (win, dil) in enumerate(DIL_PATTERNS):
        pos_col, pos_row = _strided_positions(pos_f32, dil)
        qkv = matmul(views[dil], w_qkv, bm=1024, bn=512, out_dtype=BF16,
                     layer=j, n_slice=(gi * gw, gw), row_groups=dil)
        hps = 4
        ng = nh // hps
        o, lse = band_attention(
            qkv,
            lambda g: g,
            lambda g, ng=ng: ng + g,
            lambda g, ng=ng: 2 * ng + g,
            pos_col, pos_row, slopes_all[gi * nh:(gi + 1) * nh], None,
            n_seq=dil, n_groups=ng, hps=hps, kv_shared=False, seq_len=s // dil,
            window=win // dil, out_dtype=F32, want_lse=True)
        outs.append(o)
        lses.append(lse)
    o = dil_merge(outs, lses, [dil for _, dil in DIL_PATTERNS])
    return matmul(o, w_o, bm=1024, bn=512, out_dtype=F32, layer=j)


def kernel(x, positions, norm_g, a_w_qkv, a_w_o, b_w_dq, b_g_q, b_w_uq, b_w_dkv, b_g_kv,
           b_w_ukv, b_w_o, c_w_qkv, c_sinks, c_w_o, d_w_qkv, d_w_o, ffn_w_in, ffn_conv_w,
           ffn_conv_b, ffn_w_out):
    batch, s, d = x.shape
    depth = norm_g.shape[0]
    n_mixers = 4
    outs = []
    for bi in range(batch):
        xb = x[bi]
        pos_f32 = positions[bi].astype(F32)
        hin = norm_rows(xb, norm_g[0, 0])
        hin_views = ()
        for i in range(depth):
            mixer, j = i % n_mixers, i // n_mixers
            if mixer == 0:
                h = stick_breaking_mixer(hin, a_w_qkv, a_w_o, j)
            elif mixer == 1:
                h = mla_mixer(hin, pos_f32, b_w_dq, b_g_q, b_w_uq, b_w_dkv, b_g_kv, b_w_ukv,
                              b_w_o, j)
            elif mixer == 2:
                h = swa_mixer(hin, pos_f32, c_w_qkv, c_sinks, c_w_o, j)
            else:
                h = dilated_mixer(hin, hin_views, pos_f32, d_w_qkv, d_w_o, j)
            xb, xn = resid_norm(h, xb, norm_g[i, 1], norm_g[i, 2])
            act = ffn_in(xn, ffn_w_in, i, ffn_conv_w[i], ffn_conv_b[i])
            h = ffn_out(act, ffn_w_out, i)
            if i + 1 == depth:
                xb = resid_norm(h, xb, norm_g[i, 3])
            elif (i + 1) % n_mixers == 3:
                xb, hin, hin_views = resid_norm(h, xb, norm_g[i, 3], norm_g[i + 1, 0],
                                                dils=_DIL_VIEWS)
            else:
                xb, hin = resid_norm(h, xb, norm_g[i, 3], norm_g[i + 1, 0])
        outs.append(xb)
    return jnp.stack(outs, axis=0)
```

```python
import functools

import numpy as np
import jax
import jax.numpy as jnp
from jax import lax
from jax.experimental import pallas as pl
from jax.experimental.pallas import tpu as pltpu

F32 = jnp.float32
BF16 = jnp.bfloat16

V7X_VMEM_BYTES = 64 * 2**20
V7X_LANES = 128
V7X_MXU_DIM = 256

EPS = 1e-6
BLK = 128
SB_HEADS, SB_HEAD_DIM = 32, 128
MLA_HEADS, MLA_NOPE, MLA_ROPE, MLA_V = 32, 128, 64, 128
MLA_Q_LORA, MLA_KV_LORA = 1024, 512
ROPE_THETA = 10000.0
SWA_HEADS, SWA_KV_HEADS, SWA_HEAD_DIM, SWA_WINDOW = 32, 8, 128, 128
DIL_PATTERNS = ((128, 1), (512, 4), (2048, 16))
DIL_HEADS, DIL_HEAD_DIM = 16, 128
CONV_W = 3

MASKED = -1e30
EXP_ZERO_BELOW = -105.0

_NT = (((1,), (1,)), ((), ()))


def _params(semantics, vmem_bytes):
    limit = min(int(vmem_bytes) + (8 << 20), V7X_VMEM_BYTES - (6 << 20))
    return pltpu.CompilerParams(dimension_semantics=semantics, vmem_limit_bytes=limit)


def _rms(x, g):
    return (x * lax.rsqrt(jnp.mean(x * x, axis=-1, keepdims=True) + EPS)) * g


def _mm_body(*refs, mode, out_scale, sub):
    x_ref, w_ref, o_ref = refs[0], refs[1], refs[-1]
    if sub > 1:
        wb = w_ref[...].astype(BF16)
        rows, k = x_ref.shape[0], x_ref.shape[1] // sub
        for r in range(sub):
            o_ref[r * rows:(r + 1) * rows, :] = jnp.dot(
                x_ref[:, r * k:(r + 1) * k], wb, preferred_element_type=F32).astype(o_ref.dtype)
        return
    acc = jnp.dot(x_ref[...], w_ref[...].astype(BF16), preferred_element_type=F32)
    if mode == "plain":
        o_ref[...] = acc.astype(o_ref.dtype)
    elif mode == "add":
        o_ref[...] = (refs[2][...] + acc).astype(o_ref.dtype)
    else:
        cs = refs[2][...] * out_scale
        for c in range(acc.shape[1] // (2 * V7X_LANES)):
            lo = 2 * c * V7X_LANES
            o_ref[:, lo:lo + V7X_LANES] = (acc[:, lo:lo + V7X_LANES] * out_scale
                                           ).astype(o_ref.dtype)
            o_ref[:, lo + V7X_LANES:lo + 2 * V7X_LANES] = (
                acc[:, lo + V7X_LANES:lo + 2 * V7X_LANES] * cs).astype(o_ref.dtype)


def matmul(x, w, *, bm, bn, out_dtype, layer=0, k_slice=None, n_slice=None, row_groups=1,
           add=None, rope_cs=None, out_scale=1.0):
    n0, n = (0, w.shape[2]) if n_slice is None else n_slice
    if row_groups == 1:
        m, kx = x.shape
    else:
        assert k_slice is None
        m, kx = x.shape[0] * row_groups, x.shape[1] // row_groups
    kb, tk = (0, kx) if k_slice is None else k_slice
    group_rows = m // row_groups
    bm, bn = min(bm, m), min(bn, n)
    sub = max(bm // group_rows, 1)
    assert n % bn == 0 and n0 % bn == 0 and m % bm == 0
    assert (group_rows % bm == 0) if sub == 1 else (bm == sub * group_rows and add is None
                                                    and rope_cs is None)
    per_group = max(group_rows // bm, 1)
    in_specs = [pl.BlockSpec((bm // sub, sub * tk),
                             lambda i, j: (i % per_group, i // per_group + kb)),
                pl.BlockSpec((None, tk, bn), lambda i, j: (layer, kb, n0 // bn + j))]
    args = [x, w]
    mode = "plain"
    aliases = {}
    extra = 0
    if add is not None:
        in_specs.append(pl.BlockSpec((bm, bn), lambda i, j: (i, j)))
        args.append(add)
        mode = "add"
        aliases = {2: 0}
        extra = 2 * bm * bn * add.dtype.itemsize
    if rope_cs is not None:
        in_specs.append(pl.BlockSpec((bm, V7X_LANES), lambda i, j: (i, 0)))
        args.append(rope_cs)
        mode = "rope"
    osize = jnp.dtype(out_dtype).itemsize
    vmem = (2 * bm * tk * 2 + 2 * tk * bn * w.dtype.itemsize + tk * bn * 2
            + 2 * bm * bn * osize + bm * bn * 4 + extra)
    return pl.pallas_call(
        functools.partial(_mm_body, mode=mode, out_scale=out_scale, sub=sub),
        grid=(m // bm, n // bn),
        in_specs=in_specs,
        out_specs=pl.BlockSpec((bm, bn), lambda i, j: (i, j)),
        out_shape=jax.ShapeDtypeStruct((m, n), out_dtype),
        input_output_aliases=aliases,
        compiler_params=_params(("arbitrary", "arbitrary"), vmem),
        name="matmul",
    )(*args)


def _mm_nt_body(w_ref, x_ref, o_ref):
    o_ref[...] = lax.dot_general(w_ref[...].astype(BF16), x_ref[...], _NT,
                                 preferred_element_type=F32).astype(o_ref.dtype)


def matmul_nt_tiled(w_t, x, *, bm, bn, out_dtype):
    n, k = w_t.shape
    m = x.shape[0]
    bm, bn = min(bm, m), min(bn, n)
    assert m % bm == 0 and n % bn == 0
    vmem = 3 * bn * k * 4 + 2 * bm * k * 2 + 2 * bn * bm * jnp.dtype(out_dtype).itemsize + bn * bm * 4
    return pl.pallas_call(
        _mm_nt_body,
        grid=(n // bn, m // bm),
        in_specs=[pl.BlockSpec((bn, k), lambda j, i: (j, 0)),
                  pl.BlockSpec((bm, k), lambda j, i: (i, 0))],
        out_specs=pl.BlockSpec((None, bn, bm), lambda j, i: (i, j, 0)),
        out_shape=jax.ShapeDtypeStruct((m // bm, n, bm), out_dtype),
        compiler_params=_params(("arbitrary", "arbitrary"), vmem),
        name="matmul_nt",
    )(w_t, x)


def _norm_body(x_ref, g_ref, o_ref):
    o_ref[...] = _rms(x_ref[...], g_ref[...]).astype(o_ref.dtype)


def norm_rows(x, g, *, tm=256):
    m, d = x.shape
    tm = min(tm, m)
    return pl.pallas_call(
        _norm_body,
        grid=(m // tm,),
        in_specs=[pl.BlockSpec((tm, d), lambda i: (i, 0)),
                  pl.BlockSpec((1, d), lambda i: (0, 0))],
        out_specs=pl.BlockSpec((tm, d), lambda i: (i, 0)),
        out_shape=jax.ShapeDtypeStruct((m, d), BF16),
        compiler_params=_params(("arbitrary",), 2 * tm * d * 6 + 3 * tm * d * 4),
        name="norm_rows",
    )(x, g.reshape(1, d))


def _resid_norm_body(h_ref, x_ref, ga_ref, gb_ref, xo_ref, no_ref, *rest, dils):
    xn = x_ref[...] + _rms(h_ref[...], ga_ref[...])
    xo_ref[...] = xn
    n = _rms(xn, gb_ref[...])
    no_ref[...] = n.astype(no_ref.dtype)
    if dils:
        n_scr = rest[-1]
        tm, dm = n.shape
        chunks = dm // V7X_LANES
        for c in range(chunks):
            n_scr[c] = n[:, c * V7X_LANES:(c + 1) * V7X_LANES]
        for d, ref in zip(dils, rest[:-1]):
            for r in range(d):
                for c in range(chunks):
                    lo = r * dm + c * V7X_LANES
                    ref[:, lo:lo + V7X_LANES] = n_scr[c, pl.ds(r, tm // d, stride=d), :].astype(
                        ref.dtype)


def _resid_body(h_ref, x_ref, ga_ref, xo_ref):
    xo_ref[...] = x_ref[...] + _rms(h_ref[...], ga_ref[...])


def resid_norm(h, x, g_a, g_b=None, *, tm=128, dils=()):
    m, d = x.shape
    if dils:
        tm = 16 * max(dils)
    tm = min(tm, m)
    row = pl.BlockSpec((tm, d), lambda i: (i, 0))
    gain = pl.BlockSpec((1, d), lambda i: (0, 0))
    if g_b is None:
        return pl.pallas_call(
            _resid_body,
            grid=(m // tm,),
            in_specs=[row, row, gain],
            out_specs=row,
            out_shape=jax.ShapeDtypeStruct((m, d), F32),
            compiler_params=_params(("arbitrary",), 2 * tm * d * 12 + 3 * tm * d * 4),
            name="resid",
        )(h, x, g_a.reshape(1, d))
    outs = pl.pallas_call(
        functools.partial(_resid_norm_body, dils=tuple(dils)),
        grid=(m // tm,),
        in_specs=[row, row, gain, gain],
        out_specs=[row, row] + [pl.BlockSpec((tm // dl, dl * d), lambda i: (i, 0)) for dl in dils],
        out_shape=[jax.ShapeDtypeStruct((m, d), F32), jax.ShapeDtypeStruct((m, d), BF16)]
        + [jax.ShapeDtypeStruct((m // dl, dl * d), BF16) for dl in dils],
        scratch_shapes=[pltpu.VMEM((d // V7X_LANES, tm, V7X_LANES), F32)] if dils else [],
        compiler_params=_params(("arbitrary",),
                                2 * tm * d * (14 + 2 * len(dils)) + 5 * tm * d * 4),
        name="resid_norm",
    )(h, x, g_a.reshape(1, d), g_b.reshape(1, d))
    return (outs[0], outs[1]) + ((tuple(outs[2:]),) if dils else ())


_HALO = 8


def _order_after(x, v):
    zero = (lax.bitcast_convert_type(v, jnp.uint32) >> 16) >> 16
    return pltpu.bitcast(pltpu.bitcast(x, jnp.uint32) | zero, BF16)


def _ffn_in_body(x_ref, wg_ref, wu_ref, cw_ref, cb_ref, wo_ref, o_ref, wo16_ref, wb_ref, g_ref,
                 u_ref, t_ref, *, bm, cm, row_tiles):
    s = pl.program_id(0)
    i = s % row_tiles
    bn = o_ref.shape[1]
    nc = bm // cm

    @pl.when(s == 0)
    def _():
        g_ref[...] = jnp.zeros(g_ref.shape, F32)
        u_ref[...] = jnp.zeros(u_ref.shape, F32)

    @pl.when(i == 0)
    def _():
        wb_ref[:, 0:bn] = wg_ref[...].astype(BF16)
        wb_ref[:, bn:2 * bn] = wu_ref[...].astype(BF16)

    wo16_ref[...] = wo_ref[...].astype(BF16)

    cw = cw_ref[...]
    cb = cb_ref[...]
    for c in range(nc):
        rows = slice(c * cm, (c + 1) * cm)
        lo = _HALO + c * cm
        g = g_ref[lo:lo + cm, :]
        g1 = g_ref[lo - 1:lo - 1 + cm, :]
        g2 = g_ref[lo - 2:lo - 2 + cm, :]
        gate = cw[0:1, :] * g2 + cw[1:2, :] * g1 + cw[2:3, :] * g + cb
        act = (gate / (1.0 + jnp.exp(-gate))) * u_ref[rows, :]
        o_ref[rows, :] = act.astype(o_ref.dtype)
        if c == 0:
            g_ref[0:_HALO, :] = jnp.where(i == 0, 0.0, g_ref[bm:bm + _HALO, :])
        else:
            g_ref[lo - _HALO:lo, :] = t_ref[c - 1]
        x0 = _order_after(x_ref[rows, 0:V7X_LANES], act[0:cm // 2, 0:V7X_LANES])
        x = jnp.concatenate([x0, x_ref[rows, V7X_LANES:]], axis=1)
        gu = jnp.dot(x, wb_ref[...], preferred_element_type=F32)
        u_ref[rows, :] = gu[:, bn:2 * bn]
        if c == nc - 1:
            g_ref[lo:lo + cm, :] = gu[:, 0:bn]
        else:
            g_ref[lo:lo + cm - _HALO, :] = gu[0:cm - _HALO, 0:bn]
            t_ref[c] = gu[cm - _HALO:cm, 0:bn]


def ffn_in(xn, w_in, layer, conv_w, conv_b, w_out, *, bm=1024, bn=V7X_MXU_DIM, cm=128):
    m, k = xn.shape
    d_ff = w_in.shape[2] // 2
    d_out = w_out.shape[2]
    bm = min(bm, m)
    bn = min(bn, d_ff)
    assert m % bm == 0 and d_ff % bn == 0
    nj, ni = d_ff // bn, m // bm
    tiles = nj * ni
    cm = min(cm, bm)
    assert bm % cm == 0 and cm % (2 * _HALO) == 0
    slab = d_ff // tiles
    assert slab * tiles == d_ff and slab % 16 == 0

    def mm(f):
        return lambda s: f(jnp.minimum(s, tiles - 1) // ni, jnp.minimum(s, tiles - 1) % ni)

    def ep(f):
        return lambda s: f(jnp.maximum(s - 1, 0) // ni, jnp.maximum(s - 1, 0) % ni)

    vmem = (2 * bm * k * 2 + 2 * 2 * k * bn * 4 + 2 * k * bn * 2 + 2 * bm * bn * 2
            + (2 * bm + _HALO) * bn * 4 + 6 * bm * bn * 4)
    return pl.pallas_call(
        functools.partial(_ffn_in_body, bm=bm, cm=cm, row_tiles=ni),
        grid=(tiles + 1,),
        in_specs=[pl.BlockSpec((bm, k), mm(lambda j, i: (i, 0))),
                  pl.BlockSpec((None, k, bn), mm(lambda j, i: (layer, 0, j))),
                  pl.BlockSpec((None, k, bn), mm(lambda j, i: (layer, 0, nj + j))),
                  pl.BlockSpec((CONV_W, bn), ep(lambda j, i: (0, j))),
                  pl.BlockSpec((1, bn), ep(lambda j, i: (0, j))),
                  pl.BlockSpec((None, slab, d_out),
                               lambda s: (layer, jnp.minimum(s, tiles - 1), 0))],
        out_specs=[pl.BlockSpec((bm, bn), ep(lambda j, i: (i, j))),
                   pl.BlockSpec((slab, d_out), lambda s: (jnp.minimum(s, tiles - 1), 0))],
        out_shape=[jax.ShapeDtypeStruct((m, d_ff), BF16),
                   jax.ShapeDtypeStruct((d_ff, d_out), BF16)],
        scratch_shapes=[pltpu.VMEM((k, 2 * bn), BF16), pltpu.VMEM((bm + _HALO, bn), F32),
                        pltpu.VMEM((bm, bn), F32),
                        pltpu.VMEM((max(bm // cm - 1, 1), _HALO, bn), F32)],
        compiler_params=_params(("arbitrary",), vmem + 12 * slab * d_out),
        name="ffn_in",
    )(xn, w_in, w_in, conv_w, conv_b.reshape(1, d_ff), w_out)


def ffn_out(act, w_out16, *, bm=1024, bn=2 * V7X_MXU_DIM):
    d_ff = act.shape[1]
    w = w_out16[None]
    if d_ff % (2 * V7X_LANES) != 0:
        return matmul(act, w, bm=bm, bn=bn, out_dtype=F32)
    half = d_ff // 2
    part = matmul(act, w, bm=bm, bn=bn, out_dtype=F32, k_slice=(0, half))
    return matmul(act, w, bm=bm, bn=bn, out_dtype=F32, k_slice=(1, half), add=part)


_SB_HEADS_PER_STEP = 4


def _sb_attn_body(q_ref, k_ref, v_ref, tri_ref, o_ref, acc_ref, c_ref, *, blk, scale):
    qi = pl.program_id(1)
    nh = _SB_HEADS_PER_STEP
    dh = SB_HEAD_DIM
    tri = tri_ref[...]
    acc_ref[...] = jnp.zeros(acc_ref.shape, F32)
    c_ref[...] = jnp.zeros(c_ref.shape, F32)
    key = lax.broadcasted_iota(jnp.int32, (blk, blk), 0)
    qry = lax.broadcasted_iota(jnp.int32, (blk, blk), 1)
    causal = key < qry

    def block(kb, diagonal):
        start = pl.multiple_of(kb * blk, blk)
        heads = range(nh)
        cols = [slice(hd * dh, (hd + 1) * dh) for hd in heads]
        z = [lax.dot_general(k_ref[pl.ds(start, blk), cols[hd]], q_ref[:, cols[hd]], _NT,
                             preferred_element_type=F32) * scale for hd in heads]
        vt = [v_ref[pl.ds(start, blk), cols[hd]].astype(F32).T.astype(BF16) for hd in heads]
        log_beta, log_1mb, hi, lo = [], [], [], []
        for hd in heads:
            sp = jnp.log(1.0 + jnp.exp(-jnp.abs(z[hd])))
            log_beta.append(jnp.minimum(z[hd], 0.0) - sp)
            l1 = jnp.minimum(-z[hd], 0.0) - sp
            if diagonal:
                l1 = jnp.where(causal, l1, 0.0)
            log_1mb.append(l1)
            hi.append(l1.astype(BF16))
            lo.append((l1 - hi[hd].astype(F32)).astype(BF16))
        tail = [jnp.dot(tri, hi[hd], preferred_element_type=F32)
                + jnp.dot(tri, lo[hd], preferred_element_type=F32) for hd in heads]
        c = [c_ref[hd] for hd in heads]
        a = []
        for hd in heads:
            w = jnp.exp(log_beta[hd] + tail[hd] + c[hd])
            if diagonal:
                w = jnp.where(causal, w, 0.0)
            a.append(w.astype(BF16))
        top = None
        for hd in heads:
            acc_ref[hd] += jnp.dot(vt[hd], a[hd], preferred_element_type=F32)
            c_new = c[hd] + tail[hd][0:1, :] + log_1mb[hd][0:1, :]
            c_ref[hd] = c_new
            top = jnp.max(c_new) if top is None else jnp.maximum(top, jnp.max(c_new))
        return top

    top = block(qi, True)
    lax.while_loop(lambda st: (st[0] >= 0) & (st[1] > EXP_ZERO_BELOW),
                   lambda st: (st[0] - 1, block(st[0], False)),
                   (qi - 1, top))
    for hd in range(nh):
        o_ref[:, hd * dh:(hd + 1) * dh] = acc_ref[hd].T.astype(o_ref.dtype)


def sb_attention(qkv, *, blk=256):
    s = qkv.shape[0]
    h, dh = SB_HEADS, SB_HEAD_DIM
    nh = _SB_HEADS_PER_STEP
    blk = min(blk, s)
    groups = h // nh
    tri = (np.arange(blk)[:, None] < np.arange(blk)[None, :]).astype(np.float32)
    vmem = (2 * 2 * s * nh * dh * 2 + 4 * blk * nh * dh * 2 + 2 * blk * blk * 2
            + nh * (dh + 8) * blk * 4 + nh * 12 * blk * blk * 4)
    return pl.pallas_call(
        functools.partial(_sb_attn_body, blk=blk, scale=dh ** -0.5),
        grid=(groups, s // blk),
        in_specs=[pl.BlockSpec((blk, nh * dh), lambda g, i: (i, g)),
                  pl.BlockSpec((s, nh * dh), lambda g, i: (0, groups + g)),
                  pl.BlockSpec((s, nh * dh), lambda g, i: (0, 2 * groups + g)),
                  pl.BlockSpec((blk, blk), lambda g, i: (0, 0))],
        out_specs=pl.BlockSpec((blk, nh * dh), lambda g, i: (i, g)),
        out_shape=jax.ShapeDtypeStruct((s, h * dh), BF16),
        scratch_shapes=[pltpu.VMEM((nh, dh, blk), F32), pltpu.VMEM((nh, 1, blk), F32)],
        compiler_params=_params(("arbitrary", "arbitrary"), vmem),
        name="sb_attention",
    )(qkv, qkv, qkv, jnp.asarray(tri, BF16))


def _mla_prep_body(y_ref, pos_ref, inv_ref, gq_ref, gkv_ref, cq_ref, ckv_ref, kr_ref, cs_ref):
    ql, kvl = MLA_Q_LORA, MLA_KV_LORA
    cq_ref[...] = _rms(y_ref[:, 0:ql], gq_ref[...]).astype(cq_ref.dtype)
    ckv_ref[...] = _rms(y_ref[:, ql:ql + kvl], gkv_ref[...]).astype(ckv_ref.dtype)
    ang = pos_ref[...] * inv_ref[...]
    lane = lax.broadcasted_iota(jnp.int32, ang.shape, 1)
    cs = jnp.where(lane < MLA_ROPE, jnp.cos(ang), jnp.sin(ang))
    cs_ref[...] = cs
    w = y_ref[:, ql + kvl:ql + kvl + V7X_LANES] * cs
    kr_ref[...] = (w + pltpu.roll(w, MLA_ROPE, axis=1)).astype(kr_ref.dtype)


def mla_prep(y, pos_col, g_q, g_kv, *, tm=256):
    m, width = y.shape
    tm = min(tm, m)
    half = MLA_ROPE // 2
    inv = ROPE_THETA ** (-jnp.arange(half, dtype=F32) / half)
    inv4 = jnp.tile(inv, 4).reshape(1, V7X_LANES)
    row = lambda w: pl.BlockSpec((tm, w), lambda i: (i, 0))
    const = lambda w: pl.BlockSpec((1, w), lambda i: (0, 0))
    return pl.pallas_call(
        _mla_prep_body,
        grid=(m // tm,),
        in_specs=[row(width), row(1), const(V7X_LANES), const(MLA_Q_LORA), const(MLA_KV_LORA)],
        out_specs=[row(MLA_Q_LORA), row(MLA_KV_LORA), row(V7X_LANES), row(V7X_LANES)],
        out_shape=[jax.ShapeDtypeStruct((m, MLA_Q_LORA), BF16),
                   jax.ShapeDtypeStruct((m, MLA_KV_LORA), BF16),
                   jax.ShapeDtypeStruct((m, V7X_LANES), BF16),
                   jax.ShapeDtypeStruct((m, V7X_LANES), F32)],
        compiler_params=_params(("arbitrary",), 6 * tm * width * 4),
        name="mla_prep",
    )(y, pos_col, inv4, g_q.reshape(1, -1), g_kv.reshape(1, -1))


_MLA_HEADS_PER_STEP = 2


def _mla_attn_body(q_ref, kn_ref, kr_ref, vt_ref, o_ref, s_ref, m_ref, l_ref, acc_ref, *, blk):
    qi = pl.program_id(1)
    nh = _MLA_HEADS_PER_STEP
    wq = MLA_NOPE + V7X_LANES
    m_ref[...] = jnp.full(m_ref.shape, MASKED, F32)
    l_ref[...] = jnp.zeros(l_ref.shape, F32)
    acc_ref[...] = jnp.zeros(acc_ref.shape, F32)
    key = lax.broadcasted_iota(jnp.int32, (blk, blk), 0)
    qry = lax.broadcasted_iota(jnp.int32, (blk, blk), 1)

    def scores(kb, slot):
        start = pl.multiple_of(kb * blk, blk)
        kr = kr_ref[pl.ds(start, blk), :]
        for hd in range(nh):
            kcat = jnp.concatenate(
                [kn_ref[pl.ds(start, blk), hd * MLA_NOPE:(hd + 1) * MLA_NOPE], kr], axis=1)
            s_ref[slot, hd] = lax.dot_general(kcat, q_ref[:, hd * wq:(hd + 1) * wq], _NT,
                                              preferred_element_type=F32)

    def reduce(kb, slot, diagonal):
        heads = range(nh)
        st = [s_ref[slot, hd] for hd in heads]
        if diagonal:
            st = [jnp.where(key <= qry, t, MASKED) for t in st]
        m_prev = [m_ref[hd] for hd in heads]
        m_new = [jnp.maximum(m_prev[hd], jnp.max(st[hd], axis=0, keepdims=True)) for hd in heads]
        p = [jnp.exp2(st[hd] - m_new[hd]) for hd in heads]
        pv = [jnp.dot(vt_ref[kb, hd * MLA_V:(hd + 1) * MLA_V, :], p[hd].astype(BF16),
                      preferred_element_type=F32) for hd in heads]
        for hd in heads:
            alpha = jnp.exp2(m_prev[hd] - m_new[hd])
            l_ref[hd] = alpha * l_ref[hd] + jnp.sum(p[hd], axis=0, keepdims=True)
            acc_ref[hd] = alpha * acc_ref[hd] + pv[hd]
            m_ref[hd] = m_new[hd]

    def off_diagonal_pair(t, carry):
        scores(2 * t + 1, 1)
        reduce(2 * t, 0, False)
        scores(2 * t + 2, 0)
        reduce(2 * t + 1, 1, False)
        return carry

    scores(0, 0)
    lax.fori_loop(0, qi // 2, off_diagonal_pair, 0)

    @pl.when(qi % 2 == 0)
    def _():
        reduce(qi, 0, True)

    @pl.when(qi % 2 == 1)
    def _():
        scores(qi, 1)
        reduce(qi - 1, 0, False)
        reduce(qi, 1, True)

    for hd in range(nh):
        o_ref[:, hd * MLA_V:(hd + 1) * MLA_V] = (acc_ref[hd] / l_ref[hd]).T.astype(o_ref.dtype)


def mla_attention(q, kn, kr, vt, *, blk):
    s = q.shape[0]
    h = MLA_HEADS
    nh = _MLA_HEADS_PER_STEP
    wq = MLA_NOPE + V7X_LANES
    assert vt.shape == (s // blk, h * MLA_V, blk)
    vmem = (2 * blk * nh * wq * 2 + 2 * s * nh * MLA_NOPE * 2 + 2 * s * V7X_LANES * 2
            + 2 * s * nh * MLA_V * 2 + 2 * blk * nh * MLA_V * 2 + nh * (MLA_V + 16) * blk * 4
            + nh * 8 * blk * blk * 4)
    return pl.pallas_call(
        functools.partial(_mla_attn_body, blk=blk),
        grid=(h // nh, s // blk),
        in_specs=[pl.BlockSpec((blk, nh * wq), lambda hp, i: (i, hp)),
                  pl.BlockSpec((s, nh * MLA_NOPE), lambda hp, i: (0, hp)),
                  pl.BlockSpec((s, V7X_LANES), lambda hp, i: (0, 0)),
                  pl.BlockSpec((s // blk, nh * MLA_V, blk), lambda hp, i: (0, hp, 0))],
        out_specs=pl.BlockSpec((blk, nh * MLA_V), lambda hp, i: (i, hp)),
        out_shape=jax.ShapeDtypeStruct((s, h * MLA_V), BF16),
        scratch_shapes=[pltpu.VMEM((2, nh, blk, blk), F32),
                        pltpu.VMEM((nh, 1, blk), F32), pltpu.VMEM((nh, 1, blk), F32),
                        pltpu.VMEM((nh, MLA_V, blk), F32)],
        compiler_params=_params(("arbitrary", "arbitrary"), vmem),
        name="mla_attention",
    )(q, kn, kr, vt)


def _band_attn_body(*refs, tile, hps, kv_shared, window, scale, has_sink, want_lse):
    refs = list(refs)
    slope_ref = refs.pop(0)
    sink_ref = refs.pop(0) if has_sink else None
    q_ref, k_ref, v_ref, pq_ref, pk_ref, o_ref = refs[:6]
    lse_ref = refs[6] if want_lse else None
    hbase = pl.program_id(1) * hps
    t = pl.program_id(2)
    nb = tile // BLK
    heads = range(hps)
    qi = lax.broadcasted_iota(jnp.int32, (BLK, 2 * BLK), 0)
    kj = lax.broadcasted_iota(jnp.int32, (BLK, 2 * BLK), 1)
    slopes = [slope_ref[hbase + h] for h in heads]
    sinks = [sink_ref[hbase + h] for h in heads] if has_sink else None
    for b in range(nb):
        gb = t * nb + b
        pb = jnp.maximum(gb - 1, 0)
        start = pl.multiple_of(pb * BLK, BLK)
        rows = slice(b * BLK, (b + 1) * BLK)
        pk = jnp.concatenate([pk_ref[pb], pk_ref[pb + 1]], axis=1)
        delta = (gb - pb) * BLK + qi - kj
        valid = (delta >= 0) & (delta <= window)
        dist = jnp.where(valid, pq_ref[rows, :] - pk, -MASKED)
        cols = [slice(h * BLK, (h + 1) * BLK) for h in heads]
        kv_cols = [slice(0, BLK)] * hps if kv_shared else cols
        sc = [lax.dot_general(q_ref[rows, cols[h]], k_ref[pl.ds(start, 2 * BLK), kv_cols[h]], _NT,
                              preferred_element_type=F32) * scale - slopes[h] * dist
              for h in heads]
        m = [jnp.max(sc[h], axis=1, keepdims=True) for h in heads]
        if has_sink:
            m = [jnp.maximum(m[h], sinks[h]) for h in heads]
        p = [jnp.exp(sc[h] - m[h]) for h in heads]
        den = [jnp.sum(p[h], axis=1, keepdims=True) for h in heads]
        if has_sink:
            den = [den[h] + jnp.exp(sinks[h] - m[h]) for h in heads]
        o = [jnp.dot((p[h] * (1.0 / den[h])).astype(BF16),
                     v_ref[pl.ds(start, 2 * BLK), kv_cols[h]], preferred_element_type=F32)
             for h in heads]
        for h in heads:
            o_ref[rows, cols[h]] = o[h].astype(o_ref.dtype)
            if want_lse:
                lse_ref[rows, cols[h]] = jnp.broadcast_to(m[h] + jnp.log(den[h]), (BLK, BLK))


def band_attention(arr, col_q, col_k, col_v, pos_col, pos_row, slopes, sinks, *,
                   n_seq, n_groups, hps, kv_shared, seq_len, window, out_dtype, want_lse):
    dh = BLK
    tile = min(seq_len, 4 * BLK)
    tiles = seq_len // tile
    has_sink = sinks is not None
    kvw = dh if kv_shared else hps * dh
    smem = pl.BlockSpec(memory_space=pltpu.SMEM)
    in_specs = [smem] + ([smem] if has_sink else []) + [
        pl.BlockSpec((tile, hps * dh), lambda r, g, t: (r * tiles + t, col_q(g))),
        pl.BlockSpec((seq_len, kvw), lambda r, g, t: (r, col_k(g))),
        pl.BlockSpec((seq_len, kvw), lambda r, g, t: (r, col_v(g))),
        pl.BlockSpec((None, tile, 1), lambda r, g, t: (r, t, 0)),
        pl.BlockSpec((None, seq_len // BLK, 1, BLK), lambda r, g, t: (r, 0, 0, 0)),
    ]
    args = [slopes] + ([sinks] if has_sink else []) + [arr, arr, arr, pos_col, pos_row]
    out_block = pl.BlockSpec((tile, hps * dh), lambda r, g, t: (t, r * n_groups + g))
    out_cols = n_seq * n_groups * hps * dh
    out_sds = jax.ShapeDtypeStruct((seq_len, out_cols), out_dtype)
    osize = jnp.dtype(out_dtype).itemsize
    vmem = (2 * tile * hps * dh * 2 + 2 * 2 * seq_len * kvw * 2 + 4 * tile * hps * dh * osize
            + 2 * tile * V7X_LANES * 4 + 2 * seq_len * 4 * 8 + hps * 8 * BLK * 2 * BLK * 4)
    return pl.pallas_call(
        functools.partial(_band_attn_body, tile=tile, hps=hps, kv_shared=kv_shared, window=window,
                          scale=dh ** -0.5, has_sink=has_sink, want_lse=want_lse),
        grid=(n_seq, n_groups, seq_len // tile),
        in_specs=in_specs,
        out_specs=[out_block, out_block] if want_lse else out_block,
        out_shape=[out_sds, jax.ShapeDtypeStruct((seq_len, out_cols), F32)] if want_lse else out_sds,
        compiler_params=_params(("arbitrary", "arbitrary", "arbitrary"), vmem),
        name="band_attention",
    )(*args)


def _alibi_slopes(n):
    return jnp.asarray(2.0 ** (-8.0 * np.arange(1, n + 1) / n), dtype=F32)


def _strided_positions(pos_f32, dil):
    s = pos_f32.shape[0]
    ps = pos_f32.reshape(s // dil, dil).T
    return ps.reshape(dil, s // dil, 1), ps.reshape(dil, s // dil // BLK, 1, BLK)


def _dil_merge_body(*refs, dils):
    n = len(dils)
    o_refs, l_refs, out_ref, buf = refs[:n], refs[n:2 * n], refs[2 * n], refs[2 * n + 1]
    tm, w = out_ref.shape

    def token_order(ref, d, c):
        if d == 1:
            return ref[:, c * V7X_LANES:(c + 1) * V7X_LANES]
        for r in range(d):
            lo = r * w + c * V7X_LANES
            buf[pl.ds(r, tm // d, stride=d), :] = ref[:, lo:lo + V7X_LANES]
        return buf[...]

    for c in range(w // V7X_LANES):
        lse = [token_order(l_refs[g], dils[g], c) for g in range(n)]
        m = functools.reduce(jnp.maximum, lse)
        e = [jnp.exp(l - m) for l in lse]
        inv = 1.0 / functools.reduce(lambda a, b: a + b, e)
        acc = None
        for g in range(n):
            term = (e[g] * inv) * token_order(o_refs[g], dils[g], c)
            acc = term if acc is None else acc + term
        out_ref[:, c * V7X_LANES:(c + 1) * V7X_LANES] = acc.astype(out_ref.dtype)


def dil_merge(outs, lses, dils, *, tm=256):
    m = outs[0].shape[0] * dils[0]
    w = outs[0].shape[1] // dils[0]
    tm = min(tm, m)
    specs = [pl.BlockSpec((tm // d, d * w), lambda i: (i, 0)) for d in dils]
    return pl.pallas_call(
        functools.partial(_dil_merge_body, dils=tuple(dils)),
        grid=(m // tm,),
        in_specs=specs + specs,
        out_specs=pl.BlockSpec((tm, w), lambda i: (i, 0)),
        out_shape=jax.ShapeDtypeStruct((m, w), BF16),
        scratch_shapes=[pltpu.VMEM((tm, V7X_LANES), F32)],
        compiler_params=_params(("arbitrary",), (4 * len(dils) + 4) * tm * w * 4),
        name="dil_merge",
    )(*outs, *lses)


def stick_breaking_mixer(hin, w_qkv, w_o, j):
    qkv = matmul(hin, w_qkv, bm=1024, bn=512, out_dtype=BF16, layer=j)
    o = sb_attention(qkv)
    return matmul(o, w_o, bm=1024, bn=512, out_dtype=F32, layer=j)


def mla_mixer(hin, pos_f32, w_dq, g_q, w_uq, w_dkv, g_kv, w_ukv, w_o, j):
    h, nope, rope = MLA_HEADS, MLA_NOPE, MLA_ROPE
    half = rope // 2
    w_dq, w_uq, w_dkv, g_q, g_kv = w_dq[j], w_uq[j], w_dkv[j], g_q[j], g_kv[j]
    kx1 = w_dkv[:, MLA_KV_LORA:MLA_KV_LORA + half]
    kx2 = w_dkv[:, MLA_KV_LORA + half:]
    w_down = jnp.concatenate(
        [w_dq, w_dkv[:, :MLA_KV_LORA], kx1, kx2, -kx2, kx1,
         jnp.zeros((w_dq.shape[0], V7X_LANES), w_dq.dtype)], axis=1)
    y = matmul(hin, w_down[None], bm=1024, bn=V7X_MXU_DIM, out_dtype=F32)
    cq, ckv, kr, cs = mla_prep(y, pos_f32.reshape(-1, 1), g_q, g_kv)
    wq = w_uq.reshape(MLA_Q_LORA, h, nope + rope)
    qx1, qx2 = wq[:, :, nope:nope + half], wq[:, :, nope + half:]
    wq = jnp.concatenate([wq[:, :, :nope], qx1, qx2, -qx2, qx1], axis=2)
    q_scale = (nope + rope) ** -0.5 * float(np.log2(np.e))
    q = matmul(cq, wq.reshape(1, MLA_Q_LORA, h * 2 * V7X_LANES), bm=1024, bn=1024,
               out_dtype=BF16, rope_cs=cs, out_scale=q_scale)
    wkv = w_ukv[j].reshape(MLA_KV_LORA, h, nope + MLA_V)
    w_k = wkv[:, :, :nope].reshape(1, MLA_KV_LORA, h * nope)
    w_vt = wkv[:, :, nope:].reshape(MLA_KV_LORA, h * MLA_V).T
    blk = min(512, hin.shape[0])
    kn = matmul(ckv, w_k, bm=1024, bn=1024, out_dtype=BF16)
    vt = matmul_nt_tiled(w_vt, ckv, bm=blk, bn=1024, out_dtype=BF16)
    o = mla_attention(q, kn, kr, vt, blk=blk)
    return matmul(o, w_o, bm=1024, bn=512, out_dtype=F32, layer=j)


def swa_mixer(hin, pos_f32, w_qkv, sinks, w_o, j):
    s = hin.shape[0]
    rep = SWA_HEADS // SWA_KV_HEADS
    sinks = sinks[j]
    qkv = matmul(hin, w_qkv, bm=1024, bn=512, out_dtype=BF16, layer=j)
    o = band_attention(
        qkv,
        lambda g: g,
        lambda g: SWA_HEADS + g,
        lambda g: SWA_HEADS + SWA_KV_HEADS + g,
        pos_f32.reshape(1, s, 1), pos_f32.reshape(1, s // BLK, 1, BLK),
        _alibi_slopes(SWA_HEADS), sinks,
        n_seq=1, n_groups=SWA_KV_HEADS, hps=rep, kv_shared=True, seq_len=s,
        window=SWA_WINDOW - 1, out_dtype=BF16, want_lse=False)
    return matmul(o, w_o, bm=1024, bn=512, out_dtype=F32, layer=j)


_DIL_VIEWS = tuple(sorted({dil for _, dil in DIL_PATTERNS if dil > 1}))


def dilated_mixer(hin, hin_views, pos_f32, w_qkv, w_o, j):
    s, d_model = hin.shape
    nh = DIL_HEADS
    n_groups = len(DIL_PATTERNS)
    gw = 3 * nh * DIL_HEAD_DIM
    slopes_all = _alibi_slopes(n_groups * nh)
    views = {1: hin, **dict(zip(_DIL_VIEWS, hin_views))}
    outs, lses = [], []
    for gi, (win, dil) in enumerate(DIL_PATTERNS):
        pos_col, pos_row = _strided_positions(pos_f32, dil)
        qkv = matmul(views[dil], w_qkv, bm=1024, bn=512, out_dtype=BF16,
                     layer=j, n_slice=(gi * gw, gw), row_groups=dil)
        hps = 4
        ng = nh // hps
        o, lse = band_attention(
            qkv,
            lambda g: g,
            lambda g, ng=ng: ng + g,
            lambda g, ng=ng: 2 * ng + g,
            pos_col, pos_row, slopes_all[gi * nh:(gi + 1) * nh], None,
            n_seq=dil, n_groups=ng, hps=hps, kv_shared=False, seq_len=s // dil,
            window=win // dil, out_dtype=F32, want_lse=True)
        outs.append(o)
        lses.append(lse)
    o = dil_merge(outs, lses, [dil for _, dil in DIL_PATTERNS])
    return matmul(o, w_o, bm=1024, bn=512, out_dtype=F32, layer=j)


def kernel(x, positions, norm_g, a_w_qkv, a_w_o, b_w_dq, b_g_q, b_w_uq, b_w_dkv, b_g_kv,
           b_w_ukv, b_w_o, c_w_qkv, c_sinks, c_w_o, d_w_qkv, d_w_o, ffn_w_in, ffn_conv_w,
           ffn_conv_b, ffn_w_out):
    batch, s, d = x.shape
    depth = norm_g.shape[0]
    n_mixers = 4
    outs = []
    for bi in range(batch):
        xb = x[bi]
        pos_f32 = positions[bi].astype(F32)
        hin = norm_rows(xb, norm_g[0, 0])
        hin_views = ()
        for i in range(depth):
            mixer, j = i % n_mixers, i // n_mixers
            if mixer == 0:
                h = stick_breaking_mixer(hin, a_w_qkv, a_w_o, j)
            elif mixer == 1:
                h = mla_mixer(hin, pos_f32, b_w_dq, b_g_q, b_w_uq, b_w_dkv, b_g_kv, b_w_ukv,
                              b_w_o, j)
            elif mixer == 2:
                h = swa_mixer(hin, pos_f32, c_w_qkv, c_sinks, c_w_o, j)
            else:
                h = dilated_mixer(hin, hin_views, pos_f32, d_w_qkv, d_w_o, j)
            xb, xn = resid_norm(h, xb, norm_g[i, 1], norm_g[i, 2])
            act, w_out16 = ffn_in(xn, ffn_w_in, i, ffn_conv_w[i], ffn_conv_b[i], ffn_w_out)
            h = ffn_out(act, w_out16)
            if i + 1 == depth:
                xb = resid_norm(h, xb, norm_g[i, 3])
            elif (i + 1) % n_mixers == 3:
                xb, hin, hin_views = resid_norm(h, xb, norm_g[i, 3], norm_g[i + 1, 0],
                                                dils=_DIL_VIEWS)
            else:
                xb, hin = resid_norm(h, xb, norm_g[i, 3], norm_g[i + 1, 0])
        outs.append(xb)
    return jnp.stack(outs, axis=0)
```

```python
import functools

import numpy as np
import jax
import jax.numpy as jnp
from jax import lax
from jax.experimental import pallas as pl
from jax.experimental.pallas import tpu as pltpu

F32 = jnp.float32
BF16 = jnp.bfloat16

V7X_VMEM_BYTES = 64 * 2**20
V7X_LANES = 128
V7X_MXU_DIM = 256

EPS = 1e-6
BLK = 128
SB_HEADS, SB_HEAD_DIM = 32, 128
MLA_HEADS, MLA_NOPE, MLA_ROPE, MLA_V = 32, 128, 64, 128
MLA_Q_LORA, MLA_KV_LORA = 1024, 512
ROPE_THETA = 10000.0
SWA_HEADS, SWA_KV_HEADS, SWA_HEAD_DIM, SWA_WINDOW = 32, 8, 128, 128
DIL_PATTERNS = ((128, 1), (512, 4), (2048, 16))
DIL_HEADS, DIL_HEAD_DIM = 16, 128
CONV_W = 3

MASKED = -1e30
EXP_ZERO_BELOW = -105.0

_NT = (((1,), (1,)), ((), ()))


_VMEM_SPILL_ALLOWANCE = 8 * 2**20
_VMEM_RESERVE = 6 * 2**20


def _params(semantics, vmem_bytes):
    limit = min(int(vmem_bytes) + _VMEM_SPILL_ALLOWANCE, V7X_VMEM_BYTES - _VMEM_RESERVE)
    return pltpu.CompilerParams(dimension_semantics=semantics, vmem_limit_bytes=limit)


def _rms(x, g):
    return (x * lax.rsqrt(jnp.mean(x * x, axis=-1, keepdims=True) + EPS)) * g


def _mm_body(*refs, mode, out_scale, sub):
    x_ref, w_ref, o_ref = refs[0], refs[1], refs[-1]
    if sub > 1:
        wb = w_ref[...].astype(BF16)
        rows, k = x_ref.shape[0], x_ref.shape[1] // sub
        for r in range(sub):
            o_ref[r * rows:(r + 1) * rows, :] = jnp.dot(
                x_ref[:, r * k:(r + 1) * k], wb, preferred_element_type=F32).astype(o_ref.dtype)
        return
    acc = jnp.dot(x_ref[...], w_ref[...].astype(BF16), preferred_element_type=F32)
    if mode == "plain":
        o_ref[...] = acc.astype(o_ref.dtype)
    elif mode == "add":
        o_ref[...] = (refs[2][...] + acc).astype(o_ref.dtype)
    else:
        cs = refs[2][...] * out_scale
        for c in range(acc.shape[1] // (2 * V7X_LANES)):
            lo = 2 * c * V7X_LANES
            o_ref[:, lo:lo + V7X_LANES] = (acc[:, lo:lo + V7X_LANES] * out_scale
                                           ).astype(o_ref.dtype)
            o_ref[:, lo + V7X_LANES:lo + 2 * V7X_LANES] = (
                acc[:, lo + V7X_LANES:lo + 2 * V7X_LANES] * cs).astype(o_ref.dtype)


def matmul(x, w, *, bm, bn, out_dtype, layer=0, k_slice=None, n_slice=None, row_groups=1,
           add=None, rope_cs=None, out_scale=1.0):
    n0, n = (0, w.shape[2]) if n_slice is None else n_slice
    if row_groups == 1:
        m, kx = x.shape
    else:
        assert k_slice is None
        m, kx = x.shape[0] * row_groups, x.shape[1] // row_groups
    kb, tk = (0, kx) if k_slice is None else k_slice
    group_rows = m // row_groups
    bm, bn = min(bm, m), min(bn, n)
    sub = max(bm // group_rows, 1)
    assert n % bn == 0 and n0 % bn == 0 and m % bm == 0
    assert (group_rows % bm == 0) if sub == 1 else (bm == sub * group_rows and add is None
                                                    and rope_cs is None)
    per_group = max(group_rows // bm, 1)
    in_specs = [pl.BlockSpec((bm // sub, sub * tk),
                             lambda i, j: (i % per_group, i // per_group + kb)),
                pl.BlockSpec((None, tk, bn), lambda i, j: (layer, kb, n0 // bn + j))]
    args = [x, w]
    mode = "plain"
    aliases = {}
    extra = 0
    if add is not None:
        in_specs.append(pl.BlockSpec((bm, bn), lambda i, j: (i, j)))
        args.append(add)
        mode = "add"
        aliases = {2: 0}
        extra = 2 * bm * bn * add.dtype.itemsize
    if rope_cs is not None:
        in_specs.append(pl.BlockSpec((bm, V7X_LANES), lambda i, j: (i, 0)))
        args.append(rope_cs)
        mode = "rope"
    osize = jnp.dtype(out_dtype).itemsize
    vmem = (2 * bm * tk * 2 + 2 * tk * bn * w.dtype.itemsize + tk * bn * 2
            + 2 * bm * bn * osize + bm * bn * 4 + extra)
    return pl.pallas_call(
        functools.partial(_mm_body, mode=mode, out_scale=out_scale, sub=sub),
        grid=(m // bm, n // bn),
        in_specs=in_specs,
        out_specs=pl.BlockSpec((bm, bn), lambda i, j: (i, j)),
        out_shape=jax.ShapeDtypeStruct((m, n), out_dtype),
        input_output_aliases=aliases,
        compiler_params=_params(("arbitrary", "arbitrary"), vmem),
        name="matmul",
    )(*args)


def _mm_nt_body(w_ref, x_ref, o_ref):
    o_ref[...] = lax.dot_general(w_ref[...].astype(BF16), x_ref[...], _NT,
                                 preferred_element_type=F32).astype(o_ref.dtype)


def matmul_nt_tiled(w_t, x, *, bm, bn, out_dtype):
    n, k = w_t.shape
    m = x.shape[0]
    bm, bn = min(bm, m), min(bn, n)
    assert m % bm == 0 and n % bn == 0
    vmem = 3 * bn * k * 4 + 2 * bm * k * 2 + 2 * bn * bm * jnp.dtype(out_dtype).itemsize + bn * bm * 4
    return pl.pallas_call(
        _mm_nt_body,
        grid=(n // bn, m // bm),
        in_specs=[pl.BlockSpec((bn, k), lambda j, i: (j, 0)),
                  pl.BlockSpec((bm, k), lambda j, i: (i, 0))],
        out_specs=pl.BlockSpec((None, bn, bm), lambda j, i: (i, j, 0)),
        out_shape=jax.ShapeDtypeStruct((m // bm, n, bm), out_dtype),
        compiler_params=_params(("arbitrary", "arbitrary"), vmem),
        name="matmul_nt",
    )(w_t, x)


def _norm_body(x_ref, g_ref, o_ref):
    o_ref[...] = _rms(x_ref[...], g_ref[...]).astype(o_ref.dtype)


def norm_rows(x, g, *, tm=256):
    m, d = x.shape
    tm = min(tm, m)
    return pl.pallas_call(
        _norm_body,
        grid=(m // tm,),
        in_specs=[pl.BlockSpec((tm, d), lambda i: (i, 0)),
                  pl.BlockSpec((1, d), lambda i: (0, 0))],
        out_specs=pl.BlockSpec((tm, d), lambda i: (i, 0)),
        out_shape=jax.ShapeDtypeStruct((m, d), BF16),
        compiler_params=_params(("arbitrary",), 2 * tm * d * 6 + 3 * tm * d * 4),
        name="norm_rows",
    )(x, g.reshape(1, d))


def _resid_norm_body(h_ref, x_ref, ga_ref, gb_ref, xo_ref, no_ref, *rest, dils):
    xn = x_ref[...] + _rms(h_ref[...], ga_ref[...])
    xo_ref[...] = xn
    n = _rms(xn, gb_ref[...])
    no_ref[...] = n.astype(no_ref.dtype)
    if dils:
        n_scr = rest[-1]
        tm, dm = n.shape
        chunks = dm // V7X_LANES
        for c in range(chunks):
            n_scr[c] = n[:, c * V7X_LANES:(c + 1) * V7X_LANES]
        for d, ref in zip(dils, rest[:-1]):
            for r in range(d):
                for c in range(chunks):
                    lo = r * dm + c * V7X_LANES
                    ref[:, lo:lo + V7X_LANES] = n_scr[c, pl.ds(r, tm // d, stride=d), :].astype(
                        ref.dtype)


def _resid_body(h_ref, x_ref, ga_ref, xo_ref):
    xo_ref[...] = x_ref[...] + _rms(h_ref[...], ga_ref[...])


def resid_norm(h, x, g_a, g_b=None, *, tm=128, dils=()):
    m, d = x.shape
    if dils:
        tm = 16 * max(dils)
    tm = min(tm, m)
    row = pl.BlockSpec((tm, d), lambda i: (i, 0))
    gain = pl.BlockSpec((1, d), lambda i: (0, 0))
    if g_b is None:
        return pl.pallas_call(
            _resid_body,
            grid=(m // tm,),
            in_specs=[row, row, gain],
            out_specs=row,
            out_shape=jax.ShapeDtypeStruct((m, d), F32),
            compiler_params=_params(("arbitrary",), 2 * tm * d * 12 + 3 * tm * d * 4),
            name="resid",
        )(h, x, g_a.reshape(1, d))
    outs = pl.pallas_call(
        functools.partial(_resid_norm_body, dils=tuple(dils)),
        grid=(m // tm,),
        in_specs=[row, row, gain, gain],
        out_specs=[row, row] + [pl.BlockSpec((tm // dl, dl * d), lambda i: (i, 0)) for dl in dils],
        out_shape=[jax.ShapeDtypeStruct((m, d), F32), jax.ShapeDtypeStruct((m, d), BF16)]
        + [jax.ShapeDtypeStruct((m // dl, dl * d), BF16) for dl in dils],
        scratch_shapes=[pltpu.VMEM((d // V7X_LANES, tm, V7X_LANES), F32)] if dils else [],
        compiler_params=_params(("arbitrary",),
                                2 * tm * d * (14 + 2 * len(dils)) + 5 * tm * d * 4),
        name="resid_norm",
    )(h, x, g_a.reshape(1, d), g_b.reshape(1, d))
    return (outs[0], outs[1]) + ((tuple(outs[2:]),) if dils else ())


_HALO = 8


def _order_after(x, v):
    zero = (lax.bitcast_convert_type(v, jnp.uint32) >> 16) >> 16
    return pltpu.bitcast(pltpu.bitcast(x, jnp.uint32) | zero, BF16)


def _ffn_in_body(x_ref, wg_ref, wu_ref, cw_ref, cb_ref, wo_ref, o_ref, wo16_ref, wb_ref, g_ref,
                 u_ref, t_ref, *, bm, cm, row_tiles):
    s = pl.program_id(0)
    i = s % row_tiles
    bn = o_ref.shape[1]
    nc = bm // cm

    @pl.when(s == 0)
    def _():
        g_ref[...] = jnp.zeros(g_ref.shape, F32)
        u_ref[...] = jnp.zeros(u_ref.shape, F32)

    @pl.when(i == 0)
    def _():
        wb_ref[:, 0:bn] = wg_ref[...].astype(BF16)
        wb_ref[:, bn:2 * bn] = wu_ref[...].astype(BF16)

    wo16_ref[...] = wo_ref[...].astype(BF16)

    cw = cw_ref[...]
    cb = cb_ref[...]
    for c in range(nc):
        rows = slice(c * cm, (c + 1) * cm)
        lo = _HALO + c * cm
        g = g_ref[lo:lo + cm, :]
        g1 = g_ref[lo - 1:lo - 1 + cm, :]
        g2 = g_ref[lo - 2:lo - 2 + cm, :]
        gate = cw[0:1, :] * g2 + cw[1:2, :] * g1 + cw[2:3, :] * g + cb
        act = (gate / (1.0 + jnp.exp(-gate))) * u_ref[rows, :]
        o_ref[rows, :] = act.astype(o_ref.dtype)
        if c == 0:
            g_ref[0:_HALO, :] = jnp.where(i == 0, 0.0, g_ref[bm:bm + _HALO, :])
        else:
            g_ref[lo - _HALO:lo, :] = t_ref[c - 1]
        x0 = _order_after(x_ref[rows, 0:V7X_LANES], act[0:cm // 2, 0:V7X_LANES])
        x = jnp.concatenate([x0, x_ref[rows, V7X_LANES:]], axis=1)
        gu = jnp.dot(x, wb_ref[...], preferred_element_type=F32)
        u_ref[rows, :] = gu[:, bn:2 * bn]
        if c == nc - 1:
            g_ref[lo:lo + cm, :] = gu[:, 0:bn]
        else:
            g_ref[lo:lo + cm - _HALO, :] = gu[0:cm - _HALO, 0:bn]
            t_ref[c] = gu[cm - _HALO:cm, 0:bn]


def ffn_in(xn, w_in, layer, conv_w, conv_b, w_out, *, bm=1024, bn=V7X_MXU_DIM, cm=128):
    m, k = xn.shape
    d_ff = w_in.shape[2] // 2
    d_out = w_out.shape[2]
    bm = min(bm, m)
    bn = min(bn, d_ff)
    assert m % bm == 0 and d_ff % bn == 0
    nj, ni = d_ff // bn, m // bm
    tiles = nj * ni
    cm = min(cm, bm)
    assert bm % cm == 0 and cm % (2 * _HALO) == 0
    slab = d_ff // tiles
    assert slab * tiles == d_ff and slab % 16 == 0

    def mm(f):
        return lambda s: f(jnp.minimum(s, tiles - 1) // ni, jnp.minimum(s, tiles - 1) % ni)

    def ep(f):
        return lambda s: f(jnp.maximum(s - 1, 0) // ni, jnp.maximum(s - 1, 0) % ni)

    vmem = (2 * bm * k * 2 + 2 * 2 * k * bn * 4 + 2 * k * bn * 2 + 2 * bm * bn * 2
            + (2 * bm + _HALO) * bn * 4 + 6 * bm * bn * 4)
    return pl.pallas_call(
        functools.partial(_ffn_in_body, bm=bm, cm=cm, row_tiles=ni),
        grid=(tiles + 1,),
        in_specs=[pl.BlockSpec((bm, k), mm(lambda j, i: (i, 0))),
                  pl.BlockSpec((None, k, bn), mm(lambda j, i: (layer, 0, j))),
                  pl.BlockSpec((None, k, bn), mm(lambda j, i: (layer, 0, nj + j))),
                  pl.BlockSpec((CONV_W, bn), ep(lambda j, i: (0, j))),
                  pl.BlockSpec((1, bn), ep(lambda j, i: (0, j))),
                  pl.BlockSpec((None, slab, d_out),
                               lambda s: (layer, jnp.minimum(s, tiles - 1), 0))],
        out_specs=[pl.BlockSpec((bm, bn), ep(lambda j, i: (i, j))),
                   pl.BlockSpec((slab, d_out), lambda s: (jnp.minimum(s, tiles - 1), 0))],
        out_shape=[jax.ShapeDtypeStruct((m, d_ff), BF16),
                   jax.ShapeDtypeStruct((d_ff, d_out), BF16)],
        scratch_shapes=[pltpu.VMEM((k, 2 * bn), BF16), pltpu.VMEM((bm + _HALO, bn), F32),
                        pltpu.VMEM((bm, bn), F32),
                        pltpu.VMEM((max(bm // cm - 1, 1), _HALO, bn), F32)],
        compiler_params=_params(("arbitrary",), vmem + 12 * slab * d_out),
        name="ffn_in",
    )(xn, w_in, w_in, conv_w, conv_b.reshape(1, d_ff), w_out)


def ffn_out(act, w_out16, *, bm=1024, bn=2 * V7X_MXU_DIM):
    d_ff = act.shape[1]
    w = w_out16[None]
    if d_ff % (2 * V7X_LANES) != 0:
        return matmul(act, w, bm=bm, bn=bn, out_dtype=F32)
    half = d_ff // 2
    part = matmul(act, w, bm=bm, bn=bn, out_dtype=F32, k_slice=(0, half))
    return matmul(act, w, bm=bm, bn=bn, out_dtype=F32, k_slice=(1, half), add=part)


_SB_HEADS_PER_STEP = 4


def _sb_attn_body(q_ref, k_ref, v_ref, tri_ref, o_ref, acc_ref, c_ref, *, blk, scale):
    qi = pl.program_id(1)
    nh = _SB_HEADS_PER_STEP
    dh = SB_HEAD_DIM
    tri = tri_ref[...]
    acc_ref[...] = jnp.zeros(acc_ref.shape, F32)
    c_ref[...] = jnp.zeros(c_ref.shape, F32)
    key = lax.broadcasted_iota(jnp.int32, (blk, blk), 0)
    qry = lax.broadcasted_iota(jnp.int32, (blk, blk), 1)
    causal = key < qry

    def block(kb, diagonal):
        start = pl.multiple_of(kb * blk, blk)
        heads = range(nh)
        cols = [slice(hd * dh, (hd + 1) * dh) for hd in heads]
        z = [lax.dot_general(k_ref[pl.ds(start, blk), cols[hd]], q_ref[:, cols[hd]], _NT,
                             preferred_element_type=F32) * scale for hd in heads]
        vt = [v_ref[pl.ds(start, blk), cols[hd]].astype(F32).T.astype(BF16) for hd in heads]
        log_beta, log_1mb, hi, lo = [], [], [], []
        for hd in heads:
            sp = jnp.log(1.0 + jnp.exp(-jnp.abs(z[hd])))
            log_beta.append(jnp.minimum(z[hd], 0.0) - sp)
            l1 = jnp.minimum(-z[hd], 0.0) - sp
            if diagonal:
                l1 = jnp.where(causal, l1, 0.0)
            log_1mb.append(l1)
            hi.append(l1.astype(BF16))
            lo.append((l1 - hi[hd].astype(F32)).astype(BF16))
        tail = [jnp.dot(tri, hi[hd], preferred_element_type=F32)
                + jnp.dot(tri, lo[hd], preferred_element_type=F32) for hd in heads]
        c = [c_ref[hd] for hd in heads]
        a = []
        for hd in heads:
            w = jnp.exp(log_beta[hd] + tail[hd] + c[hd])
            if diagonal:
                w = jnp.where(causal, w, 0.0)
            a.append(w.astype(BF16))
        top = None
        for hd in heads:
            acc_ref[hd] += jnp.dot(vt[hd], a[hd], preferred_element_type=F32)
            c_new = c[hd] + tail[hd][0:1, :] + log_1mb[hd][0:1, :]
            c_ref[hd] = c_new
            top = jnp.max(c_new) if top is None else jnp.maximum(top, jnp.max(c_new))
        return top

    top = block(qi, True)
    lax.while_loop(lambda st: (st[0] >= 0) & (st[1] > EXP_ZERO_BELOW),
                   lambda st: (st[0] - 1, block(st[0], False)),
                   (qi - 1, top))
    for hd in range(nh):
        o_ref[:, hd * dh:(hd + 1) * dh] = acc_ref[hd].T.astype(o_ref.dtype)


def sb_attention(qkv, *, blk=256):
    s = qkv.shape[0]
    h, dh = SB_HEADS, SB_HEAD_DIM
    nh = _SB_HEADS_PER_STEP
    blk = min(blk, s)
    groups = h // nh
    tri = (np.arange(blk)[:, None] < np.arange(blk)[None, :]).astype(np.float32)
    vmem = (2 * 2 * s * nh * dh * 2 + 4 * blk * nh * dh * 2 + 2 * blk * blk * 2
            + nh * (dh + 8) * blk * 4 + nh * 12 * blk * blk * 4)
    return pl.pallas_call(
        functools.partial(_sb_attn_body, blk=blk, scale=dh ** -0.5),
        grid=(groups, s // blk),
        in_specs=[pl.BlockSpec((blk, nh * dh), lambda g, i: (i, g)),
                  pl.BlockSpec((s, nh * dh), lambda g, i: (0, groups + g)),
                  pl.BlockSpec((s, nh * dh), lambda g, i: (0, 2 * groups + g)),
                  pl.BlockSpec((blk, blk), lambda g, i: (0, 0))],
        out_specs=pl.BlockSpec((blk, nh * dh), lambda g, i: (i, g)),
        out_shape=jax.ShapeDtypeStruct((s, h * dh), BF16),
        scratch_shapes=[pltpu.VMEM((nh, dh, blk), F32), pltpu.VMEM((nh, 1, blk), F32)],
        compiler_params=_params(("arbitrary", "arbitrary"), vmem),
        name="sb_attention",
    )(qkv, qkv, qkv, jnp.asarray(tri, BF16))


def _mla_prep_body(y_ref, pos_ref, inv_ref, gq_ref, gkv_ref, cq_ref, ckv_ref, kr_ref, cs_ref):
    ql, kvl = MLA_Q_LORA, MLA_KV_LORA
    cq_ref[...] = _rms(y_ref[:, 0:ql], gq_ref[...]).astype(cq_ref.dtype)
    ckv_ref[...] = _rms(y_ref[:, ql:ql + kvl], gkv_ref[...]).astype(ckv_ref.dtype)
    ang = pos_ref[...] * inv_ref[...]
    lane = lax.broadcasted_iota(jnp.int32, ang.shape, 1)
    cs = jnp.where(lane < MLA_ROPE, jnp.cos(ang), jnp.sin(ang))
    cs_ref[...] = cs
    w = y_ref[:, ql + kvl:ql + kvl + V7X_LANES] * cs
    kr_ref[...] = (w + pltpu.roll(w, MLA_ROPE, axis=1)).astype(kr_ref.dtype)


def mla_prep(y, pos_col, g_q, g_kv, *, tm=256):
    m, width = y.shape
    tm = min(tm, m)
    half = MLA_ROPE // 2
    inv = ROPE_THETA ** (-jnp.arange(half, dtype=F32) / half)
    inv4 = jnp.tile(inv, 4).reshape(1, V7X_LANES)
    row = lambda w: pl.BlockSpec((tm, w), lambda i: (i, 0))
    const = lambda w: pl.BlockSpec((1, w), lambda i: (0, 0))
    return pl.pallas_call(
        _mla_prep_body,
        grid=(m // tm,),
        in_specs=[row(width), row(1), const(V7X_LANES), const(MLA_Q_LORA), const(MLA_KV_LORA)],
        out_specs=[row(MLA_Q_LORA), row(MLA_KV_LORA), row(V7X_LANES), row(V7X_LANES)],
        out_shape=[jax.ShapeDtypeStruct((m, MLA_Q_LORA), BF16),
                   jax.ShapeDtypeStruct((m, MLA_KV_LORA), BF16),
                   jax.ShapeDtypeStruct((m, V7X_LANES), BF16),
                   jax.ShapeDtypeStruct((m, V7X_LANES), F32)],
        compiler_params=_params(("arbitrary",), 6 * tm * width * 4),
        name="mla_prep",
    )(y, pos_col, inv4, g_q.reshape(1, -1), g_kv.reshape(1, -1))


_MLA_HEADS_PER_STEP = 4


def _mla_attn_body(q_ref, kn_ref, kr_ref, vt_ref, o_ref, s_ref, m_ref, l_ref, acc_ref, *, blk):
    qi = pl.program_id(1)
    nh = _MLA_HEADS_PER_STEP
    wq = MLA_NOPE + V7X_LANES
    m_ref[...] = jnp.full(m_ref.shape, MASKED, F32)
    l_ref[...] = jnp.zeros(l_ref.shape, F32)
    acc_ref[...] = jnp.zeros(acc_ref.shape, F32)
    key = lax.broadcasted_iota(jnp.int32, (blk, blk), 0)
    qry = lax.broadcasted_iota(jnp.int32, (blk, blk), 1)

    def scores(kb, slot):
        start = pl.multiple_of(kb * blk, blk)
        kr = kr_ref[pl.ds(start, blk), :]
        for hd in range(nh):
            kcat = jnp.concatenate(
                [kn_ref[pl.ds(start, blk), hd * MLA_NOPE:(hd + 1) * MLA_NOPE], kr], axis=1)
            s_ref[slot, hd] = lax.dot_general(kcat, q_ref[:, hd * wq:(hd + 1) * wq], _NT,
                                              preferred_element_type=F32)

    def reduce(kb, slot, diagonal):
        heads = range(nh)
        st = [s_ref[slot, hd] for hd in heads]
        if diagonal:
            st = [jnp.where(key <= qry, t, MASKED) for t in st]
        m_prev = [m_ref[hd] for hd in heads]
        m_new = [jnp.maximum(m_prev[hd], jnp.max(st[hd], axis=0, keepdims=True)) for hd in heads]
        p = [jnp.exp2(st[hd] - m_new[hd]) for hd in heads]
        pv = [jnp.dot(vt_ref[kb, hd * MLA_V:(hd + 1) * MLA_V, :], p[hd].astype(BF16),
                      preferred_element_type=F32) for hd in heads]
        for hd in heads:
            alpha = jnp.exp2(m_prev[hd] - m_new[hd])
            l_ref[hd] = alpha * l_ref[hd] + jnp.sum(p[hd], axis=0, keepdims=True)
            acc_ref[hd] = alpha * acc_ref[hd] + pv[hd]
            m_ref[hd] = m_new[hd]

    def off_diagonal_pair(t, carry):
        scores(2 * t + 1, 1)
        reduce(2 * t, 0, False)
        scores(2 * t + 2, 0)
        reduce(2 * t + 1, 1, False)
        return carry

    scores(0, 0)
    lax.fori_loop(0, qi // 2, off_diagonal_pair, 0)

    @pl.when(qi % 2 == 0)
    def _():
        reduce(qi, 0, True)

    @pl.when(qi % 2 == 1)
    def _():
        scores(qi, 1)
        reduce(qi - 1, 0, False)
        reduce(qi, 1, True)

    for hd in range(nh):
        o_ref[:, hd * MLA_V:(hd + 1) * MLA_V] = (acc_ref[hd] / l_ref[hd]).T.astype(o_ref.dtype)


def mla_attention(q, kn, kr, vt, *, blk):
    s = q.shape[0]
    h = MLA_HEADS
    nh = _MLA_HEADS_PER_STEP
    wq = MLA_NOPE + V7X_LANES
    assert vt.shape == (s // blk, h * MLA_V, blk)
    vmem = (2 * blk * nh * wq * 2 + 2 * s * nh * MLA_NOPE * 2 + 2 * s * V7X_LANES * 2
            + 2 * s * nh * MLA_V * 2 + 2 * blk * nh * MLA_V * 2 + nh * (MLA_V + 16) * blk * 4
            + nh * 8 * blk * blk * 4)
    return pl.pallas_call(
        functools.partial(_mla_attn_body, blk=blk),
        grid=(h // nh, s // blk),
        in_specs=[pl.BlockSpec((blk, nh * wq), lambda hp, i: (i, hp)),
                  pl.BlockSpec((s, nh * MLA_NOPE), lambda hp, i: (0, hp)),
                  pl.BlockSpec((s, V7X_LANES), lambda hp, i: (0, 0)),
                  pl.BlockSpec((s // blk, nh * MLA_V, blk), lambda hp, i: (0, hp, 0))],
        out_specs=pl.BlockSpec((blk, nh * MLA_V), lambda hp, i: (i, hp)),
        out_shape=jax.ShapeDtypeStruct((s, h * MLA_V), BF16),
        scratch_shapes=[pltpu.VMEM((2, nh, blk, blk), F32),
                        pltpu.VMEM((nh, 1, blk), F32), pltpu.VMEM((nh, 1, blk), F32),
                        pltpu.VMEM((nh, MLA_V, blk), F32)],
        compiler_params=_params(("arbitrary", "arbitrary"), vmem),
        name="mla_attention",
    )(q, kn, kr, vt)


def _band_attn_body(*refs, tile, hps, kv_shared, window, scale, has_sink, want_lse):
    refs = list(refs)
    slope_ref = refs.pop(0)
    sink_ref = refs.pop(0) if has_sink else None
    q_ref, k_ref, v_ref, pq_ref, pk_ref, o_ref = refs[:6]
    lse_ref = refs[6] if want_lse else None
    hbase = pl.program_id(1) * hps
    t = pl.program_id(2)
    nb = tile // BLK
    heads = range(hps)
    qi = lax.broadcasted_iota(jnp.int32, (BLK, 2 * BLK), 0)
    kj = lax.broadcasted_iota(jnp.int32, (BLK, 2 * BLK), 1)
    slopes = [slope_ref[hbase + h] for h in heads]
    sinks = [sink_ref[hbase + h] for h in heads] if has_sink else None
    for b in range(nb):
        gb = t * nb + b
        pb = jnp.maximum(gb - 1, 0)
        start = pl.multiple_of(pb * BLK, BLK)
        rows = slice(b * BLK, (b + 1) * BLK)
        pk = jnp.concatenate([pk_ref[pb], pk_ref[pb + 1]], axis=1)
        delta = (gb - pb) * BLK + qi - kj
        valid = (delta >= 0) & (delta <= window)
        dist = jnp.where(valid, pq_ref[rows, :] - pk, -MASKED)
        cols = [slice(h * BLK, (h + 1) * BLK) for h in heads]
        kv_cols = [slice(0, BLK)] * hps if kv_shared else cols
        sc = [lax.dot_general(q_ref[rows, cols[h]], k_ref[pl.ds(start, 2 * BLK), kv_cols[h]], _NT,
                              preferred_element_type=F32) * scale - slopes[h] * dist
              for h in heads]
        m = [jnp.max(sc[h], axis=1, keepdims=True) for h in heads]
        if has_sink:
            m = [jnp.maximum(m[h], sinks[h]) for h in heads]
        p = [jnp.exp(sc[h] - m[h]) for h in heads]
        den = [jnp.sum(p[h], axis=1, keepdims=True) for h in heads]
        if has_sink:
            den = [den[h] + jnp.exp(sinks[h] - m[h]) for h in heads]
        o = [jnp.dot((p[h] * (1.0 / den[h])).astype(BF16),
                     v_ref[pl.ds(start, 2 * BLK), kv_cols[h]], preferred_element_type=F32)
             for h in heads]
        for h in heads:
            o_ref[rows, cols[h]] = o[h].astype(o_ref.dtype)
            if want_lse:
                lse_ref[rows, cols[h]] = jnp.broadcast_to(m[h] + jnp.log(den[h]), (BLK, BLK))


def band_attention(arr, col_q, col_k, col_v, pos_col, pos_row, slopes, sinks, *,
                   n_seq, n_groups, hps, kv_shared, seq_len, window, out_dtype, want_lse):
    dh = BLK
    tile = min(seq_len, 4 * BLK)
    tiles = seq_len // tile
    has_sink = sinks is not None
    kvw = dh if kv_shared else hps * dh
    smem = pl.BlockSpec(memory_space=pltpu.SMEM)
    in_specs = [smem] + ([smem] if has_sink else []) + [
        pl.BlockSpec((tile, hps * dh), lambda r, g, t: (r * tiles + t, col_q(g))),
        pl.BlockSpec((seq_len, kvw), lambda r, g, t: (r, col_k(g))),
        pl.BlockSpec((seq_len, kvw), lambda r, g, t: (r, col_v(g))),
        pl.BlockSpec((None, tile, 1), lambda r, g, t: (r, t, 0)),
        pl.BlockSpec((None, seq_len // BLK, 1, BLK), lambda r, g, t: (r, 0, 0, 0)),
    ]
    args = [slopes] + ([sinks] if has_sink else []) + [arr, arr, arr, pos_col, pos_row]
    out_block = pl.BlockSpec((tile, hps * dh), lambda r, g, t: (t, r * n_groups + g))
    out_cols = n_seq * n_groups * hps * dh
    out_sds = jax.ShapeDtypeStruct((seq_len, out_cols), out_dtype)
    osize = jnp.dtype(out_dtype).itemsize
    vmem = (2 * tile * hps * dh * 2 + 2 * 2 * seq_len * kvw * 2 + 4 * tile * hps * dh * osize
            + 2 * tile * V7X_LANES * 4 + 2 * seq_len * 4 * 8 + hps * 8 * BLK * 2 * BLK * 4)
    return pl.pallas_call(
        functools.partial(_band_attn_body, tile=tile, hps=hps, kv_shared=kv_shared, window=window,
                          scale=dh ** -0.5, has_sink=has_sink, want_lse=want_lse),
        grid=(n_seq, n_groups, seq_len // tile),
        in_specs=in_specs,
        out_specs=[out_block, out_block] if want_lse else out_block,
        out_shape=[out_sds, jax.ShapeDtypeStruct((seq_len, out_cols), F32)] if want_lse else out_sds,
        compiler_params=_params(("arbitrary", "arbitrary", "arbitrary"), vmem),
        name="band_attention",
    )(*args)


def _alibi_slopes(n):
    return jnp.asarray(2.0 ** (-8.0 * np.arange(1, n + 1) / n), dtype=F32)


def _strided_positions(pos_f32, dil):
    s = pos_f32.shape[0]
    ps = pos_f32.reshape(s // dil, dil).T
    return ps.reshape(dil, s // dil, 1), ps.reshape(dil, s // dil // BLK, 1, BLK)


def _dil_merge_body(*refs, dils):
    n = len(dils)
    o_refs, l_refs, out_ref, buf = refs[:n], refs[n:2 * n], refs[2 * n], refs[2 * n + 1]
    tm, w = out_ref.shape

    def token_order(ref, d, c):
        if d == 1:
            return ref[:, c * V7X_LANES:(c + 1) * V7X_LANES]
        for r in range(d):
            lo = r * w + c * V7X_LANES
            buf[pl.ds(r, tm // d, stride=d), :] = ref[:, lo:lo + V7X_LANES]
        return buf[...]

    for c in range(w // V7X_LANES):
        lse = [token_order(l_refs[g], dils[g], c) for g in range(n)]
        m = functools.reduce(jnp.maximum, lse)
        e = [jnp.exp(l - m) for l in lse]
        inv = 1.0 / functools.reduce(lambda a, b: a + b, e)
        acc = None
        for g in range(n):
            term = (e[g] * inv) * token_order(o_refs[g], dils[g], c)
            acc = term if acc is None else acc + term
        out_ref[:, c * V7X_LANES:(c + 1) * V7X_LANES] = acc.astype(out_ref.dtype)


def dil_merge(outs, lses, dils, *, tm=256):
    m = outs[0].shape[0] * dils[0]
    w = outs[0].shape[1] // dils[0]
    tm = min(tm, m)
    specs = [pl.BlockSpec((tm // d, d * w), lambda i: (i, 0)) for d in dils]
    return pl.pallas_call(
        functools.partial(_dil_merge_body, dils=tuple(dils)),
        grid=(m // tm,),
        in_specs=specs + specs,
        out_specs=pl.BlockSpec((tm, w), lambda i: (i, 0)),
        out_shape=jax.ShapeDtypeStruct((m, w), BF16),
        scratch_shapes=[pltpu.VMEM((tm, V7X_LANES), F32)],
        compiler_params=_params(("arbitrary",), (4 * len(dils) + 4) * tm * w * 4),
        name="dil_merge",
    )(*outs, *lses)


def stick_breaking_mixer(hin, w_qkv, w_o, j):
    qkv = matmul(hin, w_qkv, bm=1024, bn=512, out_dtype=BF16, layer=j)
    o = sb_attention(qkv)
    return matmul(o, w_o, bm=1024, bn=512, out_dtype=F32, layer=j)


def mla_mixer(hin, pos_f32, w_dq, g_q, w_uq, w_dkv, g_kv, w_ukv, w_o, j):
    h, nope, rope = MLA_HEADS, MLA_NOPE, MLA_ROPE
    half = rope // 2
    w_dq, w_uq, w_dkv, g_q, g_kv = w_dq[j], w_uq[j], w_dkv[j], g_q[j], g_kv[j]
    kx1 = w_dkv[:, MLA_KV_LORA:MLA_KV_LORA + half]
    kx2 = w_dkv[:, MLA_KV_LORA + half:]
    w_down = jnp.concatenate(
        [w_dq, w_dkv[:, :MLA_KV_LORA], kx1, kx2, -kx2, kx1,
         jnp.zeros((w_dq.shape[0], V7X_LANES), w_dq.dtype)], axis=1)
    y = matmul(hin, w_down[None], bm=1024, bn=V7X_MXU_DIM, out_dtype=F32)
    cq, ckv, kr, cs = mla_prep(y, pos_f32.reshape(-1, 1), g_q, g_kv)
    wq = w_uq.reshape(MLA_Q_LORA, h, nope + rope)
    qx1, qx2 = wq[:, :, nope:nope + half], wq[:, :, nope + half:]
    wq = jnp.concatenate([wq[:, :, :nope], qx1, qx2, -qx2, qx1], axis=2)
    q_scale = (nope + rope) ** -0.5 * float(np.log2(np.e))
    q = matmul(cq, wq.reshape(1, MLA_Q_LORA, h * 2 * V7X_LANES), bm=2048, bn=1024,
               out_dtype=BF16, rope_cs=cs, out_scale=q_scale)
    wkv = w_ukv[j].reshape(MLA_KV_LORA, h, nope + MLA_V)
    w_k = wkv[:, :, :nope].reshape(1, MLA_KV_LORA, h * nope)
    w_vt = wkv[:, :, nope:].reshape(MLA_KV_LORA, h * MLA_V).T
    blk = min(512, hin.shape[0])
    kn = matmul(ckv, w_k, bm=2048, bn=1024, out_dtype=BF16)
    vt = matmul_nt_tiled(w_vt, ckv, bm=blk, bn=1024, out_dtype=BF16)
    o = mla_attention(q, kn, kr, vt, blk=blk)
    return matmul(o, w_o, bm=1024, bn=512, out_dtype=F32, layer=j)


def swa_mixer(hin, pos_f32, w_qkv, sinks, w_o, j):
    s = hin.shape[0]
    rep = SWA_HEADS // SWA_KV_HEADS
    sinks = sinks[j]
    qkv = matmul(hin, w_qkv, bm=1024, bn=512, out_dtype=BF16, layer=j)
    o = band_attention(
        qkv,
        lambda g: g,
        lambda g: SWA_HEADS + g,
        lambda g: SWA_HEADS + SWA_KV_HEADS + g,
        pos_f32.reshape(1, s, 1), pos_f32.reshape(1, s // BLK, 1, BLK),
        _alibi_slopes(SWA_HEADS), sinks,
        n_seq=1, n_groups=SWA_KV_HEADS, hps=rep, kv_shared=True, seq_len=s,
        window=SWA_WINDOW - 1, out_dtype=BF16, want_lse=False)
    return matmul(o, w_o, bm=1024, bn=512, out_dtype=F32, layer=j)


_DIL_VIEWS = tuple(sorted({dil for _, dil in DIL_PATTERNS if dil > 1}))


def dilated_mixer(hin, hin_views, pos_f32, w_qkv, w_o, j):
    s, d_model = hin.shape
    nh = DIL_HEADS
    n_groups = len(DIL_PATTERNS)
    gw = 3 * nh * DIL_HEAD_DIM
    slopes_all = _alibi_slopes(n_groups * nh)
    views = {1: hin, **dict(zip(_DIL_VIEWS, hin_views))}
    outs, lses = [], []
    for gi, (win, dil) in enumerate(DIL_PATTERNS):
        pos_col, pos_row = _strided_positions(pos_f32, dil)
        qkv = matmul(views[dil], w_qkv, bm=1024, bn=512, out_dtype=BF16,
                     layer=j, n_slice=(gi * gw, gw), row_groups=dil)
        hps = 4
        ng = nh // hps
        o, lse = band_attention(
            qkv,
            lambda g: g,
            lambda g, ng=ng: ng + g,
            lambda g, ng=ng: 2 * ng + g,
            pos_col, pos_row, slopes_all[gi * nh:(gi + 1) * nh], None,
            n_seq=dil, n_groups=ng, hps=hps, kv_shared=False, seq_len=s // dil,
            window=win // dil, out_dtype=F32, want_lse=True)
        outs.append(o)
        lses.append(lse)
    o = dil_merge(outs, lses, [dil for _, dil in DIL_PATTERNS])
    return matmul(o, w_o, bm=1024, bn=512, out_dtype=F32, layer=j)


def kernel(x, positions, norm_g, a_w_qkv, a_w_o, b_w_dq, b_g_q, b_w_uq, b_w_dkv, b_g_kv,
           b_w_ukv, b_w_o, c_w_qkv, c_sinks, c_w_o, d_w_qkv, d_w_o, ffn_w_in, ffn_conv_w,
           ffn_conv_b, ffn_w_out):
    batch, s, d = x.shape
    depth = norm_g.shape[0]
    n_mixers = 4
    outs = []
    for bi in range(batch):
        xb = x[bi]
        pos_f32 = positions[bi].astype(F32)
        hin = norm_rows(xb, norm_g[0, 0])
        hin_views = ()
        for i in range(depth):
            mixer, j = i % n_mixers, i // n_mixers
            if mixer == 0:
                h = stick_breaking_mixer(hin, a_w_qkv, a_w_o, j)
            elif mixer == 1:
                h = mla_mixer(hin, pos_f32, b_w_dq, b_g_q, b_w_uq, b_w_dkv, b_g_kv, b_w_ukv,
                              b_w_o, j)
            elif mixer == 2:
                h = swa_mixer(hin, pos_f32, c_w_qkv, c_sinks, c_w_o, j)
            else:
                h = dilated_mixer(hin, hin_views, pos_f32, d_w_qkv, d_w_o, j)
            xb, xn = resid_norm(h, xb, norm_g[i, 1], norm_g[i, 2])
            act, w_out16 = ffn_in(xn, ffn_w_in, i, ffn_conv_w[i], ffn_conv_b[i], ffn_w_out)
            h = ffn_out(act, w_out16)
            if i + 1 == depth:
                xb = resid_norm(h, xb, norm_g[i, 3])
            elif (i + 1) % n_mixers == 3:
                xb, hin, hin_views = resid_norm(h, xb, norm_g[i, 3], norm_g[i + 1, 0],
                                                dils=_DIL_VIEWS)
            else:
                xb, hin = resid_norm(h, xb, norm_g[i, 3], norm_g[i + 1, 0])
        outs.append(xb)
    return jnp.stack(outs, axis=0)
```

```python
import functools

import numpy as np
import jax
import jax.numpy as jnp
from jax import lax
from jax.experimental import pallas as pl
from jax.experimental.pallas import tpu as pltpu

F32 = jnp.float32
BF16 = jnp.bfloat16

V7X_VMEM_BYTES = 64 * 2**20
V7X_LANES = 128
V7X_MXU_DIM = 256

EPS = 1e-6
BLK = 128
SB_HEADS, SB_HEAD_DIM = 32, 128
MLA_HEADS, MLA_NOPE, MLA_ROPE, MLA_V = 32, 128, 64, 128
MLA_Q_LORA, MLA_KV_LORA = 1024, 512
ROPE_THETA = 10000.0
SWA_HEADS, SWA_KV_HEADS, SWA_HEAD_DIM, SWA_WINDOW = 32, 8, 128, 128
DIL_PATTERNS = ((128, 1), (512, 4), (2048, 16))
DIL_HEADS, DIL_HEAD_DIM = 16, 128
CONV_W = 3

MASKED = -1e30
EXP_ZERO_BELOW = -105.0

_NT = (((1,), (1,)), ((), ()))


_VMEM_SPILL_ALLOWANCE = 8 * 2**20
_VMEM_RESERVE = 6 * 2**20


def _params(semantics, vmem_bytes):
    limit = min(int(vmem_bytes) + _VMEM_SPILL_ALLOWANCE, V7X_VMEM_BYTES - _VMEM_RESERVE)
    return pltpu.CompilerParams(dimension_semantics=semantics, vmem_limit_bytes=limit)


def _rms(x, g):
    return (x * lax.rsqrt(jnp.mean(x * x, axis=-1, keepdims=True) + EPS)) * g


def _mm_body(*refs, mode, out_scale, sub):
    x_ref, w_ref, o_ref = refs[0], refs[1], refs[-1]
    if sub > 1:
        wb = w_ref[...].astype(BF16)
        rows, k = x_ref.shape[0], x_ref.shape[1] // sub
        for r in range(sub):
            o_ref[r * rows:(r + 1) * rows, :] = jnp.dot(
                x_ref[:, r * k:(r + 1) * k], wb, preferred_element_type=F32).astype(o_ref.dtype)
        return
    acc = jnp.dot(x_ref[...], w_ref[...].astype(BF16), preferred_element_type=F32)
    if mode == "plain":
        o_ref[...] = acc.astype(o_ref.dtype)
    elif mode == "add":
        o_ref[...] = (refs[2][...] + acc).astype(o_ref.dtype)
    else:
        cs = refs[2][...] * out_scale
        for c in range(acc.shape[1] // (2 * V7X_LANES)):
            lo = 2 * c * V7X_LANES
            o_ref[:, lo:lo + V7X_LANES] = (acc[:, lo:lo + V7X_LANES] * out_scale
                                           ).astype(o_ref.dtype)
            o_ref[:, lo + V7X_LANES:lo + 2 * V7X_LANES] = (
                acc[:, lo + V7X_LANES:lo + 2 * V7X_LANES] * cs).astype(o_ref.dtype)


def matmul(x, w, *, bm, bn, out_dtype, layer=0, k_slice=None, n_slice=None, row_groups=1,
           add=None, rope_cs=None, out_scale=1.0):
    n0, n = (0, w.shape[2]) if n_slice is None else n_slice
    if row_groups == 1:
        m, kx = x.shape
    else:
        assert k_slice is None
        m, kx = x.shape[0] * row_groups, x.shape[1] // row_groups
    kb, tk = (0, kx) if k_slice is None else k_slice
    group_rows = m // row_groups
    bm, bn = min(bm, m), min(bn, n)
    sub = max(bm // group_rows, 1)
    assert n % bn == 0 and n0 % bn == 0 and m % bm == 0
    assert (group_rows % bm == 0) if sub == 1 else (bm == sub * group_rows and add is None
                                                    and rope_cs is None)
    per_group = max(group_rows // bm, 1)
    in_specs = [pl.BlockSpec((bm // sub, sub * tk),
                             lambda i, j: (i % per_group, i // per_group + kb)),
                pl.BlockSpec((None, tk, bn), lambda i, j: (layer, kb, n0 // bn + j))]
    args = [x, w]
    mode = "plain"
    aliases = {}
    extra = 0
    if add is not None:
        in_specs.append(pl.BlockSpec((bm, bn), lambda i, j: (i, j)))
        args.append(add)
        mode = "add"
        aliases = {2: 0}
        extra = 2 * bm * bn * add.dtype.itemsize
    if rope_cs is not None:
        in_specs.append(pl.BlockSpec((bm, V7X_LANES), lambda i, j: (i, 0)))
        args.append(rope_cs)
        mode = "rope"
    osize = jnp.dtype(out_dtype).itemsize
    vmem = (2 * bm * tk * 2 + 2 * tk * bn * w.dtype.itemsize + tk * bn * 2
            + 2 * bm * bn * osize + bm * bn * 4 + extra)
    return pl.pallas_call(
        functools.partial(_mm_body, mode=mode, out_scale=out_scale, sub=sub),
        grid=(m // bm, n // bn),
        in_specs=in_specs,
        out_specs=pl.BlockSpec((bm, bn), lambda i, j: (i, j)),
        out_shape=jax.ShapeDtypeStruct((m, n), out_dtype),
        input_output_aliases=aliases,
        compiler_params=_params(("arbitrary", "arbitrary"), vmem),
        name="matmul",
    )(*args)


def _mm_nt_body(w_ref, x_ref, o_ref):
    o_ref[...] = lax.dot_general(w_ref[...].astype(BF16), x_ref[...], _NT,
                                 preferred_element_type=F32).astype(o_ref.dtype)


def matmul_nt_tiled(w_t, x, *, bm, bn, out_dtype):
    n, k = w_t.shape
    m = x.shape[0]
    bm, bn = min(bm, m), min(bn, n)
    assert m % bm == 0 and n % bn == 0
    vmem = 3 * bn * k * 4 + 2 * bm * k * 2 + 2 * bn * bm * jnp.dtype(out_dtype).itemsize + bn * bm * 4
    return pl.pallas_call(
        _mm_nt_body,
        grid=(n // bn, m // bm),
        in_specs=[pl.BlockSpec((bn, k), lambda j, i: (j, 0)),
                  pl.BlockSpec((bm, k), lambda j, i: (i, 0))],
        out_specs=pl.BlockSpec((None, bn, bm), lambda j, i: (i, j, 0)),
        out_shape=jax.ShapeDtypeStruct((m // bm, n, bm), out_dtype),
        compiler_params=_params(("arbitrary", "arbitrary"), vmem),
        name="matmul_nt",
    )(w_t, x)


def _norm_body(x_ref, g_ref, o_ref):
    o_ref[...] = _rms(x_ref[...], g_ref[...]).astype(o_ref.dtype)


def norm_rows(x, g, *, tm=256):
    m, d = x.shape
    tm = min(tm, m)
    return pl.pallas_call(
        _norm_body,
        grid=(m // tm,),
        in_specs=[pl.BlockSpec((tm, d), lambda i: (i, 0)),
                  pl.BlockSpec((1, d), lambda i: (0, 0))],
        out_specs=pl.BlockSpec((tm, d), lambda i: (i, 0)),
        out_shape=jax.ShapeDtypeStruct((m, d), BF16),
        compiler_params=_params(("arbitrary",), 2 * tm * d * 6 + 3 * tm * d * 4),
        name="norm_rows",
    )(x, g.reshape(1, d))


def _resid_norm_body(h_ref, x_ref, ga_ref, gb_ref, xo_ref, no_ref, *rest, dils):
    xn = x_ref[...] + _rms(h_ref[...], ga_ref[...])
    xo_ref[...] = xn
    n = _rms(xn, gb_ref[...])
    no_ref[...] = n.astype(no_ref.dtype)
    if dils:
        n_scr = rest[-1]
        tm, dm = n.shape
        chunks = dm // V7X_LANES
        for c in range(chunks):
            n_scr[c] = n[:, c * V7X_LANES:(c + 1) * V7X_LANES]
        for d, ref in zip(dils, rest[:-1]):
            for r in range(d):
                for c in range(chunks):
                    lo = r * dm + c * V7X_LANES
                    ref[:, lo:lo + V7X_LANES] = n_scr[c, pl.ds(r, tm // d, stride=d), :].astype(
                        ref.dtype)


def _resid_body(h_ref, x_ref, ga_ref, xo_ref):
    xo_ref[...] = x_ref[...] + _rms(h_ref[...], ga_ref[...])


def resid_norm(h, x, g_a, g_b=None, *, tm=256, dils=()):
    m, d = x.shape
    if dils:
        tm = 16 * max(dils)
    tm = min(tm, m)
    row = pl.BlockSpec((tm, d), lambda i: (i, 0))
    gain = pl.BlockSpec((1, d), lambda i: (0, 0))
    if g_b is None:
        return pl.pallas_call(
            _resid_body,
            grid=(m // tm,),
            in_specs=[row, row, gain],
            out_specs=row,
            out_shape=jax.ShapeDtypeStruct((m, d), F32),
            compiler_params=_params(("arbitrary",), 2 * tm * d * 12 + 3 * tm * d * 4),
            name="resid",
        )(h, x, g_a.reshape(1, d))
    outs = pl.pallas_call(
        functools.partial(_resid_norm_body, dils=tuple(dils)),
        grid=(m // tm,),
        in_specs=[row, row, gain, gain],
        out_specs=[row, row] + [pl.BlockSpec((tm // dl, dl * d), lambda i: (i, 0)) for dl in dils],
        out_shape=[jax.ShapeDtypeStruct((m, d), F32), jax.ShapeDtypeStruct((m, d), BF16)]
        + [jax.ShapeDtypeStruct((m // dl, dl * d), BF16) for dl in dils],
        scratch_shapes=[pltpu.VMEM((d // V7X_LANES, tm, V7X_LANES), F32)] if dils else [],
        compiler_params=_params(("arbitrary",),
                                2 * tm * d * (14 + 2 * len(dils)) + 5 * tm * d * 4),
        name="resid_norm",
    )(h, x, g_a.reshape(1, d), g_b.reshape(1, d))
    return (outs[0], outs[1]) + ((tuple(outs[2:]),) if dils else ())


_HALO = 8


def _order_after(x, v):
    zero = (lax.bitcast_convert_type(v, jnp.uint32) >> 16) >> 16
    return pltpu.bitcast(pltpu.bitcast(x, jnp.uint32) | zero, BF16)


def _ffn_in_body(x_ref, wg_ref, wu_ref, cw_ref, cb_ref, wo_ref, o_ref, wo16_ref, wb_ref, g_ref,
                 u_ref, t_ref, *, bm, cm, row_tiles):
    s = pl.program_id(0)
    i = s % row_tiles
    bn = o_ref.shape[1]
    nc = bm // cm

    @pl.when(s == 0)
    def _():
        g_ref[...] = jnp.zeros(g_ref.shape, F32)
        u_ref[...] = jnp.zeros(u_ref.shape, F32)

    @pl.when(i == 0)
    def _():
        wb_ref[:, 0:bn] = wg_ref[...].astype(BF16)
        wb_ref[:, bn:2 * bn] = wu_ref[...].astype(BF16)

    wo16_ref[...] = wo_ref[...].astype(BF16)

    cw = cw_ref[...]
    cb = cb_ref[...]
    for c in range(nc):
        rows = slice(c * cm, (c + 1) * cm)
        lo = _HALO + c * cm
        g = g_ref[lo:lo + cm, :]
        g1 = g_ref[lo - 1:lo - 1 + cm, :]
        g2 = g_ref[lo - 2:lo - 2 + cm, :]
        gate = cw[0:1, :] * g2 + cw[1:2, :] * g1 + cw[2:3, :] * g + cb
        act = (gate / (1.0 + jnp.exp(-gate))) * u_ref[rows, :]
        o_ref[rows, :] = act.astype(o_ref.dtype)
        if c == 0:
            g_ref[0:_HALO, :] = jnp.where(i == 0, 0.0, g_ref[bm:bm + _HALO, :])
        else:
            g_ref[lo - _HALO:lo, :] = t_ref[c - 1]
        x0 = _order_after(x_ref[rows, 0:V7X_LANES], act[0:cm // 2, 0:V7X_LANES])
        x = jnp.concatenate([x0, x_ref[rows, V7X_LANES:]], axis=1)
        gu = jnp.dot(x, wb_ref[...], preferred_element_type=F32)
        u_ref[rows, :] = gu[:, bn:2 * bn]
        if c == nc - 1:
            g_ref[lo:lo + cm, :] = gu[:, 0:bn]
        else:
            g_ref[lo:lo + cm - _HALO, :] = gu[0:cm - _HALO, 0:bn]
            t_ref[c] = gu[cm - _HALO:cm, 0:bn]


def ffn_in(xn, w_in, layer, conv_w, conv_b, w_out, *, bm=1024, bn=V7X_MXU_DIM, cm=128):
    m, k = xn.shape
    d_ff = w_in.shape[2] // 2
    d_out = w_out.shape[2]
    bm = min(bm, m)
    bn = min(bn, d_ff)
    assert m % bm == 0 and d_ff % bn == 0
    nj, ni = d_ff // bn, m // bm
    tiles = nj * ni
    cm = min(cm, bm)
    assert bm % cm == 0 and cm % (2 * _HALO) == 0
    slab = d_ff // tiles
    assert slab * tiles == d_ff and slab % 16 == 0

    def mm(f):
        return lambda s: f(jnp.minimum(s, tiles - 1) // ni, jnp.minimum(s, tiles - 1) % ni)

    def ep(f):
        return lambda s: f(jnp.maximum(s - 1, 0) // ni, jnp.maximum(s - 1, 0) % ni)

    vmem = (2 * bm * k * 2 + 2 * 2 * k * bn * 4 + 2 * k * bn * 2 + 2 * bm * bn * 2
            + (2 * bm + _HALO) * bn * 4 + 6 * bm * bn * 4)
    return pl.pallas_call(
        functools.partial(_ffn_in_body, bm=bm, cm=cm, row_tiles=ni),
        grid=(tiles + 1,),
        in_specs=[pl.BlockSpec((bm, k), mm(lambda j, i: (i, 0))),
                  pl.BlockSpec((None, k, bn), mm(lambda j, i: (layer, 0, j))),
                  pl.BlockSpec((None, k, bn), mm(lambda j, i: (layer, 0, nj + j))),
                  pl.BlockSpec((CONV_W, bn), ep(lambda j, i: (0, j))),
                  pl.BlockSpec((1, bn), ep(lambda j, i: (0, j))),
                  pl.BlockSpec((None, slab, d_out),
                               lambda s: (layer, jnp.minimum(s, tiles - 1), 0))],
        out_specs=[pl.BlockSpec((bm, bn), ep(lambda j, i: (i, j))),
                   pl.BlockSpec((slab, d_out), lambda s: (jnp.minimum(s, tiles - 1), 0))],
        out_shape=[jax.ShapeDtypeStruct((m, d_ff), BF16),
                   jax.ShapeDtypeStruct((d_ff, d_out), BF16)],
        scratch_shapes=[pltpu.VMEM((k, 2 * bn), BF16), pltpu.VMEM((bm + _HALO, bn), F32),
                        pltpu.VMEM((bm, bn), F32),
                        pltpu.VMEM((max(bm // cm - 1, 1), _HALO, bn), F32)],
        compiler_params=_params(("arbitrary",), vmem + 12 * slab * d_out),
        name="ffn_in",
    )(xn, w_in, w_in, conv_w, conv_b.reshape(1, d_ff), w_out)


def ffn_out(act, w_out16, *, bm=1024, bn=2 * V7X_MXU_DIM):
    d_ff = act.shape[1]
    w = w_out16[None]
    if d_ff % (2 * V7X_LANES) != 0:
        return matmul(act, w, bm=bm, bn=bn, out_dtype=F32)
    half = d_ff // 2
    part = matmul(act, w, bm=bm, bn=bn, out_dtype=F32, k_slice=(0, half))
    return matmul(act, w, bm=bm, bn=bn, out_dtype=F32, k_slice=(1, half), add=part)


_SB_HEADS_PER_STEP = 4


def _sb_attn_body(q_ref, k_ref, v_ref, tri_ref, o_ref, acc_ref, c_ref, *, blk, scale):
    qi = pl.program_id(1)
    nh = _SB_HEADS_PER_STEP
    dh = SB_HEAD_DIM
    tri = tri_ref[...]
    acc_ref[...] = jnp.zeros(acc_ref.shape, F32)
    c_ref[...] = jnp.zeros(c_ref.shape, F32)
    key = lax.broadcasted_iota(jnp.int32, (blk, blk), 0)
    qry = lax.broadcasted_iota(jnp.int32, (blk, blk), 1)
    causal = key < qry

    def block(kb, diagonal):
        start = pl.multiple_of(kb * blk, blk)
        heads = range(nh)
        cols = [slice(hd * dh, (hd + 1) * dh) for hd in heads]
        z = [lax.dot_general(k_ref[pl.ds(start, blk), cols[hd]], q_ref[:, cols[hd]], _NT,
                             preferred_element_type=F32) * scale for hd in heads]
        vt = [v_ref[pl.ds(start, blk), cols[hd]].astype(F32).T.astype(BF16) for hd in heads]
        log_beta, log_1mb, hi, lo = [], [], [], []
        for hd in heads:
            sp = jnp.log(1.0 + jnp.exp(-jnp.abs(z[hd])))
            log_beta.append(jnp.minimum(z[hd], 0.0) - sp)
            l1 = jnp.minimum(-z[hd], 0.0) - sp
            if diagonal:
                l1 = jnp.where(causal, l1, 0.0)
            log_1mb.append(l1)
            hi.append(l1.astype(BF16))
            lo.append((l1 - hi[hd].astype(F32)).astype(BF16))
        tail = [jnp.dot(tri, hi[hd], preferred_element_type=F32)
                + jnp.dot(tri, lo[hd], preferred_element_type=F32) for hd in heads]
        c = [c_ref[hd] for hd in heads]
        a = []
        for hd in heads:
            w = jnp.exp(log_beta[hd] + tail[hd] + c[hd])
            if diagonal:
                w = jnp.where(causal, w, 0.0)
            a.append(w.astype(BF16))
        top = None
        for hd in heads:
            acc_ref[hd] += jnp.dot(vt[hd], a[hd], preferred_element_type=F32)
            c_new = c[hd] + tail[hd][0:1, :] + log_1mb[hd][0:1, :]
            c_ref[hd] = c_new
            top = jnp.max(c_new) if top is None else jnp.maximum(top, jnp.max(c_new))
        return top

    top = block(qi, True)
    lax.while_loop(lambda st: (st[0] >= 0) & (st[1] > EXP_ZERO_BELOW),
                   lambda st: (st[0] - 1, block(st[0], False)),
                   (qi - 1, top))
    for hd in range(nh):
        o_ref[:, hd * dh:(hd + 1) * dh] = acc_ref[hd].T.astype(o_ref.dtype)


def sb_attention(qkv, *, blk=256):
    s = qkv.shape[0]
    h, dh = SB_HEADS, SB_HEAD_DIM
    nh = _SB_HEADS_PER_STEP
    blk = min(blk, s)
    groups = h // nh
    tri = (np.arange(blk)[:, None] < np.arange(blk)[None, :]).astype(np.float32)
    vmem = (2 * 2 * s * nh * dh * 2 + 4 * blk * nh * dh * 2 + 2 * blk * blk * 2
            + nh * (dh + 8) * blk * 4 + nh * 12 * blk * blk * 4)
    return pl.pallas_call(
        functools.partial(_sb_attn_body, blk=blk, scale=dh ** -0.5),
        grid=(groups, s // blk),
        in_specs=[pl.BlockSpec((blk, nh * dh), lambda g, i: (i, g)),
                  pl.BlockSpec((s, nh * dh), lambda g, i: (0, groups + g)),
                  pl.BlockSpec((s, nh * dh), lambda g, i: (0, 2 * groups + g)),
                  pl.BlockSpec((blk, blk), lambda g, i: (0, 0))],
        out_specs=pl.BlockSpec((blk, nh * dh), lambda g, i: (i, g)),
        out_shape=jax.ShapeDtypeStruct((s, h * dh), BF16),
        scratch_shapes=[pltpu.VMEM((nh, dh, blk), F32), pltpu.VMEM((nh, 1, blk), F32)],
        compiler_params=_params(("arbitrary", "arbitrary"), vmem),
        name="sb_attention",
    )(qkv, qkv, qkv, jnp.asarray(tri, BF16))


def _mla_prep_body(y_ref, pos_ref, inv_ref, gq_ref, gkv_ref, cq_ref, ckv_ref, kr_ref, cs_ref):
    ql, kvl = MLA_Q_LORA, MLA_KV_LORA
    cq_ref[...] = _rms(y_ref[:, 0:ql], gq_ref[...]).astype(cq_ref.dtype)
    ckv_ref[...] = _rms(y_ref[:, ql:ql + kvl], gkv_ref[...]).astype(ckv_ref.dtype)
    ang = pos_ref[...] * inv_ref[...]
    lane = lax.broadcasted_iota(jnp.int32, ang.shape, 1)
    cs = jnp.where(lane < MLA_ROPE, jnp.cos(ang), jnp.sin(ang))
    cs_ref[...] = cs
    w = y_ref[:, ql + kvl:ql + kvl + V7X_LANES] * cs
    kr_ref[...] = (w + pltpu.roll(w, MLA_ROPE, axis=1)).astype(kr_ref.dtype)


def mla_prep(y, pos_col, g_q, g_kv, *, tm=256):
    m, width = y.shape
    tm = min(tm, m)
    half = MLA_ROPE // 2
    inv = ROPE_THETA ** (-jnp.arange(half, dtype=F32) / half)
    inv4 = jnp.tile(inv, 4).reshape(1, V7X_LANES)
    row = lambda w: pl.BlockSpec((tm, w), lambda i: (i, 0))
    const = lambda w: pl.BlockSpec((1, w), lambda i: (0, 0))
    return pl.pallas_call(
        _mla_prep_body,
        grid=(m // tm,),
        in_specs=[row(width), row(1), const(V7X_LANES), const(MLA_Q_LORA), const(MLA_KV_LORA)],
        out_specs=[row(MLA_Q_LORA), row(MLA_KV_LORA), row(V7X_LANES), row(V7X_LANES)],
        out_shape=[jax.ShapeDtypeStruct((m, MLA_Q_LORA), BF16),
                   jax.ShapeDtypeStruct((m, MLA_KV_LORA), BF16),
                   jax.ShapeDtypeStruct((m, V7X_LANES), BF16),
                   jax.ShapeDtypeStruct((m, V7X_LANES), F32)],
        compiler_params=_params(("arbitrary",), 6 * tm * width * 4),
        name="mla_prep",
    )(y, pos_col, inv4, g_q.reshape(1, -1), g_kv.reshape(1, -1))


_MLA_HEADS_PER_STEP = 4


def _mla_attn_body(q_ref, kn_ref, kr_ref, vt_ref, o_ref, s_ref, m_ref, l_ref, acc_ref, *, blk):
    qi = pl.program_id(1)
    nh = _MLA_HEADS_PER_STEP
    wq = MLA_NOPE + V7X_LANES
    m_ref[...] = jnp.full(m_ref.shape, MASKED, F32)
    l_ref[...] = jnp.zeros(l_ref.shape, F32)
    acc_ref[...] = jnp.zeros(acc_ref.shape, F32)
    key = lax.broadcasted_iota(jnp.int32, (blk, blk), 0)
    qry = lax.broadcasted_iota(jnp.int32, (blk, blk), 1)

    def scores(kb, slot):
        start = pl.multiple_of(kb * blk, blk)
        kr = kr_ref[pl.ds(start, blk), :]
        for hd in range(nh):
            kcat = jnp.concatenate(
                [kn_ref[pl.ds(start, blk), hd * MLA_NOPE:(hd + 1) * MLA_NOPE], kr], axis=1)
            s_ref[slot, hd] = lax.dot_general(kcat, q_ref[:, hd * wq:(hd + 1) * wq], _NT,
                                              preferred_element_type=F32)

    def reduce(kb, slot, diagonal):
        heads = range(nh)
        st = [s_ref[slot, hd] for hd in heads]
        if diagonal:
            st = [jnp.where(key <= qry, t, MASKED) for t in st]
        m_prev = [m_ref[hd] for hd in heads]
        m_new = [jnp.maximum(m_prev[hd], jnp.max(st[hd], axis=0, keepdims=True)) for hd in heads]
        p = [jnp.exp2(st[hd] - m_new[hd]) for hd in heads]
        pv = [jnp.dot(vt_ref[kb, hd * MLA_V:(hd + 1) * MLA_V, :], p[hd].astype(BF16),
                      preferred_element_type=F32) for hd in heads]
        for hd in heads:
            alpha = jnp.exp2(m_prev[hd] - m_new[hd])
            l_ref[hd] = alpha * l_ref[hd] + jnp.sum(p[hd], axis=0, keepdims=True)
            acc_ref[hd] = alpha * acc_ref[hd] + pv[hd]
            m_ref[hd] = m_new[hd]

    def off_diagonal_pair(t, carry):
        scores(2 * t + 1, 1)
        reduce(2 * t, 0, False)
        scores(2 * t + 2, 0)
        reduce(2 * t + 1, 1, False)
        return carry

    scores(0, 0)
    lax.fori_loop(0, qi // 2, off_diagonal_pair, 0)

    @pl.when(qi % 2 == 0)
    def _():
        reduce(qi, 0, True)

    @pl.when(qi % 2 == 1)
    def _():
        scores(qi, 1)
        reduce(qi - 1, 0, False)
        reduce(qi, 1, True)

    for hd in range(nh):
        o_ref[:, hd * MLA_V:(hd + 1) * MLA_V] = (acc_ref[hd] / l_ref[hd]).T.astype(o_ref.dtype)


def mla_attention(q, kn, kr, vt, *, blk):
    s = q.shape[0]
    h = MLA_HEADS
    nh = _MLA_HEADS_PER_STEP
    wq = MLA_NOPE + V7X_LANES
    assert vt.shape == (s // blk, h * MLA_V, blk)
    vmem = (2 * blk * nh * wq * 2 + 2 * s * nh * MLA_NOPE * 2 + 2 * s * V7X_LANES * 2
            + 2 * s * nh * MLA_V * 2 + 2 * blk * nh * MLA_V * 2 + nh * (MLA_V + 16) * blk * 4
            + nh * 8 * blk * blk * 4)
    return pl.pallas_call(
        functools.partial(_mla_attn_body, blk=blk),
        grid=(h // nh, s // blk),
        in_specs=[pl.BlockSpec((blk, nh * wq), lambda hp, i: (i, hp)),
                  pl.BlockSpec((s, nh * MLA_NOPE), lambda hp, i: (0, hp)),
                  pl.BlockSpec((s, V7X_LANES), lambda hp, i: (0, 0)),
                  pl.BlockSpec((s // blk, nh * MLA_V, blk), lambda hp, i: (0, hp, 0))],
        out_specs=pl.BlockSpec((blk, nh * MLA_V), lambda hp, i: (i, hp)),
        out_shape=jax.ShapeDtypeStruct((s, h * MLA_V), BF16),
        scratch_shapes=[pltpu.VMEM((2, nh, blk, blk), F32),
                        pltpu.VMEM((nh, 1, blk), F32), pltpu.VMEM((nh, 1, blk), F32),
                        pltpu.VMEM((nh, MLA_V, blk), F32)],
        compiler_params=_params(("arbitrary", "arbitrary"), vmem),
        name="mla_attention",
    )(q, kn, kr, vt)


def _band_attn_body(*refs, tile, hps, kv_shared, window, scale, has_sink, want_lse):
    refs = list(refs)
    slope_ref = refs.pop(0)
    sink_ref = refs.pop(0) if has_sink else None
    q_ref, k_ref, v_ref, pq_ref, pk_ref, o_ref = refs[:6]
    lse_ref = refs[6] if want_lse else None
    hbase = pl.program_id(1) * hps
    t = pl.program_id(2)
    nb = tile // BLK
    heads = range(hps)
    qi = lax.broadcasted_iota(jnp.int32, (BLK, 2 * BLK), 0)
    kj = lax.broadcasted_iota(jnp.int32, (BLK, 2 * BLK), 1)
    slopes = [slope_ref[hbase + h] for h in heads]
    sinks = [sink_ref[hbase + h] for h in heads] if has_sink else None
    for b in range(nb):
        gb = t * nb + b
        pb = jnp.maximum(gb - 1, 0)
        start = pl.multiple_of(pb * BLK, BLK)
        rows = slice(b * BLK, (b + 1) * BLK)
        pk = jnp.concatenate([pk_ref[pb], pk_ref[pb + 1]], axis=1)
        delta = (gb - pb) * BLK + qi - kj
        valid = (delta >= 0) & (delta <= window)
        dist = jnp.where(valid, pq_ref[rows, :] - pk, -MASKED)
        cols = [slice(h * BLK, (h + 1) * BLK) for h in heads]
        kv_cols = [slice(0, BLK)] * hps if kv_shared else cols
        sc = [lax.dot_general(q_ref[rows, cols[h]], k_ref[pl.ds(start, 2 * BLK), kv_cols[h]], _NT,
                              preferred_element_type=F32) * scale - slopes[h] * dist
              for h in heads]
        m = [jnp.max(sc[h], axis=1, keepdims=True) for h in heads]
        if has_sink:
            m = [jnp.maximum(m[h], sinks[h]) for h in heads]
        p = [jnp.exp(sc[h] - m[h]) for h in heads]
        den = [jnp.sum(p[h], axis=1, keepdims=True) for h in heads]
        if has_sink:
            den = [den[h] + jnp.exp(sinks[h] - m[h]) for h in heads]
        o = [jnp.dot((p[h] * (1.0 / den[h])).astype(BF16),
                     v_ref[pl.ds(start, 2 * BLK), kv_cols[h]], preferred_element_type=F32)
             for h in heads]
        for h in heads:
            o_ref[rows, cols[h]] = o[h].astype(o_ref.dtype)
            if want_lse:
                lse_ref[rows, cols[h]] = jnp.broadcast_to(m[h] + jnp.log(den[h]), (BLK, BLK))


def band_attention(arr, col_q, col_k, col_v, pos_col, pos_row, slopes, sinks, *,
                   n_seq, n_groups, hps, kv_shared, seq_len, window, out_dtype, want_lse):
    dh = BLK
    tile = min(seq_len, 4 * BLK)
    tiles = seq_len // tile
    has_sink = sinks is not None
    kvw = dh if kv_shared else hps * dh
    smem = pl.BlockSpec(memory_space=pltpu.SMEM)
    in_specs = [smem] + ([smem] if has_sink else []) + [
        pl.BlockSpec((tile, hps * dh), lambda r, g, t: (r * tiles + t, col_q(g))),
        pl.BlockSpec((seq_len, kvw), lambda r, g, t: (r, col_k(g))),
        pl.BlockSpec((seq_len, kvw), lambda r, g, t: (r, col_v(g))),
        pl.BlockSpec((None, tile, 1), lambda r, g, t: (r, t, 0)),
        pl.BlockSpec((None, seq_len // BLK, 1, BLK), lambda r, g, t: (r, 0, 0, 0)),
    ]
    args = [slopes] + ([sinks] if has_sink else []) + [arr, arr, arr, pos_col, pos_row]
    out_block = pl.BlockSpec((tile, hps * dh), lambda r, g, t: (t, r * n_groups + g))
    out_cols = n_seq * n_groups * hps * dh
    out_sds = jax.ShapeDtypeStruct((seq_len, out_cols), out_dtype)
    osize = jnp.dtype(out_dtype).itemsize
    vmem = (2 * tile * hps * dh * 2 + 2 * 2 * seq_len * kvw * 2 + 4 * tile * hps * dh * osize
            + 2 * tile * V7X_LANES * 4 + 2 * seq_len * 4 * 8 + hps * 8 * BLK * 2 * BLK * 4)
    return pl.pallas_call(
        functools.partial(_band_attn_body, tile=tile, hps=hps, kv_shared=kv_shared, window=window,
                          scale=dh ** -0.5, has_sink=has_sink, want_lse=want_lse),
        grid=(n_seq, n_groups, seq_len // tile),
        in_specs=in_specs,
        out_specs=[out_block, out_block] if want_lse else out_block,
        out_shape=[out_sds, jax.ShapeDtypeStruct((seq_len, out_cols), F32)] if want_lse else out_sds,
        compiler_params=_params(("arbitrary", "arbitrary", "arbitrary"), vmem),
        name="band_attention",
    )(*args)


def _alibi_slopes(n):
    return jnp.asarray(2.0 ** (-8.0 * np.arange(1, n + 1) / n), dtype=F32)


def _strided_positions(pos_f32, dil):
    s = pos_f32.shape[0]
    ps = pos_f32.reshape(s // dil, dil).T
    return ps.reshape(dil, s // dil, 1), ps.reshape(dil, s // dil // BLK, 1, BLK)


def _dil_merge_body(*refs, dils):
    n = len(dils)
    o_refs, l_refs, out_ref, buf = refs[:n], refs[n:2 * n], refs[2 * n], refs[2 * n + 1]
    tm, w = out_ref.shape

    def token_order(ref, d, c):
        if d == 1:
            return ref[:, c * V7X_LANES:(c + 1) * V7X_LANES]
        for r in range(d):
            lo = r * w + c * V7X_LANES
            buf[pl.ds(r, tm // d, stride=d), :] = ref[:, lo:lo + V7X_LANES]
        return buf[...]

    for c in range(w // V7X_LANES):
        lse = [token_order(l_refs[g], dils[g], c) for g in range(n)]
        m = functools.reduce(jnp.maximum, lse)
        e = [jnp.exp(l - m) for l in lse]
        inv = 1.0 / functools.reduce(lambda a, b: a + b, e)
        acc = None
        for g in range(n):
            term = (e[g] * inv) * token_order(o_refs[g], dils[g], c)
            acc = term if acc is None else acc + term
        out_ref[:, c * V7X_LANES:(c + 1) * V7X_LANES] = acc.astype(out_ref.dtype)


def dil_merge(outs, lses, dils, *, tm=256):
    m = outs[0].shape[0] * dils[0]
    w = outs[0].shape[1] // dils[0]
    tm = min(tm, m)
    specs = [pl.BlockSpec((tm // d, d * w), lambda i: (i, 0)) for d in dils]
    return pl.pallas_call(
        functools.partial(_dil_merge_body, dils=tuple(dils)),
        grid=(m // tm,),
        in_specs=specs + specs,
        out_specs=pl.BlockSpec((tm, w), lambda i: (i, 0)),
        out_shape=jax.ShapeDtypeStruct((m, w), BF16),
        scratch_shapes=[pltpu.VMEM((tm, V7X_LANES), F32)],
        compiler_params=_params(("arbitrary",), (4 * len(dils) + 4) * tm * w * 4),
        name="dil_merge",
    )(*outs, *lses)


def stick_breaking_mixer(hin, w_qkv, w_o, j):
    qkv = matmul(hin, w_qkv, bm=1024, bn=512, out_dtype=BF16, layer=j)
    o = sb_attention(qkv)
    return matmul(o, w_o, bm=1024, bn=512, out_dtype=F32, layer=j)


def mla_mixer(hin, pos_f32, w_dq, g_q, w_uq, w_dkv, g_kv, w_ukv, w_o, j):
    h, nope, rope = MLA_HEADS, MLA_NOPE, MLA_ROPE
    half = rope // 2
    w_dq, w_uq, w_dkv, g_q, g_kv = w_dq[j], w_uq[j], w_dkv[j], g_q[j], g_kv[j]
    kx1 = w_dkv[:, MLA_KV_LORA:MLA_KV_LORA + half]
    kx2 = w_dkv[:, MLA_KV_LORA + half:]
    w_down = jnp.concatenate(
        [w_dq, w_dkv[:, :MLA_KV_LORA], kx1, kx2, -kx2, kx1,
         jnp.zeros((w_dq.shape[0], V7X_LANES), w_dq.dtype)], axis=1)
    y = matmul(hin, w_down[None], bm=1024, bn=V7X_MXU_DIM, out_dtype=F32)
    cq, ckv, kr, cs = mla_prep(y, pos_f32.reshape(-1, 1), g_q, g_kv)
    wq = w_uq.reshape(MLA_Q_LORA, h, nope + rope)
    qx1, qx2 = wq[:, :, nope:nope + half], wq[:, :, nope + half:]
    wq = jnp.concatenate([wq[:, :, :nope], qx1, qx2, -qx2, qx1], axis=2)
    q_scale = (nope + rope) ** -0.5 * float(np.log2(np.e))
    q = matmul(cq, wq.reshape(1, MLA_Q_LORA, h * 2 * V7X_LANES), bm=2048, bn=1024,
               out_dtype=BF16, rope_cs=cs, out_scale=q_scale)
    wkv = w_ukv[j].reshape(MLA_KV_LORA, h, nope + MLA_V)
    w_k = wkv[:, :, :nope].reshape(1, MLA_KV_LORA, h * nope)
    w_vt = wkv[:, :, nope:].reshape(MLA_KV_LORA, h * MLA_V).T
    blk = min(512, hin.shape[0])
    kn = matmul(ckv, w_k, bm=2048, bn=1024, out_dtype=BF16)
    vt = matmul_nt_tiled(w_vt, ckv, bm=blk, bn=1024, out_dtype=BF16)
    o = mla_attention(q, kn, kr, vt, blk=blk)
    return matmul(o, w_o, bm=1024, bn=512, out_dtype=F32, layer=j)


def swa_mixer(hin, pos_f32, w_qkv, sinks, w_o, j):
    s = hin.shape[0]
    rep = SWA_HEADS // SWA_KV_HEADS
    sinks = sinks[j]
    qkv = matmul(hin, w_qkv, bm=1024, bn=512, out_dtype=BF16, layer=j)
    o = band_attention(
        qkv,
        lambda g: g,
        lambda g: SWA_HEADS + g,
        lambda g: SWA_HEADS + SWA_KV_HEADS + g,
        pos_f32.reshape(1, s, 1), pos_f32.reshape(1, s // BLK, 1, BLK),
        _alibi_slopes(SWA_HEADS), sinks,
        n_seq=1, n_groups=SWA_KV_HEADS, hps=rep, kv_shared=True, seq_len=s,
        window=SWA_WINDOW - 1, out_dtype=BF16, want_lse=False)
    return matmul(o, w_o, bm=1024, bn=512, out_dtype=F32, layer=j)


_DIL_VIEWS = tuple(sorted({dil for _, dil in DIL_PATTERNS if dil > 1}))


def dilated_mixer(hin, hin_views, pos_f32, w_qkv, w_o, j):
    s, d_model = hin.shape
    nh = DIL_HEADS
    n_groups = len(DIL_PATTERNS)
    gw = 3 * nh * DIL_HEAD_DIM
    slopes_all = _alibi_slopes(n_groups * nh)
    views = {1: hin, **dict(zip(_DIL_VIEWS, hin_views))}
    outs, lses = [], []
    for gi, (win, dil) in enumerate(DIL_PATTERNS):
        pos_col, pos_row = _strided_positions(pos_f32, dil)
        qkv = matmul(views[dil], w_qkv, bm=1024, bn=512, out_dtype=BF16,
                     layer=j, n_slice=(gi * gw, gw), row_groups=dil)
        hps = 4
        ng = nh // hps
        o, lse = band_attention(
            qkv,
            lambda g: g,
            lambda g, ng=ng: ng + g,
            lambda g, ng=ng: 2 * ng + g,
            pos_col, pos_row, slopes_all[gi * nh:(gi + 1) * nh], None,
            n_seq=dil, n_groups=ng, hps=hps, kv_shared=False, seq_len=s // dil,
            window=win // dil, out_dtype=F32, want_lse=True)
        outs.append(o)
        lses.append(lse)
    o = dil_merge(outs, lses, [dil for _, dil in DIL_PATTERNS])
    return matmul(o, w_o, bm=1024, bn=512, out_dtype=F32, layer=j)


def kernel(x, positions, norm_g, a_w_qkv, a_w_o, b_w_dq, b_g_q, b_w_uq, b_w_dkv, b_g_kv,
           b_w_ukv, b_w_o, c_w_qkv, c_sinks, c_w_o, d_w_qkv, d_w_o, ffn_w_in, ffn_conv_w,
           ffn_conv_b, ffn_w_out):
    batch, s, d = x.shape
    depth = norm_g.shape[0]
    n_mixers = 4
    outs = []
    for bi in range(batch):
        xb = x[bi]
        pos_f32 = positions[bi].astype(F32)
        hin = norm_rows(xb, norm_g[0, 0])
        hin_views = ()
        for i in range(depth):
            mixer, j = i % n_mixers, i // n_mixers
            if mixer == 0:
                h = stick_breaking_mixer(hin, a_w_qkv, a_w_o, j)
            elif mixer == 1:
                h = mla_mixer(hin, pos_f32, b_w_dq, b_g_q, b_w_uq, b_w_dkv, b_g_kv, b_w_ukv,
                              b_w_o, j)
            elif mixer == 2:
                h = swa_mixer(hin, pos_f32, c_w_qkv, c_sinks, c_w_o, j)
            else:
                h = dilated_mixer(hin, hin_views, pos_f32, d_w_qkv, d_w_o, j)
            xb, xn = resid_norm(h, xb, norm_g[i, 1], norm_g[i, 2])
            act, w_out16 = ffn_in(xn, ffn_w_in, i, ffn_conv_w[i], ffn_conv_b[i], ffn_w_out)
            h = ffn_out(act, w_out16)
            if i + 1 == depth:
                xb = resid_norm(h, xb, norm_g[i, 3])
            elif (i + 1) % n_mixers == 3:
                xb, hin, hin_views = resid_norm(h, xb, norm_g[i, 3], norm_g[i + 1, 0],
                                                dils=_DIL_VIEWS)
            else:
                xb, hin = resid_norm(h, xb, norm_g[i, 3], norm_g[i + 1, 0])
        outs.append(xb)
    return jnp.stack(outs, axis=0)
```

```python
import functools

import numpy as np
import jax
import jax.numpy as jnp
from jax import lax
from jax.experimental import pallas as pl
from jax.experimental.pallas import tpu as pltpu

F32 = jnp.float32
BF16 = jnp.bfloat16
H_DTYPE = BF16

V7X_VMEM_BYTES = 64 * 2**20
V7X_LANES = 128
V7X_MXU_DIM = 256

EPS = 1e-6
BLK = 128
SB_HEADS, SB_HEAD_DIM = 32, 128
MLA_HEADS, MLA_NOPE, MLA_ROPE, MLA_V = 32, 128, 64, 128
MLA_Q_LORA, MLA_KV_LORA = 1024, 512
ROPE_THETA = 10000.0
SWA_HEADS, SWA_KV_HEADS, SWA_HEAD_DIM, SWA_WINDOW = 32, 8, 128, 128
DIL_PATTERNS = ((128, 1), (512, 4), (2048, 16))
DIL_HEADS, DIL_HEAD_DIM = 16, 128
CONV_W = 3

MASKED = -1e30
EXP_ZERO_BELOW = -105.0

_NT = (((1,), (1,)), ((), ()))


_VMEM_SPILL_ALLOWANCE = 8 * 2**20
_VMEM_RESERVE = 6 * 2**20


def _params(semantics, vmem_bytes):
    limit = min(int(vmem_bytes) + _VMEM_SPILL_ALLOWANCE, V7X_VMEM_BYTES - _VMEM_RESERVE)
    return pltpu.CompilerParams(dimension_semantics=semantics, vmem_limit_bytes=limit)


def _rms(x, g):
    return (x * lax.rsqrt(jnp.mean(x * x, axis=-1, keepdims=True) + EPS)) * g


def _mm_body(*refs, mode, out_scale, sub):
    x_ref, w_ref, o_ref = refs[0], refs[1], refs[-1]
    if sub > 1:
        wb = w_ref[...].astype(BF16)
        rows, k = x_ref.shape[0], x_ref.shape[1] // sub
        for r in range(sub):
            o_ref[r * rows:(r + 1) * rows, :] = jnp.dot(
                x_ref[:, r * k:(r + 1) * k], wb, preferred_element_type=F32).astype(o_ref.dtype)
        return
    acc = jnp.dot(x_ref[...], w_ref[...].astype(BF16), preferred_element_type=F32)
    if mode == "plain":
        o_ref[...] = acc.astype(o_ref.dtype)
    elif mode == "add":
        o_ref[...] = (refs[2][...] + acc).astype(o_ref.dtype)
    else:
        cs = refs[2][...] * out_scale
        for c in range(acc.shape[1] // (2 * V7X_LANES)):
            lo = 2 * c * V7X_LANES
            o_ref[:, lo:lo + V7X_LANES] = (acc[:, lo:lo + V7X_LANES] * out_scale
                                           ).astype(o_ref.dtype)
            o_ref[:, lo + V7X_LANES:lo + 2 * V7X_LANES] = (
                acc[:, lo + V7X_LANES:lo + 2 * V7X_LANES] * cs).astype(o_ref.dtype)


def matmul(x, w, *, bm, bn, out_dtype, layer=0, k_slice=None, n_slice=None, row_groups=1,
           add=None, rope_cs=None, out_scale=1.0):
    n0, n = (0, w.shape[2]) if n_slice is None else n_slice
    if row_groups == 1:
        m, kx = x.shape
    else:
        assert k_slice is None
        m, kx = x.shape[0] * row_groups, x.shape[1] // row_groups
    kb, tk = (0, kx) if k_slice is None else k_slice
    group_rows = m // row_groups
    bm, bn = min(bm, m), min(bn, n)
    sub = max(bm // group_rows, 1)
    assert n % bn == 0 and n0 % bn == 0 and m % bm == 0
    assert (group_rows % bm == 0) if sub == 1 else (bm == sub * group_rows and add is None
                                                    and rope_cs is None)
    per_group = max(group_rows // bm, 1)
    in_specs = [pl.BlockSpec((bm // sub, sub * tk),
                             lambda i, j: (i % per_group, i // per_group + kb)),
                pl.BlockSpec((None, tk, bn), lambda i, j: (layer, kb, n0 // bn + j))]
    args = [x, w]
    mode = "plain"
    aliases = {}
    extra = 0
    if add is not None:
        in_specs.append(pl.BlockSpec((bm, bn), lambda i, j: (i, j)))
        args.append(add)
        mode = "add"
        aliases = {2: 0} if add.dtype == jnp.dtype(out_dtype) else {}
        extra = 2 * bm * bn * add.dtype.itemsize
    if rope_cs is not None:
        in_specs.append(pl.BlockSpec((bm, V7X_LANES), lambda i, j: (i, 0)))
        args.append(rope_cs)
        mode = "rope"
    osize = jnp.dtype(out_dtype).itemsize
    vmem = (2 * bm * tk * 2 + 2 * tk * bn * w.dtype.itemsize + tk * bn * 2
            + 2 * bm * bn * osize + bm * bn * 4 + extra)
    return pl.pallas_call(
        functools.partial(_mm_body, mode=mode, out_scale=out_scale, sub=sub),
        grid=(m // bm, n // bn),
        in_specs=in_specs,
        out_specs=pl.BlockSpec((bm, bn), lambda i, j: (i, j)),
        out_shape=jax.ShapeDtypeStruct((m, n), out_dtype),
        input_output_aliases=aliases,
        compiler_params=_params(("arbitrary", "arbitrary"), vmem),
        name="matmul",
    )(*args)


def _mm_nt_body(w_ref, x_ref, o_ref):
    o_ref[...] = lax.dot_general(w_ref[...].astype(BF16), x_ref[...], _NT,
                                 preferred_element_type=F32).astype(o_ref.dtype)


def matmul_nt_tiled(w_t, x, *, bm, bn, out_dtype):
    n, k = w_t.shape
    m = x.shape[0]
    bm, bn = min(bm, m), min(bn, n)
    assert m % bm == 0 and n % bn == 0
    vmem = 3 * bn * k * 4 + 2 * bm * k * 2 + 2 * bn * bm * jnp.dtype(out_dtype).itemsize + bn * bm * 4
    return pl.pallas_call(
        _mm_nt_body,
        grid=(n // bn, m // bm),
        in_specs=[pl.BlockSpec((bn, k), lambda j, i: (j, 0)),
                  pl.BlockSpec((bm, k), lambda j, i: (i, 0))],
        out_specs=pl.BlockSpec((None, bn, bm), lambda j, i: (i, j, 0)),
        out_shape=jax.ShapeDtypeStruct((m // bm, n, bm), out_dtype),
        compiler_params=_params(("arbitrary", "arbitrary"), vmem),
        name="matmul_nt",
    )(w_t, x)


def _norm_body(x_ref, g_ref, o_ref):
    o_ref[...] = _rms(x_ref[...], g_ref[...]).astype(o_ref.dtype)


def norm_rows(x, g, *, tm=256):
    m, d = x.shape
    tm = min(tm, m)
    return pl.pallas_call(
        _norm_body,
        grid=(m // tm,),
        in_specs=[pl.BlockSpec((tm, d), lambda i: (i, 0)),
                  pl.BlockSpec((1, d), lambda i: (0, 0))],
        out_specs=pl.BlockSpec((tm, d), lambda i: (i, 0)),
        out_shape=jax.ShapeDtypeStruct((m, d), BF16),
        compiler_params=_params(("arbitrary",), 2 * tm * d * 6 + 3 * tm * d * 4),
        name="norm_rows",
    )(x, g.reshape(1, d))


def _resid_norm_body(h_ref, x_ref, ga_ref, gb_ref, xo_ref, no_ref, *rest, dils):
    xn = x_ref[...] + _rms(h_ref[...].astype(F32), ga_ref[...])
    xo_ref[...] = xn
    n = _rms(xn, gb_ref[...])
    no_ref[...] = n.astype(no_ref.dtype)
    if dils:
        n_scr = rest[-1]
        tm, dm = n.shape
        chunks = dm // V7X_LANES
        for c in range(chunks):
            n_scr[c] = n[:, c * V7X_LANES:(c + 1) * V7X_LANES]
        for d, ref in zip(dils, rest[:-1]):
            for r in range(d):
                for c in range(chunks):
                    lo = r * dm + c * V7X_LANES
                    ref[:, lo:lo + V7X_LANES] = n_scr[c, pl.ds(r, tm // d, stride=d), :].astype(
                        ref.dtype)


def _resid_body(h_ref, x_ref, ga_ref, xo_ref):
    xo_ref[...] = x_ref[...] + _rms(h_ref[...].astype(F32), ga_ref[...])


def resid_norm(h, x, g_a, g_b=None, *, tm=256, dils=()):
    m, d = x.shape
    if dils:
        tm = 16 * max(dils)
    tm = min(tm, m)
    row = pl.BlockSpec((tm, d), lambda i: (i, 0))
    gain = pl.BlockSpec((1, d), lambda i: (0, 0))
    if g_b is None:
        return pl.pallas_call(
            _resid_body,
            grid=(m // tm,),
            in_specs=[row, row, gain],
            out_specs=row,
            out_shape=jax.ShapeDtypeStruct((m, d), F32),
            compiler_params=_params(("arbitrary",), 2 * tm * d * 12 + 3 * tm * d * 4),
            name="resid",
        )(h, x, g_a.reshape(1, d))
    outs = pl.pallas_call(
        functools.partial(_resid_norm_body, dils=tuple(dils)),
        grid=(m // tm,),
        in_specs=[row, row, gain, gain],
        out_specs=[row, row] + [pl.BlockSpec((tm // dl, dl * d), lambda i: (i, 0)) for dl in dils],
        out_shape=[jax.ShapeDtypeStruct((m, d), F32), jax.ShapeDtypeStruct((m, d), BF16)]
        + [jax.ShapeDtypeStruct((m // dl, dl * d), BF16) for dl in dils],
        scratch_shapes=[pltpu.VMEM((d // V7X_LANES, tm, V7X_LANES), F32)] if dils else [],
        compiler_params=_params(("arbitrary",),
                                2 * tm * d * (14 + 2 * len(dils)) + 5 * tm * d * 4),
        name="resid_norm",
    )(h, x, g_a.reshape(1, d), g_b.reshape(1, d))
    return (outs[0], outs[1]) + ((tuple(outs[2:]),) if dils else ())


_HALO = 8


def _order_after(x, v):
    zero = (lax.bitcast_convert_type(v, jnp.uint32) >> 16) >> 16
    return pltpu.bitcast(pltpu.bitcast(x, jnp.uint32) | zero, BF16)


def _ffn_in_body(x_ref, wg_ref, wu_ref, cw_ref, cb_ref, wo_ref, o_ref, wo16_ref, wb_ref, g_ref,
                 u_ref, t_ref, *, bm, cm, row_tiles):
    s = pl.program_id(0)
    i = s % row_tiles
    bn = o_ref.shape[1]
    nc = bm // cm

    @pl.when(s == 0)
    def _():
        g_ref[...] = jnp.zeros(g_ref.shape, F32)
        u_ref[...] = jnp.zeros(u_ref.shape, F32)

    @pl.when(i == 0)
    def _():
        wb_ref[:, 0:bn] = wg_ref[...].astype(BF16)
        wb_ref[:, bn:2 * bn] = wu_ref[...].astype(BF16)

    wo16_ref[...] = wo_ref[...].astype(BF16)

    cw = cw_ref[...]
    cb = cb_ref[...]
    for c in range(nc):
        rows = slice(c * cm, (c + 1) * cm)
        lo = _HALO + c * cm
        g = g_ref[lo:lo + cm, :]
        g1 = g_ref[lo - 1:lo - 1 + cm, :]
        g2 = g_ref[lo - 2:lo - 2 + cm, :]
        gate = cw[0:1, :] * g2 + cw[1:2, :] * g1 + cw[2:3, :] * g + cb
        act = (gate / (1.0 + jnp.exp(-gate))) * u_ref[rows, :]
        o_ref[rows, :] = act.astype(o_ref.dtype)
        if c == 0:
            g_ref[0:_HALO, :] = jnp.where(i == 0, 0.0, g_ref[bm:bm + _HALO, :])
        else:
            g_ref[lo - _HALO:lo, :] = t_ref[c - 1]
        x0 = _order_after(x_ref[rows, 0:V7X_LANES], act[0:cm // 2, 0:V7X_LANES])
        x = jnp.concatenate([x0, x_ref[rows, V7X_LANES:]], axis=1)
        gu = jnp.dot(x, wb_ref[...], preferred_element_type=F32)
        u_ref[rows, :] = gu[:, bn:2 * bn]
        if c == nc - 1:
            g_ref[lo:lo + cm, :] = gu[:, 0:bn]
        else:
            g_ref[lo:lo + cm - _HALO, :] = gu[0:cm - _HALO, 0:bn]
            t_ref[c] = gu[cm - _HALO:cm, 0:bn]


def ffn_in(xn, w_in, layer, conv_w, conv_b, w_out, *, bm=1024, bn=V7X_MXU_DIM, cm=128):
    m, k = xn.shape
    d_ff = w_in.shape[2] // 2
    d_out = w_out.shape[2]
    bm = min(bm, m)
    bn = min(bn, d_ff)
    assert m % bm == 0 and d_ff % bn == 0
    nj, ni = d_ff // bn, m // bm
    tiles = nj * ni
    cm = min(cm, bm)
    assert bm % cm == 0 and cm % (2 * _HALO) == 0
    slab = d_ff // tiles
    assert slab * tiles == d_ff and slab % 16 == 0

    def mm(f):
        return lambda s: f(jnp.minimum(s, tiles - 1) // ni, jnp.minimum(s, tiles - 1) % ni)

    def ep(f):
        return lambda s: f(jnp.maximum(s - 1, 0) // ni, jnp.maximum(s - 1, 0) % ni)

    vmem = (2 * bm * k * 2 + 2 * 2 * k * bn * 4 + 2 * k * bn * 2 + 2 * bm * bn * 2
            + (2 * bm + _HALO) * bn * 4 + 6 * bm * bn * 4)
    return pl.pallas_call(
        functools.partial(_ffn_in_body, bm=bm, cm=cm, row_tiles=ni),
        grid=(tiles + 1,),
        in_specs=[pl.BlockSpec((bm, k), mm(lambda j, i: (i, 0))),
                  pl.BlockSpec((None, k, bn), mm(lambda j, i: (layer, 0, j))),
                  pl.BlockSpec((None, k, bn), mm(lambda j, i: (layer, 0, nj + j))),
                  pl.BlockSpec((CONV_W, bn), ep(lambda j, i: (0, j))),
                  pl.BlockSpec((1, bn), ep(lambda j, i: (0, j))),
                  pl.BlockSpec((None, slab, d_out),
                               lambda s: (layer, jnp.minimum(s, tiles - 1), 0))],
        out_specs=[pl.BlockSpec((bm, bn), ep(lambda j, i: (i, j))),
                   pl.BlockSpec((slab, d_out), lambda s: (jnp.minimum(s, tiles - 1), 0))],
        out_shape=[jax.ShapeDtypeStruct((m, d_ff), BF16),
                   jax.ShapeDtypeStruct((d_ff, d_out), BF16)],
        scratch_shapes=[pltpu.VMEM((k, 2 * bn), BF16), pltpu.VMEM((bm + _HALO, bn), F32),
                        pltpu.VMEM((bm, bn), F32),
                        pltpu.VMEM((max(bm // cm - 1, 1), _HALO, bn), F32)],
        compiler_params=_params(("arbitrary",), vmem + 12 * slab * d_out),
        name="ffn_in",
    )(xn, w_in, w_in, conv_w, conv_b.reshape(1, d_ff), w_out)


def ffn_out(act, w_out16, *, bm=1024, bn=2 * V7X_MXU_DIM):
    d_ff = act.shape[1]
    w = w_out16[None]
    if d_ff % (2 * V7X_LANES) != 0:
        return matmul(act, w, bm=bm, bn=bn, out_dtype=H_DTYPE)
    half = d_ff // 2
    part = matmul(act, w, bm=bm, bn=bn, out_dtype=F32, k_slice=(0, half))
    return matmul(act, w, bm=bm, bn=bn, out_dtype=H_DTYPE, k_slice=(1, half), add=part)


_SB_HEADS_PER_STEP = 4


def _sb_attn_body(q_ref, k_ref, v_ref, tri_ref, o_ref, acc_ref, c_ref, *, blk, scale):
    qi = pl.program_id(1)
    nh = _SB_HEADS_PER_STEP
    dh = SB_HEAD_DIM
    tri = tri_ref[...]
    acc_ref[...] = jnp.zeros(acc_ref.shape, F32)
    c_ref[...] = jnp.zeros(c_ref.shape, F32)
    key = lax.broadcasted_iota(jnp.int32, (blk, blk), 0)
    qry = lax.broadcasted_iota(jnp.int32, (blk, blk), 1)
    causal = key < qry

    def block(kb, diagonal):
        start = pl.multiple_of(kb * blk, blk)
        heads = range(nh)
        cols = [slice(hd * dh, (hd + 1) * dh) for hd in heads]
        z = [lax.dot_general(k_ref[pl.ds(start, blk), cols[hd]], q_ref[:, cols[hd]], _NT,
                             preferred_element_type=F32) * scale for hd in heads]
        vt = [v_ref[pl.ds(start, blk), cols[hd]].astype(F32).T.astype(BF16) for hd in heads]
        log_beta, log_1mb, hi, lo = [], [], [], []
        for hd in heads:
            sp = jnp.log(1.0 + jnp.exp(-jnp.abs(z[hd])))
            log_beta.append(jnp.minimum(z[hd], 0.0) - sp)
            l1 = jnp.minimum(-z[hd], 0.0) - sp
            if diagonal:
                l1 = jnp.where(causal, l1, 0.0)
            log_1mb.append(l1)
            hi.append(l1.astype(BF16))
            lo.append((l1 - hi[hd].astype(F32)).astype(BF16))
        tail = [jnp.dot(tri, hi[hd], preferred_element_type=F32)
                + jnp.dot(tri, lo[hd], preferred_element_type=F32) for hd in heads]
        c = [c_ref[hd] for hd in heads]
        a = []
        for hd in heads:
            w = jnp.exp(log_beta[hd] + tail[hd] + c[hd])
            if diagonal:
                w = jnp.where(causal, w, 0.0)
            a.append(w.astype(BF16))
        top = None
        for hd in heads:
            acc_ref[hd] += jnp.dot(vt[hd], a[hd], preferred_element_type=F32)
            c_new = c[hd] + tail[hd][0:1, :] + log_1mb[hd][0:1, :]
            c_ref[hd] = c_new
            top = jnp.max(c_new) if top is None else jnp.maximum(top, jnp.max(c_new))
        return top

    top = block(qi, True)
    lax.while_loop(lambda st: (st[0] >= 0) & (st[1] > EXP_ZERO_BELOW),
                   lambda st: (st[0] - 1, block(st[0], False)),
                   (qi - 1, top))
    for hd in range(nh):
        o_ref[:, hd * dh:(hd + 1) * dh] = acc_ref[hd].T.astype(o_ref.dtype)


def sb_attention(qkv, *, blk=256):
    s = qkv.shape[0]
    h, dh = SB_HEADS, SB_HEAD_DIM
    nh = _SB_HEADS_PER_STEP
    blk = min(blk, s)
    groups = h // nh
    tri = (np.arange(blk)[:, None] < np.arange(blk)[None, :]).astype(np.float32)
    vmem = (2 * 2 * s * nh * dh * 2 + 4 * blk * nh * dh * 2 + 2 * blk * blk * 2
            + nh * (dh + 8) * blk * 4 + nh * 12 * blk * blk * 4)
    return pl.pallas_call(
        functools.partial(_sb_attn_body, blk=blk, scale=dh ** -0.5),
        grid=(groups, s // blk),
        in_specs=[pl.BlockSpec((blk, nh * dh), lambda g, i: (i, g)),
                  pl.BlockSpec((s, nh * dh), lambda g, i: (0, groups + g)),
                  pl.BlockSpec((s, nh * dh), lambda g, i: (0, 2 * groups + g)),
                  pl.BlockSpec((blk, blk), lambda g, i: (0, 0))],
        out_specs=pl.BlockSpec((blk, nh * dh), lambda g, i: (i, g)),
        out_shape=jax.ShapeDtypeStruct((s, h * dh), BF16),
        scratch_shapes=[pltpu.VMEM((nh, dh, blk), F32), pltpu.VMEM((nh, 1, blk), F32)],
        compiler_params=_params(("arbitrary", "arbitrary"), vmem),
        name="sb_attention",
    )(qkv, qkv, qkv, jnp.asarray(tri, BF16))


def _mla_prep_body(y_ref, pos_ref, inv_ref, gq_ref, gkv_ref, cq_ref, ckv_ref, kr_ref, cs_ref):
    ql, kvl = MLA_Q_LORA, MLA_KV_LORA
    cq_ref[...] = _rms(y_ref[:, 0:ql], gq_ref[...]).astype(cq_ref.dtype)
    ckv_ref[...] = _rms(y_ref[:, ql:ql + kvl], gkv_ref[...]).astype(ckv_ref.dtype)
    ang = pos_ref[...] * inv_ref[...]
    lane = lax.broadcasted_iota(jnp.int32, ang.shape, 1)
    cs = jnp.where(lane < MLA_ROPE, jnp.cos(ang), jnp.sin(ang))
    cs_ref[...] = cs
    w = y_ref[:, ql + kvl:ql + kvl + V7X_LANES] * cs
    kr_ref[...] = (w + pltpu.roll(w, MLA_ROPE, axis=1)).astype(kr_ref.dtype)


def mla_prep(y, pos_col, g_q, g_kv, *, tm=256):
    m, width = y.shape
    tm = min(tm, m)
    half = MLA_ROPE // 2
    inv = ROPE_THETA ** (-jnp.arange(half, dtype=F32) / half)
    inv4 = jnp.tile(inv, 4).reshape(1, V7X_LANES)
    row = lambda w: pl.BlockSpec((tm, w), lambda i: (i, 0))
    const = lambda w: pl.BlockSpec((1, w), lambda i: (0, 0))
    return pl.pallas_call(
        _mla_prep_body,
        grid=(m // tm,),
        in_specs=[row(width), row(1), const(V7X_LANES), const(MLA_Q_LORA), const(MLA_KV_LORA)],
        out_specs=[row(MLA_Q_LORA), row(MLA_KV_LORA), row(V7X_LANES), row(V7X_LANES)],
        out_shape=[jax.ShapeDtypeStruct((m, MLA_Q_LORA), BF16),
                   jax.ShapeDtypeStruct((m, MLA_KV_LORA), BF16),
                   jax.ShapeDtypeStruct((m, V7X_LANES), BF16),
                   jax.ShapeDtypeStruct((m, V7X_LANES), F32)],
        compiler_params=_params(("arbitrary",), 6 * tm * width * 4),
        name="mla_prep",
    )(y, pos_col, inv4, g_q.reshape(1, -1), g_kv.reshape(1, -1))


_MLA_HEADS_PER_STEP = 4


def _mla_attn_body(q_ref, kn_ref, kr_ref, vt_ref, o_ref, s_ref, m_ref, l_ref, acc_ref, *, blk):
    qi = pl.program_id(1)
    nh = _MLA_HEADS_PER_STEP
    wq = MLA_NOPE + V7X_LANES
    m_ref[...] = jnp.full(m_ref.shape, MASKED, F32)
    l_ref[...] = jnp.zeros(l_ref.shape, F32)
    acc_ref[...] = jnp.zeros(acc_ref.shape, F32)
    key = lax.broadcasted_iota(jnp.int32, (blk, blk), 0)
    qry = lax.broadcasted_iota(jnp.int32, (blk, blk), 1)

    def scores(kb, slot):
        start = pl.multiple_of(kb * blk, blk)
        kr = kr_ref[pl.ds(start, blk), :]
        for hd in range(nh):
            kcat = jnp.concatenate(
                [kn_ref[pl.ds(start, blk), hd * MLA_NOPE:(hd + 1) * MLA_NOPE], kr], axis=1)
            s_ref[slot, hd] = lax.dot_general(kcat, q_ref[:, hd * wq:(hd + 1) * wq], _NT,
                                              preferred_element_type=F32)

    def reduce(kb, slot, diagonal):
        heads = range(nh)
        st = [s_ref[slot, hd] for hd in heads]
        if diagonal:
            st = [jnp.where(key <= qry, t, MASKED) for t in st]
        m_prev = [m_ref[hd] for hd in heads]
        m_new = [jnp.maximum(m_prev[hd], jnp.max(st[hd], axis=0, keepdims=True)) for hd in heads]
        p = [jnp.exp2(st[hd] - m_new[hd]) for hd in heads]
        pv = [jnp.dot(vt_ref[kb, hd * MLA_V:(hd + 1) * MLA_V, :], p[hd].astype(BF16),
                      preferred_element_type=F32) for hd in heads]
        for hd in heads:
            alpha = jnp.exp2(m_prev[hd] - m_new[hd])
            l_ref[hd] = alpha * l_ref[hd] + jnp.sum(p[hd], axis=0, keepdims=True)
            acc_ref[hd] = alpha * acc_ref[hd] + pv[hd]
            m_ref[hd] = m_new[hd]

    def off_diagonal_pair(t, carry):
        scores(2 * t + 1, 1)
        reduce(2 * t, 0, False)
        scores(2 * t + 2, 0)
        reduce(2 * t + 1, 1, False)
        return carry

    scores(0, 0)
    lax.fori_loop(0, qi // 2, off_diagonal_pair, 0)

    @pl.when(qi % 2 == 0)
    def _():
        reduce(qi, 0, True)

    @pl.when(qi % 2 == 1)
    def _():
        scores(qi, 1)
        reduce(qi - 1, 0, False)
        reduce(qi, 1, True)

    for hd in range(nh):
        o_ref[:, hd * MLA_V:(hd + 1) * MLA_V] = (acc_ref[hd] / l_ref[hd]).T.astype(o_ref.dtype)


def mla_attention(q, kn, kr, vt, *, blk):
    s = q.shape[0]
    h = MLA_HEADS
    nh = _MLA_HEADS_PER_STEP
    wq = MLA_NOPE + V7X_LANES
    assert vt.shape == (s // blk, h * MLA_V, blk)
    vmem = (2 * blk * nh * wq * 2 + 2 * s * nh * MLA_NOPE * 2 + 2 * s * V7X_LANES * 2
            + 2 * s * nh * MLA_V * 2 + 2 * blk * nh * MLA_V * 2 + nh * (MLA_V + 16) * blk * 4
            + nh * 8 * blk * blk * 4)
    return pl.pallas_call(
        functools.partial(_mla_attn_body, blk=blk),
        grid=(h // nh, s // blk),
        in_specs=[pl.BlockSpec((blk, nh * wq), lambda hp, i: (i, hp)),
                  pl.BlockSpec((s, nh * MLA_NOPE), lambda hp, i: (0, hp)),
                  pl.BlockSpec((s, V7X_LANES), lambda hp, i: (0, 0)),
                  pl.BlockSpec((s // blk, nh * MLA_V, blk), lambda hp, i: (0, hp, 0))],
        out_specs=pl.BlockSpec((blk, nh * MLA_V), lambda hp, i: (i, hp)),
        out_shape=jax.ShapeDtypeStruct((s, h * MLA_V), BF16),
        scratch_shapes=[pltpu.VMEM((2, nh, blk, blk), F32),
                        pltpu.VMEM((nh, 1, blk), F32), pltpu.VMEM((nh, 1, blk), F32),
                        pltpu.VMEM((nh, MLA_V, blk), F32)],
        compiler_params=_params(("arbitrary", "arbitrary"), vmem),
        name="mla_attention",
    )(q, kn, kr, vt)


def _band_attn_body(*refs, tile, hps, kv_shared, window, scale, has_sink, want_lse):
    refs = list(refs)
    slope_ref = refs.pop(0)
    sink_ref = refs.pop(0) if has_sink else None
    q_ref, k_ref, v_ref, pq_ref, pk_ref, o_ref = refs[:6]
    lse_ref = refs[6] if want_lse else None
    hbase = pl.program_id(1) * hps
    t = pl.program_id(2)
    nb = tile // BLK
    heads = range(hps)
    qi = lax.broadcasted_iota(jnp.int32, (BLK, 2 * BLK), 0)
    kj = lax.broadcasted_iota(jnp.int32, (BLK, 2 * BLK), 1)
    slopes = [slope_ref[hbase + h] for h in heads]
    sinks = [sink_ref[hbase + h] for h in heads] if has_sink else None
    for b in range(nb):
        gb = t * nb + b
        pb = jnp.maximum(gb - 1, 0)
        start = pl.multiple_of(pb * BLK, BLK)
        rows = slice(b * BLK, (b + 1) * BLK)
        pk = jnp.concatenate([pk_ref[pb], pk_ref[pb + 1]], axis=1)
        delta = (gb - pb) * BLK + qi - kj
        valid = (delta >= 0) & (delta <= window)
        dist = jnp.where(valid, pq_ref[rows, :] - pk, -MASKED)
        cols = [slice(h * BLK, (h + 1) * BLK) for h in heads]
        kv_cols = [slice(0, BLK)] * hps if kv_shared else cols
        sc = [lax.dot_general(q_ref[rows, cols[h]], k_ref[pl.ds(start, 2 * BLK), kv_cols[h]], _NT,
                              preferred_element_type=F32) * scale - slopes[h] * dist
              for h in heads]
        m = [jnp.max(sc[h], axis=1, keepdims=True) for h in heads]
        if has_sink:
            m = [jnp.maximum(m[h], sinks[h]) for h in heads]
        p = [jnp.exp(sc[h] - m[h]) for h in heads]
        den = [jnp.sum(p[h], axis=1, keepdims=True) for h in heads]
        if has_sink:
            den = [den[h] + jnp.exp(sinks[h] - m[h]) for h in heads]
        o = [jnp.dot((p[h] * (1.0 / den[h])).astype(BF16),
                     v_ref[pl.ds(start, 2 * BLK), kv_cols[h]], preferred_element_type=F32)
             for h in heads]
        for h in heads:
            o_ref[rows, cols[h]] = o[h].astype(o_ref.dtype)
            if want_lse:
                lse_ref[rows, cols[h]] = jnp.broadcast_to(m[h] + jnp.log(den[h]), (BLK, BLK))


def band_attention(arr, col_q, col_k, col_v, pos_col, pos_row, slopes, sinks, *,
                   n_seq, n_groups, hps, kv_shared, seq_len, window, out_dtype, want_lse):
    dh = BLK
    tile = min(seq_len, 4 * BLK)
    tiles = seq_len // tile
    has_sink = sinks is not None
    kvw = dh if kv_shared else hps * dh
    smem = pl.BlockSpec(memory_space=pltpu.SMEM)
    in_specs = [smem] + ([smem] if has_sink else []) + [
        pl.BlockSpec((tile, hps * dh), lambda r, g, t: (r * tiles + t, col_q(g))),
        pl.BlockSpec((seq_len, kvw), lambda r, g, t: (r, col_k(g))),
        pl.BlockSpec((seq_len, kvw), lambda r, g, t: (r, col_v(g))),
        pl.BlockSpec((None, tile, 1), lambda r, g, t: (r, t, 0)),
        pl.BlockSpec((None, seq_len // BLK, 1, BLK), lambda r, g, t: (r, 0, 0, 0)),
    ]
    args = [slopes] + ([sinks] if has_sink else []) + [arr, arr, arr, pos_col, pos_row]
    out_block = pl.BlockSpec((tile, hps * dh), lambda r, g, t: (t, r * n_groups + g))
    out_cols = n_seq * n_groups * hps * dh
    out_sds = jax.ShapeDtypeStruct((seq_len, out_cols), out_dtype)
    osize = jnp.dtype(out_dtype).itemsize
    vmem = (2 * tile * hps * dh * 2 + 2 * 2 * seq_len * kvw * 2 + 4 * tile * hps * dh * osize
            + 2 * tile * V7X_LANES * 4 + 2 * seq_len * 4 * 8 + hps * 8 * BLK * 2 * BLK * 4)
    return pl.pallas_call(
        functools.partial(_band_attn_body, tile=tile, hps=hps, kv_shared=kv_shared, window=window,
                          scale=dh ** -0.5, has_sink=has_sink, want_lse=want_lse),
        grid=(n_seq, n_groups, seq_len // tile),
        in_specs=in_specs,
        out_specs=[out_block, out_block] if want_lse else out_block,
        out_shape=[out_sds, jax.ShapeDtypeStruct((seq_len, out_cols), F32)] if want_lse else out_sds,
        compiler_params=_params(("arbitrary", "arbitrary", "arbitrary"), vmem),
        name="band_attention",
    )(*args)


def _alibi_slopes(n):
    return jnp.asarray(2.0 ** (-8.0 * np.arange(1, n + 1) / n), dtype=F32)


def _strided_positions(pos_f32, dil):
    s = pos_f32.shape[0]
    ps = pos_f32.reshape(s // dil, dil).T
    return ps.reshape(dil, s // dil, 1), ps.reshape(dil, s // dil // BLK, 1, BLK)


def _dil_merge_body(*refs, dils):
    n = len(dils)
    o_refs, l_refs, out_ref, buf = refs[:n], refs[n:2 * n], refs[2 * n], refs[2 * n + 1]
    tm, w = out_ref.shape

    def token_order(ref, d, c):
        if d == 1:
            return ref[:, c * V7X_LANES:(c + 1) * V7X_LANES]
        for r in range(d):
            lo = r * w + c * V7X_LANES
            buf[pl.ds(r, tm // d, stride=d), :] = ref[:, lo:lo + V7X_LANES]
        return buf[...]

    for c in range(w // V7X_LANES):
        lse = [token_order(l_refs[g], dils[g], c) for g in range(n)]
        m = functools.reduce(jnp.maximum, lse)
        e = [jnp.exp(l - m) for l in lse]
        inv = 1.0 / functools.reduce(lambda a, b: a + b, e)
        acc = None
        for g in range(n):
            term = (e[g] * inv) * token_order(o_refs[g], dils[g], c)
            acc = term if acc is None else acc + term
        out_ref[:, c * V7X_LANES:(c + 1) * V7X_LANES] = acc.astype(out_ref.dtype)


def dil_merge(outs, lses, dils, *, tm=256):
    m = outs[0].shape[0] * dils[0]
    w = outs[0].shape[1] // dils[0]
    tm = min(tm, m)
    specs = [pl.BlockSpec((tm // d, d * w), lambda i: (i, 0)) for d in dils]
    return pl.pallas_call(
        functools.partial(_dil_merge_body, dils=tuple(dils)),
        grid=(m // tm,),
        in_specs=specs + specs,
        out_specs=pl.BlockSpec((tm, w), lambda i: (i, 0)),
        out_shape=jax.ShapeDtypeStruct((m, w), BF16),
        scratch_shapes=[pltpu.VMEM((tm, V7X_LANES), F32)],
        compiler_params=_params(("arbitrary",), (4 * len(dils) + 4) * tm * w * 4),
        name="dil_merge",
    )(*outs, *lses)


def stick_breaking_mixer(hin, w_qkv, w_o, j):
    qkv = matmul(hin, w_qkv, bm=1024, bn=512, out_dtype=BF16, layer=j)
    o = sb_attention(qkv)
    return matmul(o, w_o, bm=1024, bn=512, out_dtype=H_DTYPE, layer=j)


def mla_mixer(hin, pos_f32, w_dq, g_q, w_uq, w_dkv, g_kv, w_ukv, w_o, j):
    h, nope, rope = MLA_HEADS, MLA_NOPE, MLA_ROPE
    half = rope // 2
    w_dq, w_uq, w_dkv, g_q, g_kv = w_dq[j], w_uq[j], w_dkv[j], g_q[j], g_kv[j]
    kx1 = w_dkv[:, MLA_KV_LORA:MLA_KV_LORA + half]
    kx2 = w_dkv[:, MLA_KV_LORA + half:]
    w_down = jnp.concatenate(
        [w_dq, w_dkv[:, :MLA_KV_LORA], kx1, kx2, -kx2, kx1,
         jnp.zeros((w_dq.shape[0], V7X_LANES), w_dq.dtype)], axis=1)
    y = matmul(hin, w_down[None], bm=1024, bn=V7X_MXU_DIM, out_dtype=F32)
    cq, ckv, kr, cs = mla_prep(y, pos_f32.reshape(-1, 1), g_q, g_kv)
    wq = w_uq.reshape(MLA_Q_LORA, h, nope + rope)
    qx1, qx2 = wq[:, :, nope:nope + half], wq[:, :, nope + half:]
    wq = jnp.concatenate([wq[:, :, :nope], qx1, qx2, -qx2, qx1], axis=2)
    q_scale = (nope + rope) ** -0.5 * float(np.log2(np.e))
    q = matmul(cq, wq.reshape(1, MLA_Q_LORA, h * 2 * V7X_LANES), bm=2048, bn=1024,
               out_dtype=BF16, rope_cs=cs, out_scale=q_scale)
    wkv = w_ukv[j].reshape(MLA_KV_LORA, h, nope + MLA_V)
    w_k = wkv[:, :, :nope].reshape(1, MLA_KV_LORA, h * nope)
    w_vt = wkv[:, :, nope:].reshape(MLA_KV_LORA, h * MLA_V).T
    blk = min(512, hin.shape[0])
    kn = matmul(ckv, w_k, bm=2048, bn=1024, out_dtype=BF16)
    vt = matmul_nt_tiled(w_vt, ckv, bm=blk, bn=1024, out_dtype=BF16)
    o = mla_attention(q, kn, kr, vt, blk=blk)
    return matmul(o, w_o, bm=1024, bn=512, out_dtype=H_DTYPE, layer=j)


def swa_mixer(hin, pos_f32, w_qkv, sinks, w_o, j):
    s = hin.shape[0]
    rep = SWA_HEADS // SWA_KV_HEADS
    sinks = sinks[j]
    qkv = matmul(hin, w_qkv, bm=1024, bn=512, out_dtype=BF16, layer=j)
    o = band_attention(
        qkv,
        lambda g: g,
        lambda g: SWA_HEADS + g,
        lambda g: SWA_HEADS + SWA_KV_HEADS + g,
        pos_f32.reshape(1, s, 1), pos_f32.reshape(1, s // BLK, 1, BLK),
        _alibi_slopes(SWA_HEADS), sinks,
        n_seq=1, n_groups=SWA_KV_HEADS, hps=rep, kv_shared=True, seq_len=s,
        window=SWA_WINDOW - 1, out_dtype=BF16, want_lse=False)
    return matmul(o, w_o, bm=1024, bn=512, out_dtype=H_DTYPE, layer=j)


_DIL_VIEWS = tuple(sorted({dil for _, dil in DIL_PATTERNS if dil > 1}))


def dilated_mixer(hin, hin_views, pos_f32, w_qkv, w_o, j):
    s, d_model = hin.shape
    nh = DIL_HEADS
    n_groups = len(DIL_PATTERNS)
    gw = 3 * nh * DIL_HEAD_DIM
    slopes_all = _alibi_slopes(n_groups * nh)
    views = {1: hin, **dict(zip(_DIL_VIEWS, hin_views))}
    outs, lses = [], []
    for gi, (win, dil) in enumerate(DIL_PATTERNS):
        pos_col, pos_row = _strided_positions(pos_f32, dil)
        qkv = matmul(views[dil], w_qkv, bm=1024, bn=512, out_dtype=BF16,
                     layer=j, n_slice=(gi * gw, gw), row_groups=dil)
        hps = 4
        ng = nh // hps
        o, lse = band_attention(
            qkv,
            lambda g: g,
            lambda g, ng=ng: ng + g,
            lambda g, ng=ng: 2 * ng + g,
            pos_col, pos_row, slopes_all[gi * nh:(gi + 1) * nh], None,
            n_seq=dil, n_groups=ng, hps=hps, kv_shared=False, seq_len=s // dil,
            window=win // dil, out_dtype=F32, want_lse=True)
        outs.append(o)
        lses.append(lse)
    o = dil_merge(outs, lses, [dil for _, dil in DIL_PATTERNS])
    return matmul(o, w_o, bm=1024, bn=512, out_dtype=H_DTYPE, layer=j)


def kernel(x, positions, norm_g, a_w_qkv, a_w_o, b_w_dq, b_g_q, b_w_uq, b_w_dkv, b_g_kv,
           b_w_ukv, b_w_o, c_w_qkv, c_sinks, c_w_o, d_w_qkv, d_w_o, ffn_w_in, ffn_conv_w,
           ffn_conv_b, ffn_w_out):
    batch, s, d = x.shape
    depth = norm_g.shape[0]
    n_mixers = 4
    outs = []
    for bi in range(batch):
        xb = x[bi]
        pos_f32 = positions[bi].astype(F32)
        hin = norm_rows(xb, norm_g[0, 0])
        hin_views = ()
        for i in range(depth):
            mixer, j = i % n_mixers, i // n_mixers
            if mixer == 0:
                h = stick_breaking_mixer(hin, a_w_qkv, a_w_o, j)
            elif mixer == 1:
                h = mla_mixer(hin, pos_f32, b_w_dq, b_g_q, b_w_uq, b_w_dkv, b_g_kv, b_w_ukv,
                              b_w_o, j)
            elif mixer == 2:
                h = swa_mixer(hin, pos_f32, c_w_qkv, c_sinks, c_w_o, j)
            else:
                h = dilated_mixer(hin, hin_views, pos_f32, d_w_qkv, d_w_o, j)
            xb, xn = resid_norm(h, xb, norm_g[i, 1], norm_g[i, 2])
            act, w_out16 = ffn_in(xn, ffn_w_in, i, ffn_conv_w[i], ffn_conv_b[i], ffn_w_out)
            h = ffn_out(act, w_out16)
            if i + 1 == depth:
                xb = resid_norm(h, xb, norm_g[i, 3])
            elif (i + 1) % n_mixers == 3:
                xb, hin, hin_views = resid_norm(h, xb, norm_g[i, 3], norm_g[i + 1, 0],
                                                dils=_DIL_VIEWS)
            else:
                xb, hin = resid_norm(h, xb, norm_g[i, 3], norm_g[i + 1, 0])
        outs.append(xb)
    return jnp.stack(outs, axis=0)
```

```python
import functools

import numpy as np
import jax
import jax.numpy as jnp
from jax import lax
from jax.experimental import pallas as pl
from jax.experimental.pallas import tpu as pltpu

F32 = jnp.float32
BF16 = jnp.bfloat16
H_DTYPE = BF16

V7X_VMEM_BYTES = 64 * 2**20
V7X_LANES = 128
V7X_MXU_DIM = 256

EPS = 1e-6
BLK = 128
SB_HEADS, SB_HEAD_DIM = 32, 128
MLA_HEADS, MLA_NOPE, MLA_ROPE, MLA_V = 32, 128, 64, 128
MLA_Q_LORA, MLA_KV_LORA = 1024, 512
ROPE_THETA = 10000.0
SWA_HEADS, SWA_KV_HEADS, SWA_HEAD_DIM, SWA_WINDOW = 32, 8, 128, 128
DIL_PATTERNS = ((128, 1), (512, 4), (2048, 16))
DIL_HEADS, DIL_HEAD_DIM = 16, 128
CONV_W = 3

MASKED = -1e30
EXP_ZERO_BELOW = -105.0

_NT = (((1,), (1,)), ((), ()))


_VMEM_SPILL_ALLOWANCE = 8 * 2**20
_VMEM_RESERVE = 6 * 2**20


def _params(semantics, vmem_bytes):
    limit = min(int(vmem_bytes) + _VMEM_SPILL_ALLOWANCE, V7X_VMEM_BYTES - _VMEM_RESERVE)
    return pltpu.CompilerParams(dimension_semantics=semantics, vmem_limit_bytes=limit)


def _rms(x, g):
    return (x * lax.rsqrt(jnp.mean(x * x, axis=-1, keepdims=True) + EPS)) * g


def _mm_body(*refs, mode, out_scale, sub):
    x_ref, w_ref, o_ref = refs[0], refs[1], refs[-1]
    if sub > 1:
        wb = w_ref[...].astype(BF16)
        rows, k = x_ref.shape[0], x_ref.shape[1] // sub
        for r in range(sub):
            o_ref[r * rows:(r + 1) * rows, :] = jnp.dot(
                x_ref[:, r * k:(r + 1) * k], wb, preferred_element_type=F32).astype(o_ref.dtype)
        return
    acc = jnp.dot(x_ref[...], w_ref[...].astype(BF16), preferred_element_type=F32)
    if mode == "plain":
        o_ref[...] = acc.astype(o_ref.dtype)
    elif mode == "add":
        o_ref[...] = (refs[2][...] + acc).astype(o_ref.dtype)
    else:
        cs = refs[2][...] * out_scale
        for c in range(acc.shape[1] // (2 * V7X_LANES)):
            lo = 2 * c * V7X_LANES
            o_ref[:, lo:lo + V7X_LANES] = (acc[:, lo:lo + V7X_LANES] * out_scale
                                           ).astype(o_ref.dtype)
            o_ref[:, lo + V7X_LANES:lo + 2 * V7X_LANES] = (
                acc[:, lo + V7X_LANES:lo + 2 * V7X_LANES] * cs).astype(o_ref.dtype)


def matmul(x, w, *, bm, bn, out_dtype, layer=0, k_slice=None, n_slice=None, row_groups=1,
           add=None, rope_cs=None, out_scale=1.0):
    n0, n = (0, w.shape[2]) if n_slice is None else n_slice
    if row_groups == 1:
        m, kx = x.shape
    else:
        assert k_slice is None
        m, kx = x.shape[0] * row_groups, x.shape[1] // row_groups
    kb, tk = (0, kx) if k_slice is None else k_slice
    group_rows = m // row_groups
    bm, bn = min(bm, m), min(bn, n)
    sub = max(bm // group_rows, 1)
    assert n % bn == 0 and n0 % bn == 0 and m % bm == 0
    assert (group_rows % bm == 0) if sub == 1 else (bm == sub * group_rows and add is None
                                                    and rope_cs is None)
    per_group = max(group_rows // bm, 1)
    in_specs = [pl.BlockSpec((bm // sub, sub * tk),
                             lambda i, j: (i % per_group, i // per_group + kb)),
                pl.BlockSpec((None, tk, bn), lambda i, j: (layer, kb, n0 // bn + j))]
    args = [x, w]
    mode = "plain"
    aliases = {}
    extra = 0
    if add is not None:
        in_specs.append(pl.BlockSpec((bm, bn), lambda i, j: (i, j)))
        args.append(add)
        mode = "add"
        aliases = {2: 0} if add.dtype == jnp.dtype(out_dtype) else {}
        extra = 2 * bm * bn * add.dtype.itemsize
    if rope_cs is not None:
        in_specs.append(pl.BlockSpec((bm, V7X_LANES), lambda i, j: (i, 0)))
        args.append(rope_cs)
        mode = "rope"
    osize = jnp.dtype(out_dtype).itemsize
    vmem = (2 * bm * tk * 2 + 2 * tk * bn * w.dtype.itemsize + tk * bn * 2
            + 2 * bm * bn * osize + bm * bn * 4 + extra)
    return pl.pallas_call(
        functools.partial(_mm_body, mode=mode, out_scale=out_scale, sub=sub),
        grid=(m // bm, n // bn),
        in_specs=in_specs,
        out_specs=pl.BlockSpec((bm, bn), lambda i, j: (i, j)),
        out_shape=jax.ShapeDtypeStruct((m, n), out_dtype),
        input_output_aliases=aliases,
        compiler_params=_params(("arbitrary", "arbitrary"), vmem),
        name="matmul",
    )(*args)


def _mm_nt_body(w_ref, x_ref, o_ref):
    o_ref[...] = lax.dot_general(w_ref[...].astype(BF16), x_ref[...], _NT,
                                 preferred_element_type=F32).astype(o_ref.dtype)


def matmul_nt_tiled(w_t, x, *, bm, bn, out_dtype):
    n, k = w_t.shape
    m = x.shape[0]
    bm, bn = min(bm, m), min(bn, n)
    assert m % bm == 0 and n % bn == 0
    vmem = 3 * bn * k * 4 + 2 * bm * k * 2 + 2 * bn * bm * jnp.dtype(out_dtype).itemsize + bn * bm * 4
    return pl.pallas_call(
        _mm_nt_body,
        grid=(n // bn, m // bm),
        in_specs=[pl.BlockSpec((bn, k), lambda j, i: (j, 0)),
                  pl.BlockSpec((bm, k), lambda j, i: (i, 0))],
        out_specs=pl.BlockSpec((None, bn, bm), lambda j, i: (i, j, 0)),
        out_shape=jax.ShapeDtypeStruct((m // bm, n, bm), out_dtype),
        compiler_params=_params(("arbitrary", "arbitrary"), vmem),
        name="matmul_nt",
    )(w_t, x)


def _norm_body(x_ref, g_ref, o_ref):
    o_ref[...] = _rms(x_ref[...], g_ref[...]).astype(o_ref.dtype)


def norm_rows(x, g, *, tm=256):
    m, d = x.shape
    tm = min(tm, m)
    return pl.pallas_call(
        _norm_body,
        grid=(m // tm,),
        in_specs=[pl.BlockSpec((tm, d), lambda i: (i, 0)),
                  pl.BlockSpec((1, d), lambda i: (0, 0))],
        out_specs=pl.BlockSpec((tm, d), lambda i: (i, 0)),
        out_shape=jax.ShapeDtypeStruct((m, d), BF16),
        compiler_params=_params(("arbitrary",), 2 * tm * d * 6 + 3 * tm * d * 4),
        name="norm_rows",
    )(x, g.reshape(1, d))


def _resid_norm_body(h_ref, x_ref, ga_ref, gb_ref, xo_ref, no_ref, *rest, dils):
    xn = x_ref[...] + _rms(h_ref[...].astype(F32), ga_ref[...])
    xo_ref[...] = xn
    n = _rms(xn, gb_ref[...])
    no_ref[...] = n.astype(no_ref.dtype)
    if dils:
        n_scr = rest[-1]
        tm, dm = n.shape
        chunks = dm // V7X_LANES
        for c in range(chunks):
            n_scr[c] = n[:, c * V7X_LANES:(c + 1) * V7X_LANES]
        for d, ref in zip(dils, rest[:-1]):
            for r in range(d):
                for c in range(chunks):
                    lo = r * dm + c * V7X_LANES
                    ref[:, lo:lo + V7X_LANES] = n_scr[c, pl.ds(r, tm // d, stride=d), :].astype(
                        ref.dtype)


def _resid_body(h_ref, x_ref, ga_ref, xo_ref):
    xo_ref[...] = x_ref[...] + _rms(h_ref[...].astype(F32), ga_ref[...])


def resid_norm(h, x, g_a, g_b=None, *, tm=256, dils=()):
    m, d = x.shape
    if dils:
        tm = 16 * max(dils)
    tm = min(tm, m)
    row = pl.BlockSpec((tm, d), lambda i: (i, 0))
    gain = pl.BlockSpec((1, d), lambda i: (0, 0))
    if g_b is None:
        return pl.pallas_call(
            _resid_body,
            grid=(m // tm,),
            in_specs=[row, row, gain],
            out_specs=row,
            out_shape=jax.ShapeDtypeStruct((m, d), F32),
            compiler_params=_params(("arbitrary",), 2 * tm * d * 12 + 3 * tm * d * 4),
            name="resid",
        )(h, x, g_a.reshape(1, d))
    outs = pl.pallas_call(
        functools.partial(_resid_norm_body, dils=tuple(dils)),
        grid=(m // tm,),
        in_specs=[row, row, gain, gain],
        out_specs=[row, row] + [pl.BlockSpec((tm // dl, dl * d), lambda i: (i, 0)) for dl in dils],
        out_shape=[jax.ShapeDtypeStruct((m, d), F32), jax.ShapeDtypeStruct((m, d), BF16)]
        + [jax.ShapeDtypeStruct((m // dl, dl * d), BF16) for dl in dils],
        scratch_shapes=[pltpu.VMEM((d // V7X_LANES, tm, V7X_LANES), F32)] if dils else [],
        compiler_params=_params(("arbitrary",),
                                2 * tm * d * (14 + 2 * len(dils)) + 5 * tm * d * 4),
        name="resid_norm",
    )(h, x, g_a.reshape(1, d), g_b.reshape(1, d))
    return (outs[0], outs[1]) + ((tuple(outs[2:]),) if dils else ())


_HALO = 8


def _order_after(x, v):
    zero = (lax.bitcast_convert_type(v, jnp.uint32) >> 16) >> 16
    return pltpu.bitcast(pltpu.bitcast(x, jnp.uint32) | zero, BF16)


def _ffn_in_body(x_ref, wg_ref, wu_ref, cw_ref, cb_ref, wo_ref, o_ref, wo16_ref, wb_ref, g_ref,
                 u_ref, t_ref, *, bm, cm, row_tiles):
    s = pl.program_id(0)
    i = s % row_tiles
    bn = o_ref.shape[1]
    nc = bm // cm

    @pl.when(s == 0)
    def _():
        g_ref[...] = jnp.zeros(g_ref.shape, F32)
        u_ref[...] = jnp.zeros(u_ref.shape, F32)

    @pl.when(i == 0)
    def _():
        wb_ref[:, 0:bn] = wg_ref[...].astype(BF16)
        wb_ref[:, bn:2 * bn] = wu_ref[...].astype(BF16)

    wo16_ref[...] = wo_ref[...].astype(BF16)

    cw = cw_ref[...]
    cb = cb_ref[...]
    for c in range(nc):
        rows = slice(c * cm, (c + 1) * cm)
        lo = _HALO + c * cm
        g = g_ref[lo:lo + cm, :]
        g1 = g_ref[lo - 1:lo - 1 + cm, :]
        g2 = g_ref[lo - 2:lo - 2 + cm, :]
        gate = cw[0:1, :] * g2 + cw[1:2, :] * g1 + cw[2:3, :] * g + cb
        act = (gate / (1.0 + jnp.exp(-gate))) * u_ref[rows, :]
        o_ref[rows, :] = act.astype(o_ref.dtype)
        if c == 0:
            g_ref[0:_HALO, :] = jnp.where(i == 0, 0.0, g_ref[bm:bm + _HALO, :])
        else:
            g_ref[lo - _HALO:lo, :] = t_ref[c - 1]
        x0 = _order_after(x_ref[rows, 0:V7X_LANES], act[0:cm // 2, 0:V7X_LANES])
        x = jnp.concatenate([x0, x_ref[rows, V7X_LANES:]], axis=1)
        gu = jnp.dot(x, wb_ref[...], preferred_element_type=F32)
        u_ref[rows, :] = gu[:, bn:2 * bn]
        if c == nc - 1:
            g_ref[lo:lo + cm, :] = gu[:, 0:bn]
        else:
            g_ref[lo:lo + cm - _HALO, :] = gu[0:cm - _HALO, 0:bn]
            t_ref[c] = gu[cm - _HALO:cm, 0:bn]


def ffn_in(xn, w_in, layer, conv_w, conv_b, w_out, *, bm=1024, bn=V7X_MXU_DIM, cm=128):
    m, k = xn.shape
    d_ff = w_in.shape[2] // 2
    d_out = w_out.shape[2]
    bm = min(bm, m)
    bn = min(bn, d_ff)
    assert m % bm == 0 and d_ff % bn == 0
    nj, ni = d_ff // bn, m // bm
    tiles = nj * ni
    cm = min(cm, bm)
    assert bm % cm == 0 and cm % (2 * _HALO) == 0
    slab = d_ff // tiles
    assert slab * tiles == d_ff and slab % 16 == 0

    def mm(f):
        return lambda s: f(jnp.minimum(s, tiles - 1) // ni, jnp.minimum(s, tiles - 1) % ni)

    def ep(f):
        return lambda s: f(jnp.maximum(s - 1, 0) // ni, jnp.maximum(s - 1, 0) % ni)

    vmem = (2 * bm * k * 2 + 2 * 2 * k * bn * 4 + 2 * k * bn * 2 + 2 * bm * bn * 2
            + (2 * bm + _HALO) * bn * 4 + 6 * bm * bn * 4)
    return pl.pallas_call(
        functools.partial(_ffn_in_body, bm=bm, cm=cm, row_tiles=ni),
        grid=(tiles + 1,),
        in_specs=[pl.BlockSpec((bm, k), mm(lambda j, i: (i, 0))),
                  pl.BlockSpec((None, k, bn), mm(lambda j, i: (layer, 0, j))),
                  pl.BlockSpec((None, k, bn), mm(lambda j, i: (layer, 0, nj + j))),
                  pl.BlockSpec((CONV_W, bn), ep(lambda j, i: (0, j))),
                  pl.BlockSpec((1, bn), ep(lambda j, i: (0, j))),
                  pl.BlockSpec((None, slab, d_out),
                               lambda s: (layer, jnp.minimum(s, tiles - 1), 0))],
        out_specs=[pl.BlockSpec((bm, bn), ep(lambda j, i: (i, j))),
                   pl.BlockSpec((slab, d_out), lambda s: (jnp.minimum(s, tiles - 1), 0))],
        out_shape=[jax.ShapeDtypeStruct((m, d_ff), BF16),
                   jax.ShapeDtypeStruct((d_ff, d_out), BF16)],
        scratch_shapes=[pltpu.VMEM((k, 2 * bn), BF16), pltpu.VMEM((bm + _HALO, bn), F32),
                        pltpu.VMEM((bm, bn), F32),
                        pltpu.VMEM((max(bm // cm - 1, 1), _HALO, bn), F32)],
        compiler_params=_params(("arbitrary",), vmem + 12 * slab * d_out),
        name="ffn_in",
    )(xn, w_in, w_in, conv_w, conv_b.reshape(1, d_ff), w_out)


def ffn_out(act, w_out16, *, bm=1024, bn=2 * V7X_MXU_DIM):
    return matmul(act, w_out16[None], bm=bm // 2, bn=bn, out_dtype=H_DTYPE)


_SB_HEADS_PER_STEP = 4


def _sb_attn_body(q_ref, k_ref, v_ref, tri_ref, o_ref, acc_ref, c_ref, *, blk, scale):
    qi = pl.program_id(1)
    nh = _SB_HEADS_PER_STEP
    dh = SB_HEAD_DIM
    tri = tri_ref[...]
    acc_ref[...] = jnp.zeros(acc_ref.shape, F32)
    c_ref[...] = jnp.zeros(c_ref.shape, F32)
    key = lax.broadcasted_iota(jnp.int32, (blk, blk), 0)
    qry = lax.broadcasted_iota(jnp.int32, (blk, blk), 1)
    causal = key < qry

    def block(kb, diagonal):
        start = pl.multiple_of(kb * blk, blk)
        heads = range(nh)
        cols = [slice(hd * dh, (hd + 1) * dh) for hd in heads]
        z = [lax.dot_general(k_ref[pl.ds(start, blk), cols[hd]], q_ref[:, cols[hd]], _NT,
                             preferred_element_type=F32) * scale for hd in heads]
        vt = [v_ref[pl.ds(start, blk), cols[hd]].astype(F32).T.astype(BF16) for hd in heads]
        log_beta, log_1mb, hi, lo = [], [], [], []
        for hd in heads:
            sp = jnp.log(1.0 + jnp.exp(-jnp.abs(z[hd])))
            log_beta.append(jnp.minimum(z[hd], 0.0) - sp)
            l1 = jnp.minimum(-z[hd], 0.0) - sp
            if diagonal:
                l1 = jnp.where(causal, l1, 0.0)
            log_1mb.append(l1)
            hi.append(l1.astype(BF16))
            lo.append((l1 - hi[hd].astype(F32)).astype(BF16))
        tail = [jnp.dot(tri, hi[hd], preferred_element_type=F32)
                + jnp.dot(tri, lo[hd], preferred_element_type=F32) for hd in heads]
        c = [c_ref[hd] for hd in heads]
        a = []
        for hd in heads:
            w = jnp.exp(log_beta[hd] + tail[hd] + c[hd])
            if diagonal:
                w = jnp.where(causal, w, 0.0)
            a.append(w.astype(BF16))
        top = None
        for hd in heads:
            acc_ref[hd] += jnp.dot(vt[hd], a[hd], preferred_element_type=F32)
            c_new = c[hd] + tail[hd][0:1, :] + log_1mb[hd][0:1, :]
            c_ref[hd] = c_new
            top = jnp.max(c_new) if top is None else jnp.maximum(top, jnp.max(c_new))
        return top

    top = block(qi, True)
    lax.while_loop(lambda st: (st[0] >= 0) & (st[1] > EXP_ZERO_BELOW),
                   lambda st: (st[0] - 1, block(st[0], False)),
                   (qi - 1, top))
    for hd in range(nh):
        o_ref[:, hd * dh:(hd + 1) * dh] = acc_ref[hd].T.astype(o_ref.dtype)


def sb_attention(qkv, *, blk=256):
    s = qkv.shape[0]
    h, dh = SB_HEADS, SB_HEAD_DIM
    nh = _SB_HEADS_PER_STEP
    blk = min(blk, s)
    groups = h // nh
    tri = (np.arange(blk)[:, None] < np.arange(blk)[None, :]).astype(np.float32)
    vmem = (2 * 2 * s * nh * dh * 2 + 4 * blk * nh * dh * 2 + 2 * blk * blk * 2
            + nh * (dh + 8) * blk * 4 + nh * 12 * blk * blk * 4)
    return pl.pallas_call(
        functools.partial(_sb_attn_body, blk=blk, scale=dh ** -0.5),
        grid=(groups, s // blk),
        in_specs=[pl.BlockSpec((blk, nh * dh), lambda g, i: (i, g)),
                  pl.BlockSpec((s, nh * dh), lambda g, i: (0, groups + g)),
                  pl.BlockSpec((s, nh * dh), lambda g, i: (0, 2 * groups + g)),
                  pl.BlockSpec((blk, blk), lambda g, i: (0, 0))],
        out_specs=pl.BlockSpec((blk, nh * dh), lambda g, i: (i, g)),
        out_shape=jax.ShapeDtypeStruct((s, h * dh), BF16),
        scratch_shapes=[pltpu.VMEM((nh, dh, blk), F32), pltpu.VMEM((nh, 1, blk), F32)],
        compiler_params=_params(("arbitrary", "arbitrary"), vmem),
        name="sb_attention",
    )(qkv, qkv, qkv, jnp.asarray(tri, BF16))


def _mla_prep_body(y_ref, pos_ref, inv_ref, gq_ref, gkv_ref, cq_ref, ckv_ref, kr_ref, cs_ref):
    ql, kvl = MLA_Q_LORA, MLA_KV_LORA
    cq_ref[...] = _rms(y_ref[:, 0:ql], gq_ref[...]).astype(cq_ref.dtype)
    ckv_ref[...] = _rms(y_ref[:, ql:ql + kvl], gkv_ref[...]).astype(ckv_ref.dtype)
    ang = pos_ref[...] * inv_ref[...]
    lane = lax.broadcasted_iota(jnp.int32, ang.shape, 1)
    cs = jnp.where(lane < MLA_ROPE, jnp.cos(ang), jnp.sin(ang))
    cs_ref[...] = cs
    w = y_ref[:, ql + kvl:ql + kvl + V7X_LANES] * cs
    kr_ref[...] = (w + pltpu.roll(w, MLA_ROPE, axis=1)).astype(kr_ref.dtype)


def mla_prep(y, pos_col, g_q, g_kv, *, tm=256):
    m, width = y.shape
    tm = min(tm, m)
    half = MLA_ROPE // 2
    inv = ROPE_THETA ** (-jnp.arange(half, dtype=F32) / half)
    inv4 = jnp.tile(inv, 4).reshape(1, V7X_LANES)
    row = lambda w: pl.BlockSpec((tm, w), lambda i: (i, 0))
    const = lambda w: pl.BlockSpec((1, w), lambda i: (0, 0))
    return pl.pallas_call(
        _mla_prep_body,
        grid=(m // tm,),
        in_specs=[row(width), row(1), const(V7X_LANES), const(MLA_Q_LORA), const(MLA_KV_LORA)],
        out_specs=[row(MLA_Q_LORA), row(MLA_KV_LORA), row(V7X_LANES), row(V7X_LANES)],
        out_shape=[jax.ShapeDtypeStruct((m, MLA_Q_LORA), BF16),
                   jax.ShapeDtypeStruct((m, MLA_KV_LORA), BF16),
                   jax.ShapeDtypeStruct((m, V7X_LANES), BF16),
                   jax.ShapeDtypeStruct((m, V7X_LANES), F32)],
        compiler_params=_params(("arbitrary",), 6 * tm * width * 4),
        name="mla_prep",
    )(y, pos_col, inv4, g_q.reshape(1, -1), g_kv.reshape(1, -1))


_MLA_HEADS_PER_STEP = 4


def _mla_attn_body(q_ref, kn_ref, kr_ref, vt_ref, o_ref, s_ref, m_ref, l_ref, acc_ref, *, blk):
    qi = pl.program_id(1)
    nh = _MLA_HEADS_PER_STEP
    wq = MLA_NOPE + V7X_LANES
    m_ref[...] = jnp.full(m_ref.shape, MASKED, F32)
    l_ref[...] = jnp.zeros(l_ref.shape, F32)
    acc_ref[...] = jnp.zeros(acc_ref.shape, F32)
    key = lax.broadcasted_iota(jnp.int32, (blk, blk), 0)
    qry = lax.broadcasted_iota(jnp.int32, (blk, blk), 1)

    def scores(kb, slot):
        start = pl.multiple_of(kb * blk, blk)
        kr = kr_ref[pl.ds(start, blk), :]
        for hd in range(nh):
            kcat = jnp.concatenate(
                [kn_ref[pl.ds(start, blk), hd * MLA_NOPE:(hd + 1) * MLA_NOPE], kr], axis=1)
            s_ref[slot, hd] = lax.dot_general(kcat, q_ref[:, hd * wq:(hd + 1) * wq], _NT,
                                              preferred_element_type=F32)

    def reduce(kb, slot, diagonal):
        heads = range(nh)
        st = [s_ref[slot, hd] for hd in heads]
        if diagonal:
            st = [jnp.where(key <= qry, t, MASKED) for t in st]
        m_prev = [m_ref[hd] for hd in heads]
        m_new = [jnp.maximum(m_prev[hd], jnp.max(st[hd], axis=0, keepdims=True)) for hd in heads]
        p = [jnp.exp2(st[hd] - m_new[hd]) for hd in heads]
        pv = [jnp.dot(vt_ref[kb, hd * MLA_V:(hd + 1) * MLA_V, :], p[hd].astype(BF16),
                      preferred_element_type=F32) for hd in heads]
        for hd in heads:
            alpha = jnp.exp2(m_prev[hd] - m_new[hd])
            l_ref[hd] = alpha * l_ref[hd] + jnp.sum(p[hd], axis=0, keepdims=True)
            acc_ref[hd] = alpha * acc_ref[hd] + pv[hd]
            m_ref[hd] = m_new[hd]

    def off_diagonal_pair(t, carry):
        scores(2 * t + 1, 1)
        reduce(2 * t, 0, False)
        scores(2 * t + 2, 0)
        reduce(2 * t + 1, 1, False)
        return carry

    scores(0, 0)
    lax.fori_loop(0, qi // 2, off_diagonal_pair, 0)

    @pl.when(qi % 2 == 0)
    def _():
        reduce(qi, 0, True)

    @pl.when(qi % 2 == 1)
    def _():
        scores(qi, 1)
        reduce(qi - 1, 0, False)
        reduce(qi, 1, True)

    for hd in range(nh):
        o_ref[:, hd * MLA_V:(hd + 1) * MLA_V] = (acc_ref[hd] / l_ref[hd]).T.astype(o_ref.dtype)


def mla_attention(q, kn, kr, vt, *, blk):
    s = q.shape[0]
    h = MLA_HEADS
    nh = _MLA_HEADS_PER_STEP
    wq = MLA_NOPE + V7X_LANES
    assert vt.shape == (s // blk, h * MLA_V, blk)
    vmem = (2 * blk * nh * wq * 2 + 2 * s * nh * MLA_NOPE * 2 + 2 * s * V7X_LANES * 2
            + 2 * s * nh * MLA_V * 2 + 2 * blk * nh * MLA_V * 2 + nh * (MLA_V + 16) * blk * 4
            + nh * 8 * blk * blk * 4)
    return pl.pallas_call(
        functools.partial(_mla_attn_body, blk=blk),
        grid=(h // nh, s // blk),
        in_specs=[pl.BlockSpec((blk, nh * wq), lambda hp, i: (i, hp)),
                  pl.BlockSpec((s, nh * MLA_NOPE), lambda hp, i: (0, hp)),
                  pl.BlockSpec((s, V7X_LANES), lambda hp, i: (0, 0)),
                  pl.BlockSpec((s // blk, nh * MLA_V, blk), lambda hp, i: (0, hp, 0))],
        out_specs=pl.BlockSpec((blk, nh * MLA_V), lambda hp, i: (i, hp)),
        out_shape=jax.ShapeDtypeStruct((s, h * MLA_V), BF16),
        scratch_shapes=[pltpu.VMEM((2, nh, blk, blk), F32),
                        pltpu.VMEM((nh, 1, blk), F32), pltpu.VMEM((nh, 1, blk), F32),
                        pltpu.VMEM((nh, MLA_V, blk), F32)],
        compiler_params=_params(("arbitrary", "arbitrary"), vmem),
        name="mla_attention",
    )(q, kn, kr, vt)


def _band_attn_body(*refs, tile, hps, kv_shared, window, scale, has_sink, want_lse):
    refs = list(refs)
    slope_ref = refs.pop(0)
    sink_ref = refs.pop(0) if has_sink else None
    q_ref, k_ref, v_ref, pq_ref, pk_ref, o_ref = refs[:6]
    lse_ref = refs[6] if want_lse else None
    hbase = pl.program_id(1) * hps
    t = pl.program_id(2)
    nb = tile // BLK
    heads = range(hps)
    qi = lax.broadcasted_iota(jnp.int32, (BLK, 2 * BLK), 0)
    kj = lax.broadcasted_iota(jnp.int32, (BLK, 2 * BLK), 1)
    slopes = [slope_ref[hbase + h] for h in heads]
    sinks = [sink_ref[hbase + h] for h in heads] if has_sink else None
    for b in range(nb):
        gb = t * nb + b
        pb = jnp.maximum(gb - 1, 0)
        start = pl.multiple_of(pb * BLK, BLK)
        rows = slice(b * BLK, (b + 1) * BLK)
        pk = jnp.concatenate([pk_ref[pb], pk_ref[pb + 1]], axis=1)
        delta = (gb - pb) * BLK + qi - kj
        valid = (delta >= 0) & (delta <= window)
        dist = jnp.where(valid, pq_ref[rows, :] - pk, -MASKED)
        cols = [slice(h * BLK, (h + 1) * BLK) for h in heads]
        kv_cols = [slice(0, BLK)] * hps if kv_shared else cols
        sc = [lax.dot_general(q_ref[rows, cols[h]], k_ref[pl.ds(start, 2 * BLK), kv_cols[h]], _NT,
                              preferred_element_type=F32) * scale - slopes[h] * dist
              for h in heads]
        m = [jnp.max(sc[h], axis=1, keepdims=True) for h in heads]
        if has_sink:
            m = [jnp.maximum(m[h], sinks[h]) for h in heads]
        p = [jnp.exp(sc[h] - m[h]) for h in heads]
        den = [jnp.sum(p[h], axis=1, keepdims=True) for h in heads]
        if has_sink:
            den = [den[h] + jnp.exp(sinks[h] - m[h]) for h in heads]
        o = [jnp.dot((p[h] * (1.0 / den[h])).astype(BF16),
                     v_ref[pl.ds(start, 2 * BLK), kv_cols[h]], preferred_element_type=F32)
             for h in heads]
        for h in heads:
            o_ref[rows, cols[h]] = o[h].astype(o_ref.dtype)
            if want_lse:
                lse_ref[rows, cols[h]] = jnp.broadcast_to(m[h] + jnp.log(den[h]), (BLK, BLK))


def band_attention(arr, col_q, col_k, col_v, pos_col, pos_row, slopes, sinks, *,
                   n_seq, n_groups, hps, kv_shared, seq_len, window, out_dtype, want_lse):
    dh = BLK
    tile = min(seq_len, 4 * BLK)
    tiles = seq_len // tile
    has_sink = sinks is not None
    kvw = dh if kv_shared else hps * dh
    smem = pl.BlockSpec(memory_space=pltpu.SMEM)
    in_specs = [smem] + ([smem] if has_sink else []) + [
        pl.BlockSpec((tile, hps * dh), lambda r, g, t: (r * tiles + t, col_q(g))),
        pl.BlockSpec((seq_len, kvw), lambda r, g, t: (r, col_k(g))),
        pl.BlockSpec((seq_len, kvw), lambda r, g, t: (r, col_v(g))),
        pl.BlockSpec((None, tile, 1), lambda r, g, t: (r, t, 0)),
        pl.BlockSpec((None, seq_len // BLK, 1, BLK), lambda r, g, t: (r, 0, 0, 0)),
    ]
    args = [slopes] + ([sinks] if has_sink else []) + [arr, arr, arr, pos_col, pos_row]
    out_block = pl.BlockSpec((tile, hps * dh), lambda r, g, t: (t, r * n_groups + g))
    out_cols = n_seq * n_groups * hps * dh
    out_sds = jax.ShapeDtypeStruct((seq_len, out_cols), out_dtype)
    osize = jnp.dtype(out_dtype).itemsize
    vmem = (2 * tile * hps * dh * 2 + 2 * 2 * seq_len * kvw * 2 + 4 * tile * hps * dh * osize
            + 2 * tile * V7X_LANES * 4 + 2 * seq_len * 4 * 8 + hps * 8 * BLK * 2 * BLK * 4)
    return pl.pallas_call(
        functools.partial(_band_attn_body, tile=tile, hps=hps, kv_shared=kv_shared, window=window,
                          scale=dh ** -0.5, has_sink=has_sink, want_lse=want_lse),
        grid=(n_seq, n_groups, seq_len // tile),
        in_specs=in_specs,
        out_specs=[out_block, out_block] if want_lse else out_block,
        out_shape=[out_sds, jax.ShapeDtypeStruct((seq_len, out_cols), F32)] if want_lse else out_sds,
        compiler_params=_params(("arbitrary", "arbitrary", "arbitrary"), vmem),
        name="band_attention",
    )(*args)


def _alibi_slopes(n):
    return jnp.asarray(2.0 ** (-8.0 * np.arange(1, n + 1) / n), dtype=F32)


def _strided_positions(pos_f32, dil):
    s = pos_f32.shape[0]
    ps = pos_f32.reshape(s // dil, dil).T
    return ps.reshape(dil, s // dil, 1), ps.reshape(dil, s // dil // BLK, 1, BLK)


def _dil_merge_body(*refs, dils):
    n = len(dils)
    o_refs, l_refs, out_ref, buf = refs[:n], refs[n:2 * n], refs[2 * n], refs[2 * n + 1]
    tm, w = out_ref.shape

    def token_order(ref, d, c):
        if d == 1:
            return ref[:, c * V7X_LANES:(c + 1) * V7X_LANES]
        for r in range(d):
            lo = r * w + c * V7X_LANES
            buf[pl.ds(r, tm // d, stride=d), :] = ref[:, lo:lo + V7X_LANES]
        return buf[...]

    for c in range(w // V7X_LANES):
        lse = [token_order(l_refs[g], dils[g], c) for g in range(n)]
        m = functools.reduce(jnp.maximum, lse)
        e = [jnp.exp(l - m) for l in lse]
        inv = 1.0 / functools.reduce(lambda a, b: a + b, e)
        acc = None
        for g in range(n):
            term = (e[g] * inv) * token_order(o_refs[g], dils[g], c)
            acc = term if acc is None else acc + term
        out_ref[:, c * V7X_LANES:(c + 1) * V7X_LANES] = acc.astype(out_ref.dtype)


def dil_merge(outs, lses, dils, *, tm=256):
    m = outs[0].shape[0] * dils[0]
    w = outs[0].shape[1] // dils[0]
    tm = min(tm, m)
    specs = [pl.BlockSpec((tm // d, d * w), lambda i: (i, 0)) for d in dils]
    return pl.pallas_call(
        functools.partial(_dil_merge_body, dils=tuple(dils)),
        grid=(m // tm,),
        in_specs=specs + specs,
        out_specs=pl.BlockSpec((tm, w), lambda i: (i, 0)),
        out_shape=jax.ShapeDtypeStruct((m, w), BF16),
        scratch_shapes=[pltpu.VMEM((tm, V7X_LANES), F32)],
        compiler_params=_params(("arbitrary",), (4 * len(dils) + 4) * tm * w * 4),
        name="dil_merge",
    )(*outs, *lses)


def stick_breaking_mixer(hin, w_qkv, w_o, j):
    qkv = matmul(hin, w_qkv, bm=1024, bn=512, out_dtype=BF16, layer=j)
    o = sb_attention(qkv)
    return matmul(o, w_o, bm=1024, bn=512, out_dtype=H_DTYPE, layer=j)


def mla_mixer(hin, pos_f32, w_dq, g_q, w_uq, w_dkv, g_kv, w_ukv, w_o, j):
    h, nope, rope = MLA_HEADS, MLA_NOPE, MLA_ROPE
    half = rope // 2
    w_dq, w_uq, w_dkv, g_q, g_kv = w_dq[j], w_uq[j], w_dkv[j], g_q[j], g_kv[j]
    kx1 = w_dkv[:, MLA_KV_LORA:MLA_KV_LORA + half]
    kx2 = w_dkv[:, MLA_KV_LORA + half:]
    w_down = jnp.concatenate(
        [w_dq, w_dkv[:, :MLA_KV_LORA], kx1, kx2, -kx2, kx1,
         jnp.zeros((w_dq.shape[0], V7X_LANES), w_dq.dtype)], axis=1)
    y = matmul(hin, w_down[None], bm=1024, bn=V7X_MXU_DIM, out_dtype=F32)
    cq, ckv, kr, cs = mla_prep(y, pos_f32.reshape(-1, 1), g_q, g_kv)
    wq = w_uq.reshape(MLA_Q_LORA, h, nope + rope)
    qx1, qx2 = wq[:, :, nope:nope + half], wq[:, :, nope + half:]
    wq = jnp.concatenate([wq[:, :, :nope], qx1, qx2, -qx2, qx1], axis=2)
    q_scale = (nope + rope) ** -0.5 * float(np.log2(np.e))
    q = matmul(cq, wq.reshape(1, MLA_Q_LORA, h * 2 * V7X_LANES), bm=2048, bn=1024,
               out_dtype=BF16, rope_cs=cs, out_scale=q_scale)
    wkv = w_ukv[j].reshape(MLA_KV_LORA, h, nope + MLA_V)
    w_k = wkv[:, :, :nope].reshape(1, MLA_KV_LORA, h * nope)
    w_vt = wkv[:, :, nope:].reshape(MLA_KV_LORA, h * MLA_V).T
    blk = min(512, hin.shape[0])
    kn = matmul(ckv, w_k, bm=2048, bn=1024, out_dtype=BF16)
    vt = matmul_nt_tiled(w_vt, ckv, bm=blk, bn=1024, out_dtype=BF16)
    o = mla_attention(q, kn, kr, vt, blk=blk)
    return matmul(o, w_o, bm=1024, bn=512, out_dtype=H_DTYPE, layer=j)


def swa_mixer(hin, pos_f32, w_qkv, sinks, w_o, j):
    s = hin.shape[0]
    rep = SWA_HEADS // SWA_KV_HEADS
    sinks = sinks[j]
    qkv = matmul(hin, w_qkv, bm=1024, bn=512, out_dtype=BF16, layer=j)
    o = band_attention(
        qkv,
        lambda g: g,
        lambda g: SWA_HEADS + g,
        lambda g: SWA_HEADS + SWA_KV_HEADS + g,
        pos_f32.reshape(1, s, 1), pos_f32.reshape(1, s // BLK, 1, BLK),
        _alibi_slopes(SWA_HEADS), sinks,
        n_seq=1, n_groups=SWA_KV_HEADS, hps=rep, kv_shared=True, seq_len=s,
        window=SWA_WINDOW - 1, out_dtype=BF16, want_lse=False)
    return matmul(o, w_o, bm=1024, bn=512, out_dtype=H_DTYPE, layer=j)


_DIL_VIEWS = tuple(sorted({dil for _, dil in DIL_PATTERNS if dil > 1}))


def dilated_mixer(hin, hin_views, pos_f32, w_qkv, w_o, j):
    s, d_model = hin.shape
    nh = DIL_HEADS
    n_groups = len(DIL_PATTERNS)
    gw = 3 * nh * DIL_HEAD_DIM
    slopes_all = _alibi_slopes(n_groups * nh)
    views = {1: hin, **dict(zip(_DIL_VIEWS, hin_views))}
    outs, lses = [], []
    for gi, (win, dil) in enumerate(DIL_PATTERNS):
        pos_col, pos_row = _strided_positions(pos_f32, dil)
        qkv = matmul(views[dil], w_qkv, bm=1024, bn=512, out_dtype=BF16,
                     layer=j, n_slice=(gi * gw, gw), row_groups=dil)
        hps = 4
        ng = nh // hps
        o, lse = band_attention(
            qkv,
            lambda g: g,
            lambda g, ng=ng: ng + g,
            lambda g, ng=ng: 2 * ng + g,
            pos_col, pos_row, slopes_all[gi * nh:(gi + 1) * nh], None,
            n_seq=dil, n_groups=ng, hps=hps, kv_shared=False, seq_len=s // dil,
            window=win // dil, out_dtype=F32, want_lse=True)
        outs.append(o)
        lses.append(lse)
    o = dil_merge(outs, lses, [dil for _, dil in DIL_PATTERNS])
    return matmul(o, w_o, bm=1024, bn=512, out_dtype=H_DTYPE, layer=j)


def kernel(x, positions, norm_g, a_w_qkv, a_w_o, b_w_dq, b_g_q, b_w_uq, b_w_dkv, b_g_kv,
           b_w_ukv, b_w_o, c_w_qkv, c_sinks, c_w_o, d_w_qkv, d_w_o, ffn_w_in, ffn_conv_w,
           ffn_conv_b, ffn_w_out):
    batch, s, d = x.shape
    depth = norm_g.shape[0]
    n_mixers = 4
    outs = []
    for bi in range(batch):
        xb = x[bi]
        pos_f32 = positions[bi].astype(F32)
        hin = norm_rows(xb, norm_g[0, 0])
        hin_views = ()
        for i in range(depth):
            mixer, j = i % n_mixers, i // n_mixers
            if mixer == 0:
                h = stick_breaking_mixer(hin, a_w_qkv, a_w_o, j)
            elif mixer == 1:
                h = mla_mixer(hin, pos_f32, b_w_dq, b_g_q, b_w_uq, b_w_dkv, b_g_kv, b_w_ukv,
                              b_w_o, j)
            elif mixer == 2:
                h = swa_mixer(hin, pos_f32, c_w_qkv, c_sinks, c_w_o, j)
            else:
                h = dilated_mixer(hin, hin_views, pos_f32, d_w_qkv, d_w_o, j)
            xb, xn = resid_norm(h, xb, norm_g[i, 1], norm_g[i, 2])
            act, w_out16 = ffn_in(xn, ffn_w_in, i, ffn_conv_w[i], ffn_conv_b[i], ffn_w_out)
            h = ffn_out(act, w_out16)
            if i + 1 == depth:
                xb = resid_norm(h, xb, norm_g[i, 3])
            elif (i + 1) % n_mixers == 3:
                xb, hin, hin_views = resid_norm(h, xb, norm_g[i, 3], norm_g[i + 1, 0],
                                                dils=_DIL_VIEWS)
            else:
                xb, hin = resid_norm(h, xb, norm_g[i, 3], norm_g[i + 1, 0])
        outs.append(xb)
    return jnp.stack(outs, axis=0)
```

```python
import functools

import numpy as np
import jax
import jax.numpy as jnp
from jax import lax
from jax.experimental import pallas as pl
from jax.experimental.pallas import tpu as pltpu

F32 = jnp.float32
BF16 = jnp.bfloat16
H_DTYPE = BF16

V7X_VMEM_BYTES = 64 * 2**20
V7X_LANES = 128
V7X_MXU_DIM = 256

EPS = 1e-6
BLK = 128
SB_HEADS, SB_HEAD_DIM = 32, 128
MLA_HEADS, MLA_NOPE, MLA_ROPE, MLA_V = 32, 128, 64, 128
MLA_Q_LORA, MLA_KV_LORA = 1024, 512
ROPE_THETA = 10000.0
SWA_HEADS, SWA_KV_HEADS, SWA_HEAD_DIM, SWA_WINDOW = 32, 8, 128, 128
DIL_PATTERNS = ((128, 1), (512, 4), (2048, 16))
DIL_HEADS, DIL_HEAD_DIM = 16, 128
CONV_W = 3

MASKED = -1e30
EXP_ZERO_BELOW = -105.0

_NT = (((1,), (1,)), ((), ()))


_VMEM_SPILL_ALLOWANCE = 8 * 2**20
_VMEM_RESERVE = 6 * 2**20


def _params(semantics, vmem_bytes):
    limit = min(int(vmem_bytes) + _VMEM_SPILL_ALLOWANCE, V7X_VMEM_BYTES - _VMEM_RESERVE)
    return pltpu.CompilerParams(dimension_semantics=semantics, vmem_limit_bytes=limit)


def _rms(x, g):
    return (x * lax.rsqrt(jnp.mean(x * x, axis=-1, keepdims=True) + EPS)) * g


def _mm_body(*refs, mode, out_scale, sub):
    x_ref, w_ref, o_ref = refs[0], refs[1], refs[-1]
    if sub > 1:
        wb = w_ref[...].astype(BF16)
        rows, k = x_ref.shape[0], x_ref.shape[1] // sub
        for r in range(sub):
            o_ref[r * rows:(r + 1) * rows, :] = jnp.dot(
                x_ref[:, r * k:(r + 1) * k], wb, preferred_element_type=F32).astype(o_ref.dtype)
        return
    acc = jnp.dot(x_ref[...], w_ref[...].astype(BF16), preferred_element_type=F32)
    if mode == "plain":
        o_ref[...] = acc.astype(o_ref.dtype)
    elif mode == "add":
        o_ref[...] = (refs[2][...] + acc).astype(o_ref.dtype)
    else:
        cs = refs[2][...] * out_scale
        for c in range(acc.shape[1] // (2 * V7X_LANES)):
            lo = 2 * c * V7X_LANES
            o_ref[:, lo:lo + V7X_LANES] = (acc[:, lo:lo + V7X_LANES] * out_scale
                                           ).astype(o_ref.dtype)
            o_ref[:, lo + V7X_LANES:lo + 2 * V7X_LANES] = (
                acc[:, lo + V7X_LANES:lo + 2 * V7X_LANES] * cs).astype(o_ref.dtype)


def matmul(x, w, *, bm, bn, out_dtype, layer=0, k_slice=None, n_slice=None, row_groups=1,
           add=None, rope_cs=None, out_scale=1.0):
    n0, n = (0, w.shape[2]) if n_slice is None else n_slice
    if row_groups == 1:
        m, kx = x.shape
    else:
        assert k_slice is None
        m, kx = x.shape[0] * row_groups, x.shape[1] // row_groups
    kb, tk = (0, kx) if k_slice is None else k_slice
    group_rows = m // row_groups
    bm, bn = min(bm, m), min(bn, n)
    sub = max(bm // group_rows, 1)
    assert n % bn == 0 and n0 % bn == 0 and m % bm == 0
    assert (group_rows % bm == 0) if sub == 1 else (bm == sub * group_rows and add is None
                                                    and rope_cs is None)
    per_group = max(group_rows // bm, 1)
    in_specs = [pl.BlockSpec((bm // sub, sub * tk),
                             lambda i, j: (i % per_group, i // per_group + kb)),
                pl.BlockSpec((None, tk, bn), lambda i, j: (layer, kb, n0 // bn + j))]
    args = [x, w]
    mode = "plain"
    aliases = {}
    extra = 0
    if add is not None:
        in_specs.append(pl.BlockSpec((bm, bn), lambda i, j: (i, j)))
        args.append(add)
        mode = "add"
        aliases = {2: 0} if add.dtype == jnp.dtype(out_dtype) else {}
        extra = 2 * bm * bn * add.dtype.itemsize
    if rope_cs is not None:
        in_specs.append(pl.BlockSpec((bm, V7X_LANES), lambda i, j: (i, 0)))
        args.append(rope_cs)
        mode = "rope"
    osize = jnp.dtype(out_dtype).itemsize
    vmem = (2 * bm * tk * 2 + 2 * tk * bn * w.dtype.itemsize + tk * bn * 2
            + 2 * bm * bn * osize + bm * bn * 4 + extra)
    return pl.pallas_call(
        functools.partial(_mm_body, mode=mode, out_scale=out_scale, sub=sub),
        grid=(m // bm, n // bn),
        in_specs=in_specs,
        out_specs=pl.BlockSpec((bm, bn), lambda i, j: (i, j)),
        out_shape=jax.ShapeDtypeStruct((m, n), out_dtype),
        input_output_aliases=aliases,
        compiler_params=_params(("arbitrary", "arbitrary"), vmem),
        name="matmul",
    )(*args)


def _mm_nt_body(w_ref, x_ref, o_ref):
    o_ref[...] = lax.dot_general(w_ref[...].astype(BF16), x_ref[...], _NT,
                                 preferred_element_type=F32).astype(o_ref.dtype)


def matmul_nt_tiled(w_t, x, *, bm, bn, out_dtype):
    n, k = w_t.shape
    m = x.shape[0]
    bm, bn = min(bm, m), min(bn, n)
    assert m % bm == 0 and n % bn == 0
    vmem = 3 * bn * k * 4 + 2 * bm * k * 2 + 2 * bn * bm * jnp.dtype(out_dtype).itemsize + bn * bm * 4
    return pl.pallas_call(
        _mm_nt_body,
        grid=(n // bn, m // bm),
        in_specs=[pl.BlockSpec((bn, k), lambda j, i: (j, 0)),
                  pl.BlockSpec((bm, k), lambda j, i: (i, 0))],
        out_specs=pl.BlockSpec((None, bn, bm), lambda j, i: (i, j, 0)),
        out_shape=jax.ShapeDtypeStruct((m // bm, n, bm), out_dtype),
        compiler_params=_params(("arbitrary", "arbitrary"), vmem),
        name="matmul_nt",
    )(w_t, x)


def _norm_body(x_ref, g_ref, o_ref):
    o_ref[...] = _rms(x_ref[...], g_ref[...]).astype(o_ref.dtype)


def norm_rows(x, g, *, tm=256):
    m, d = x.shape
    tm = min(tm, m)
    return pl.pallas_call(
        _norm_body,
        grid=(m // tm,),
        in_specs=[pl.BlockSpec((tm, d), lambda i: (i, 0)),
                  pl.BlockSpec((1, d), lambda i: (0, 0))],
        out_specs=pl.BlockSpec((tm, d), lambda i: (i, 0)),
        out_shape=jax.ShapeDtypeStruct((m, d), BF16),
        compiler_params=_params(("arbitrary",), 2 * tm * d * 6 + 3 * tm * d * 4),
        name="norm_rows",
    )(x, g.reshape(1, d))


def _resid_norm_body(h_ref, x_ref, ga_ref, gb_ref, xo_ref, no_ref, *rest, dils):
    xn = x_ref[...] + _rms(h_ref[...].astype(F32), ga_ref[...])
    xo_ref[...] = xn
    n = _rms(xn, gb_ref[...])
    no_ref[...] = n.astype(no_ref.dtype)
    if dils:
        n_scr = rest[-1]
        tm, dm = n.shape
        chunks = dm // V7X_LANES
        for c in range(chunks):
            n_scr[c] = n[:, c * V7X_LANES:(c + 1) * V7X_LANES]
        for d, ref in zip(dils, rest[:-1]):
            for r in range(d):
                for c in range(chunks):
                    lo = r * dm + c * V7X_LANES
                    ref[:, lo:lo + V7X_LANES] = n_scr[c, pl.ds(r, tm // d, stride=d), :].astype(
                        ref.dtype)


def _resid_body(h_ref, x_ref, ga_ref, xo_ref):
    xo_ref[...] = x_ref[...] + _rms(h_ref[...].astype(F32), ga_ref[...])


def resid_norm(h, x, g_a, g_b=None, *, tm=256, dils=()):
    m, d = x.shape
    if dils:
        tm = 16 * max(dils)
    tm = min(tm, m)
    row = pl.BlockSpec((tm, d), lambda i: (i, 0))
    gain = pl.BlockSpec((1, d), lambda i: (0, 0))
    if g_b is None:
        return pl.pallas_call(
            _resid_body,
            grid=(m // tm,),
            in_specs=[row, row, gain],
            out_specs=row,
            out_shape=jax.ShapeDtypeStruct((m, d), F32),
            compiler_params=_params(("arbitrary",), 2 * tm * d * 12 + 3 * tm * d * 4),
            name="resid",
        )(h, x, g_a.reshape(1, d))
    outs = pl.pallas_call(
        functools.partial(_resid_norm_body, dils=tuple(dils)),
        grid=(m // tm,),
        in_specs=[row, row, gain, gain],
        out_specs=[row, row] + [pl.BlockSpec((tm // dl, dl * d), lambda i: (i, 0)) for dl in dils],
        out_shape=[jax.ShapeDtypeStruct((m, d), F32), jax.ShapeDtypeStruct((m, d), BF16)]
        + [jax.ShapeDtypeStruct((m // dl, dl * d), BF16) for dl in dils],
        scratch_shapes=[pltpu.VMEM((d // V7X_LANES, tm, V7X_LANES), F32)] if dils else [],
        compiler_params=_params(("arbitrary",),
                                2 * tm * d * (14 + 2 * len(dils)) + 5 * tm * d * 4),
        name="resid_norm",
    )(h, x, g_a.reshape(1, d), g_b.reshape(1, d))
    return (outs[0], outs[1]) + ((tuple(outs[2:]),) if dils else ())


_HALO = 8


def _order_after(x, v):
    zero = (lax.bitcast_convert_type(v, jnp.uint32) >> 16) >> 16
    return pltpu.bitcast(pltpu.bitcast(x, jnp.uint32) | zero, BF16)


def _ffn_in_body(x_ref, wg_ref, wu_ref, cw_ref, cb_ref, wo_ref, o_ref, wo16_ref, wb_ref, g_ref,
                 u_ref, t_ref, *, bm, cm, row_tiles):
    s = pl.program_id(0)
    i = s % row_tiles
    bn = o_ref.shape[1]
    nc = bm // cm

    @pl.when(s == 0)
    def _():
        g_ref[...] = jnp.zeros(g_ref.shape, F32)
        u_ref[...] = jnp.zeros(u_ref.shape, F32)

    @pl.when(i == 0)
    def _():
        wb_ref[:, 0:bn] = wg_ref[...].astype(BF16)
        wb_ref[:, bn:2 * bn] = wu_ref[...].astype(BF16)

    wo16_ref[...] = wo_ref[...].astype(BF16)

    cw = cw_ref[...]
    cb = cb_ref[...]
    for c in range(nc):
        rows = slice(c * cm, (c + 1) * cm)
        lo = _HALO + c * cm
        g = g_ref[lo:lo + cm, :]
        g1 = g_ref[lo - 1:lo - 1 + cm, :]
        g2 = g_ref[lo - 2:lo - 2 + cm, :]
        gate = cw[0:1, :] * g2 + cw[1:2, :] * g1 + cw[2:3, :] * g + cb
        act = (gate / (1.0 + jnp.exp(-gate))) * u_ref[rows, :]
        o_ref[rows, :] = act.astype(o_ref.dtype)
        if c == 0:
            g_ref[0:_HALO, :] = jnp.where(i == 0, 0.0, g_ref[bm:bm + _HALO, :])
        else:
            g_ref[lo - _HALO:lo, :] = t_ref[c - 1]
        x0 = _order_after(x_ref[rows, 0:V7X_LANES], act[0:cm // 2, 0:V7X_LANES])
        x = jnp.concatenate([x0, x_ref[rows, V7X_LANES:]], axis=1)
        gu = jnp.dot(x, wb_ref[...], preferred_element_type=F32)
        u_ref[rows, :] = gu[:, bn:2 * bn]
        if c == nc - 1:
            g_ref[lo:lo + cm, :] = gu[:, 0:bn]
        else:
            g_ref[lo:lo + cm - _HALO, :] = gu[0:cm - _HALO, 0:bn]
            t_ref[c] = gu[cm - _HALO:cm, 0:bn]


def ffn_in(xn, w_in, layer, conv_w, conv_b, w_out, *, bm=1024, bn=V7X_MXU_DIM, cm=128):
    m, k = xn.shape
    d_ff = w_in.shape[2] // 2
    d_out = w_out.shape[2]
    bm = min(bm, m)
    bn = min(bn, d_ff)
    assert m % bm == 0 and d_ff % bn == 0
    nj, ni = d_ff // bn, m // bm
    tiles = nj * ni
    cm = min(cm, bm)
    assert bm % cm == 0 and cm % (2 * _HALO) == 0
    slab = d_ff // tiles
    assert slab * tiles == d_ff and slab % 16 == 0

    def mm(f):
        return lambda s: f(jnp.minimum(s, tiles - 1) // ni, jnp.minimum(s, tiles - 1) % ni)

    def ep(f):
        return lambda s: f(jnp.maximum(s - 1, 0) // ni, jnp.maximum(s - 1, 0) % ni)

    vmem = (2 * bm * k * 2 + 2 * 2 * k * bn * 4 + 2 * k * bn * 2 + 2 * bm * bn * 2
            + (2 * bm + _HALO) * bn * 4 + 6 * bm * bn * 4)
    return pl.pallas_call(
        functools.partial(_ffn_in_body, bm=bm, cm=cm, row_tiles=ni),
        grid=(tiles + 1,),
        in_specs=[pl.BlockSpec((bm, k), mm(lambda j, i: (i, 0))),
                  pl.BlockSpec((None, k, bn), mm(lambda j, i: (layer, 0, j))),
                  pl.BlockSpec((None, k, bn), mm(lambda j, i: (layer, 0, nj + j))),
                  pl.BlockSpec((CONV_W, bn), ep(lambda j, i: (0, j))),
                  pl.BlockSpec((1, bn), ep(lambda j, i: (0, j))),
                  pl.BlockSpec((None, slab, d_out),
                               lambda s: (layer, jnp.minimum(s, tiles - 1), 0))],
        out_specs=[pl.BlockSpec((bm, bn), ep(lambda j, i: (i, j))),
                   pl.BlockSpec((slab, d_out), lambda s: (jnp.minimum(s, tiles - 1), 0))],
        out_shape=[jax.ShapeDtypeStruct((m, d_ff), BF16),
                   jax.ShapeDtypeStruct((d_ff, d_out), BF16)],
        scratch_shapes=[pltpu.VMEM((k, 2 * bn), BF16), pltpu.VMEM((bm + _HALO, bn), F32),
                        pltpu.VMEM((bm, bn), F32),
                        pltpu.VMEM((max(bm // cm - 1, 1), _HALO, bn), F32)],
        compiler_params=_params(("arbitrary",), vmem + 12 * slab * d_out),
        name="ffn_in",
    )(xn, w_in, w_in, conv_w, conv_b.reshape(1, d_ff), w_out)


def ffn_out(act, w_out16, *, bm=1024, bn=2 * V7X_MXU_DIM):
    return matmul(act, w_out16[None], bm=bm // 2, bn=bn, out_dtype=H_DTYPE)


_SB_HEADS_PER_STEP = 4


def _sb_attn_body(q_ref, k_ref, v_ref, tri_ref, o_ref, acc_ref, c_ref, *, blk, scale):
    qi = pl.program_id(1)
    nh = _SB_HEADS_PER_STEP
    dh = SB_HEAD_DIM
    tri = tri_ref[...]
    acc_ref[...] = jnp.zeros(acc_ref.shape, F32)
    c_ref[...] = jnp.zeros(c_ref.shape, F32)
    key = lax.broadcasted_iota(jnp.int32, (blk, blk), 0)
    qry = lax.broadcasted_iota(jnp.int32, (blk, blk), 1)
    causal = key < qry

    def block(kb, diagonal):
        start = pl.multiple_of(kb * blk, blk)
        heads = range(nh)
        cols = [slice(hd * dh, (hd + 1) * dh) for hd in heads]
        z = [lax.dot_general(k_ref[pl.ds(start, blk), cols[hd]], q_ref[:, cols[hd]], _NT,
                             preferred_element_type=F32) * scale for hd in heads]
        vt = [v_ref[pl.ds(start, blk), cols[hd]].astype(F32).T.astype(BF16) for hd in heads]
        log_beta, log_1mb, hi, lo = [], [], [], []
        for hd in heads:
            sp = jnp.log(1.0 + jnp.exp(-jnp.abs(z[hd])))
            log_beta.append(jnp.minimum(z[hd], 0.0) - sp)
            l1 = jnp.minimum(-z[hd], 0.0) - sp
            if diagonal:
                l1 = jnp.where(causal, l1, 0.0)
            log_1mb.append(l1)
            hi.append(l1.astype(BF16))
            lo.append((l1 - hi[hd].astype(F32)).astype(BF16))
        tail = [jnp.dot(tri, hi[hd], preferred_element_type=F32)
                + jnp.dot(tri, lo[hd], preferred_element_type=F32) for hd in heads]
        c = [c_ref[hd] for hd in heads]
        a = []
        for hd in heads:
            w = jnp.exp(log_beta[hd] + tail[hd] + c[hd])
            if diagonal:
                w = jnp.where(causal, w, 0.0)
            a.append(w.astype(BF16))
        top = None
        for hd in heads:
            acc_ref[hd] += jnp.dot(vt[hd], a[hd], preferred_element_type=F32)
            c_new = c[hd] + tail[hd][0:1, :] + log_1mb[hd][0:1, :]
            c_ref[hd] = c_new
            top = jnp.max(c_new) if top is None else jnp.maximum(top, jnp.max(c_new))
        return top

    top = block(qi, True)
    lax.while_loop(lambda st: (st[0] >= 0) & (st[1] > EXP_ZERO_BELOW),
                   lambda st: (st[0] - 1, block(st[0], False)),
                   (qi - 1, top))
    for hd in range(nh):
        o_ref[:, hd * dh:(hd + 1) * dh] = acc_ref[hd].T.astype(o_ref.dtype)


def sb_attention(qkv, *, blk=256):
    s = qkv.shape[0]
    h, dh = SB_HEADS, SB_HEAD_DIM
    nh = _SB_HEADS_PER_STEP
    blk = min(blk, s)
    groups = h // nh
    tri = (np.arange(blk)[:, None] < np.arange(blk)[None, :]).astype(np.float32)
    vmem = (2 * 2 * s * nh * dh * 2 + 4 * blk * nh * dh * 2 + 2 * blk * blk * 2
            + nh * (dh + 8) * blk * 4 + nh * 12 * blk * blk * 4)
    return pl.pallas_call(
        functools.partial(_sb_attn_body, blk=blk, scale=dh ** -0.5),
        grid=(groups, s // blk),
        in_specs=[pl.BlockSpec((blk, nh * dh), lambda g, i: (i, g)),
                  pl.BlockSpec((s, nh * dh), lambda g, i: (0, groups + g)),
                  pl.BlockSpec((s, nh * dh), lambda g, i: (0, 2 * groups + g)),
                  pl.BlockSpec((blk, blk), lambda g, i: (0, 0))],
        out_specs=pl.BlockSpec((blk, nh * dh), lambda g, i: (i, g)),
        out_shape=jax.ShapeDtypeStruct((s, h * dh), BF16),
        scratch_shapes=[pltpu.VMEM((nh, dh, blk), F32), pltpu.VMEM((nh, 1, blk), F32)],
        compiler_params=_params(("arbitrary", "arbitrary"), vmem),
        name="sb_attention",
    )(qkv, qkv, qkv, jnp.asarray(tri, BF16))


def _mla_prep_body(y_ref, pos_ref, inv_ref, gq_ref, gkv_ref, cq_ref, ckv_ref, kr_ref, cs_ref):
    ql, kvl = MLA_Q_LORA, MLA_KV_LORA
    cq_ref[...] = _rms(y_ref[:, 0:ql], gq_ref[...]).astype(cq_ref.dtype)
    ckv_ref[...] = _rms(y_ref[:, ql:ql + kvl], gkv_ref[...]).astype(ckv_ref.dtype)
    ang = pos_ref[...] * inv_ref[...]
    lane = lax.broadcasted_iota(jnp.int32, ang.shape, 1)
    cs = jnp.where(lane < MLA_ROPE, jnp.cos(ang), jnp.sin(ang))
    cs_ref[...] = cs
    w = y_ref[:, ql + kvl:ql + kvl + V7X_LANES] * cs
    kr_ref[...] = (w + pltpu.roll(w, MLA_ROPE, axis=1)).astype(kr_ref.dtype)


def mla_prep(y, pos_col, g_q, g_kv, *, tm=256):
    m, width = y.shape
    tm = min(tm, m)
    half = MLA_ROPE // 2
    inv = ROPE_THETA ** (-jnp.arange(half, dtype=F32) / half)
    inv4 = jnp.tile(inv, 4).reshape(1, V7X_LANES)
    row = lambda w: pl.BlockSpec((tm, w), lambda i: (i, 0))
    const = lambda w: pl.BlockSpec((1, w), lambda i: (0, 0))
    return pl.pallas_call(
        _mla_prep_body,
        grid=(m // tm,),
        in_specs=[row(width), row(1), const(V7X_LANES), const(MLA_Q_LORA), const(MLA_KV_LORA)],
        out_specs=[row(MLA_Q_LORA), row(MLA_KV_LORA), row(V7X_LANES), row(V7X_LANES)],
        out_shape=[jax.ShapeDtypeStruct((m, MLA_Q_LORA), BF16),
                   jax.ShapeDtypeStruct((m, MLA_KV_LORA), BF16),
                   jax.ShapeDtypeStruct((m, V7X_LANES), BF16),
                   jax.ShapeDtypeStruct((m, V7X_LANES), F32)],
        compiler_params=_params(("arbitrary",), 6 * tm * width * 4),
        name="mla_prep",
    )(y, pos_col, inv4, g_q.reshape(1, -1), g_kv.reshape(1, -1))


_MLA_HEADS_PER_STEP = 4


def _mla_attn_body(q_ref, kn_ref, kr_ref, vt_ref, o_ref, s_ref, m_ref, l_ref, acc_ref, *, blk):
    qi = pl.program_id(1)
    nh = _MLA_HEADS_PER_STEP
    wq = MLA_NOPE + V7X_LANES
    m_ref[...] = jnp.full(m_ref.shape, MASKED, F32)
    l_ref[...] = jnp.zeros(l_ref.shape, F32)
    acc_ref[...] = jnp.zeros(acc_ref.shape, F32)
    key = lax.broadcasted_iota(jnp.int32, (blk, blk), 0)
    qry = lax.broadcasted_iota(jnp.int32, (blk, blk), 1)

    def scores(kb, slot):
        start = pl.multiple_of(kb * blk, blk)
        kr = kr_ref[pl.ds(start, blk), :]
        for hd in range(nh):
            kcat = jnp.concatenate(
                [kn_ref[pl.ds(start, blk), hd * MLA_NOPE:(hd + 1) * MLA_NOPE], kr], axis=1)
            s_ref[slot, hd] = lax.dot_general(kcat, q_ref[:, hd * wq:(hd + 1) * wq], _NT,
                                              preferred_element_type=F32)

    def reduce(kb, slot, diagonal):
        heads = range(nh)
        st = [s_ref[slot, hd] for hd in heads]
        if diagonal:
            st = [jnp.where(key <= qry, t, MASKED) for t in st]
        m_prev = [m_ref[hd] for hd in heads]
        m_new = [jnp.maximum(m_prev[hd], jnp.max(st[hd], axis=0, keepdims=True)) for hd in heads]
        p = [jnp.exp2(st[hd] - m_new[hd]) for hd in heads]
        pv = [jnp.dot(vt_ref[kb, hd * MLA_V:(hd + 1) * MLA_V, :], p[hd].astype(BF16),
                      preferred_element_type=F32) for hd in heads]
        for hd in heads:
            alpha = jnp.exp2(m_prev[hd] - m_new[hd])
            l_ref[hd] = alpha * l_ref[hd] + jnp.sum(p[hd], axis=0, keepdims=True)
            acc_ref[hd] = alpha * acc_ref[hd] + pv[hd]
            m_ref[hd] = m_new[hd]

    def off_diagonal_pair(t, carry):
        scores(2 * t + 1, 1)
        reduce(2 * t, 0, False)
        scores(2 * t + 2, 0)
        reduce(2 * t + 1, 1, False)
        return carry

    scores(0, 0)
    lax.fori_loop(0, qi // 2, off_diagonal_pair, 0)

    @pl.when(qi % 2 == 0)
    def _():
        reduce(qi, 0, True)

    @pl.when(qi % 2 == 1)
    def _():
        scores(qi, 1)
        reduce(qi - 1, 0, False)
        reduce(qi, 1, True)

    for hd in range(nh):
        o_ref[:, hd * MLA_V:(hd + 1) * MLA_V] = (acc_ref[hd] / l_ref[hd]).T.astype(o_ref.dtype)


def mla_attention(q, kn, kr, vt, *, blk):
    s = q.shape[0]
    h = MLA_HEADS
    nh = _MLA_HEADS_PER_STEP
    wq = MLA_NOPE + V7X_LANES
    assert vt.shape == (s // blk, h * MLA_V, blk)
    vmem = (2 * blk * nh * wq * 2 + 2 * s * nh * MLA_NOPE * 2 + 2 * s * V7X_LANES * 2
            + 2 * s * nh * MLA_V * 2 + 2 * blk * nh * MLA_V * 2 + nh * (MLA_V + 16) * blk * 4
            + nh * 8 * blk * blk * 4)
    return pl.pallas_call(
        functools.partial(_mla_attn_body, blk=blk),
        grid=(h // nh, s // blk),
        in_specs=[pl.BlockSpec((blk, nh * wq), lambda hp, i: (i, hp)),
                  pl.BlockSpec((s, nh * MLA_NOPE), lambda hp, i: (0, hp)),
                  pl.BlockSpec((s, V7X_LANES), lambda hp, i: (0, 0)),
                  pl.BlockSpec((s // blk, nh * MLA_V, blk), lambda hp, i: (0, hp, 0))],
        out_specs=pl.BlockSpec((blk, nh * MLA_V), lambda hp, i: (i, hp)),
        out_shape=jax.ShapeDtypeStruct((s, h * MLA_V), BF16),
        scratch_shapes=[pltpu.VMEM((2, nh, blk, blk), F32),
                        pltpu.VMEM((nh, 1, blk), F32), pltpu.VMEM((nh, 1, blk), F32),
                        pltpu.VMEM((nh, MLA_V, blk), F32)],
        compiler_params=_params(("arbitrary", "arbitrary"), vmem),
        name="mla_attention",
    )(q, kn, kr, vt)


def _band_attn_body(*refs, tile, hps, kv_shared, window, scale, has_sink, want_lse):
    refs = list(refs)
    slope_ref = refs.pop(0)
    sink_ref = refs.pop(0) if has_sink else None
    q_ref, k_ref, v_ref, pq_ref, pk_ref, o_ref = refs[:6]
    lse_ref = refs[6] if want_lse else None
    hbase = pl.program_id(1) * hps
    t = pl.program_id(2)
    nb = tile // BLK
    heads = range(hps)
    qi = lax.broadcasted_iota(jnp.int32, (BLK, 2 * BLK), 0)
    kj = lax.broadcasted_iota(jnp.int32, (BLK, 2 * BLK), 1)
    slopes = [slope_ref[hbase + h] for h in heads]
    sinks = [sink_ref[hbase + h] for h in heads] if has_sink else None
    for b in range(nb):
        gb = t * nb + b
        pb = jnp.maximum(gb - 1, 0)
        start = pl.multiple_of(pb * BLK, BLK)
        rows = slice(b * BLK, (b + 1) * BLK)
        pk = jnp.concatenate([pk_ref[pb], pk_ref[pb + 1]], axis=1)
        delta = (gb - pb) * BLK + qi - kj
        valid = (delta >= 0) & (delta <= window)
        dist = jnp.where(valid, pq_ref[rows, :] - pk, -MASKED)
        cols = [slice(h * BLK, (h + 1) * BLK) for h in heads]
        kv_cols = [slice(0, BLK)] * hps if kv_shared else cols
        sc = [lax.dot_general(q_ref[rows, cols[h]], k_ref[pl.ds(start, 2 * BLK), kv_cols[h]], _NT,
                              preferred_element_type=F32) * scale - slopes[h] * dist
              for h in heads]
        m = [jnp.max(sc[h], axis=1, keepdims=True) for h in heads]
        if has_sink:
            m = [jnp.maximum(m[h], sinks[h]) for h in heads]
        p = [jnp.exp(sc[h] - m[h]) for h in heads]
        den = [jnp.sum(p[h], axis=1, keepdims=True) for h in heads]
        if has_sink:
            den = [den[h] + jnp.exp(sinks[h] - m[h]) for h in heads]
        o = [jnp.dot((p[h] * (1.0 / den[h])).astype(BF16),
                     v_ref[pl.ds(start, 2 * BLK), kv_cols[h]], preferred_element_type=F32)
             for h in heads]
        for h in heads:
            o_ref[rows, cols[h]] = o[h].astype(o_ref.dtype)
            if want_lse:
                lse_ref[rows, cols[h]] = jnp.broadcast_to(m[h] + jnp.log(den[h]), (BLK, BLK))


def band_attention(arr, col_q, col_k, col_v, pos_col, pos_row, slopes, sinks, *,
                   n_seq, n_groups, hps, kv_shared, seq_len, window, out_dtype, want_lse):
    dh = BLK
    tile = min(seq_len, 8 * BLK)
    tiles = seq_len // tile
    has_sink = sinks is not None
    kvw = dh if kv_shared else hps * dh
    smem = pl.BlockSpec(memory_space=pltpu.SMEM)
    in_specs = [smem] + ([smem] if has_sink else []) + [
        pl.BlockSpec((tile, hps * dh), lambda r, g, t: (r * tiles + t, col_q(g))),
        pl.BlockSpec((seq_len, kvw), lambda r, g, t: (r, col_k(g))),
        pl.BlockSpec((seq_len, kvw), lambda r, g, t: (r, col_v(g))),
        pl.BlockSpec((None, tile, 1), lambda r, g, t: (r, t, 0)),
        pl.BlockSpec((None, seq_len // BLK, 1, BLK), lambda r, g, t: (r, 0, 0, 0)),
    ]
    args = [slopes] + ([sinks] if has_sink else []) + [arr, arr, arr, pos_col, pos_row]
    out_block = pl.BlockSpec((tile, hps * dh), lambda r, g, t: (t, r * n_groups + g))
    out_cols = n_seq * n_groups * hps * dh
    out_sds = jax.ShapeDtypeStruct((seq_len, out_cols), out_dtype)
    osize = jnp.dtype(out_dtype).itemsize
    vmem = (2 * tile * hps * dh * 2 + 2 * 2 * seq_len * kvw * 2 + 4 * tile * hps * dh * osize
            + 2 * tile * V7X_LANES * 4 + 2 * seq_len * 4 * 8 + hps * 8 * BLK * 2 * BLK * 4)
    return pl.pallas_call(
        functools.partial(_band_attn_body, tile=tile, hps=hps, kv_shared=kv_shared, window=window,
                          scale=dh ** -0.5, has_sink=has_sink, want_lse=want_lse),
        grid=(n_seq, n_groups, seq_len // tile),
        in_specs=in_specs,
        out_specs=[out_block, out_block] if want_lse else out_block,
        out_shape=[out_sds, jax.ShapeDtypeStruct((seq_len, out_cols), F32)] if want_lse else out_sds,
        compiler_params=_params(("arbitrary", "arbitrary", "arbitrary"), vmem),
        name="band_attention",
    )(*args)


def _alibi_slopes(n):
    return jnp.asarray(2.0 ** (-8.0 * np.arange(1, n + 1) / n), dtype=F32)


def _strided_positions(pos_f32, dil):
    s = pos_f32.shape[0]
    ps = pos_f32.reshape(s // dil, dil).T
    return ps.reshape(dil, s // dil, 1), ps.reshape(dil, s // dil // BLK, 1, BLK)


def _dil_merge_body(*refs, dils):
    n = len(dils)
    o_refs, l_refs, out_ref, buf = refs[:n], refs[n:2 * n], refs[2 * n], refs[2 * n + 1]
    tm, w = out_ref.shape

    def token_order(ref, d, c):
        if d == 1:
            return ref[:, c * V7X_LANES:(c + 1) * V7X_LANES]
        for r in range(d):
            lo = r * w + c * V7X_LANES
            buf[pl.ds(r, tm // d, stride=d), :] = ref[:, lo:lo + V7X_LANES]
        return buf[...]

    for c in range(w // V7X_LANES):
        lse = [token_order(l_refs[g], dils[g], c) for g in range(n)]
        m = functools.reduce(jnp.maximum, lse)
        e = [jnp.exp(l - m) for l in lse]
        inv = 1.0 / functools.reduce(lambda a, b: a + b, e)
        acc = None
        for g in range(n):
            term = (e[g] * inv) * token_order(o_refs[g], dils[g], c)
            acc = term if acc is None else acc + term
        out_ref[:, c * V7X_LANES:(c + 1) * V7X_LANES] = acc.astype(out_ref.dtype)


def dil_merge(outs, lses, dils, *, tm=256):
    m = outs[0].shape[0] * dils[0]
    w = outs[0].shape[1] // dils[0]
    tm = min(tm, m)
    specs = [pl.BlockSpec((tm // d, d * w), lambda i: (i, 0)) for d in dils]
    return pl.pallas_call(
        functools.partial(_dil_merge_body, dils=tuple(dils)),
        grid=(m // tm,),
        in_specs=specs + specs,
        out_specs=pl.BlockSpec((tm, w), lambda i: (i, 0)),
        out_shape=jax.ShapeDtypeStruct((m, w), BF16),
        scratch_shapes=[pltpu.VMEM((tm, V7X_LANES), F32)],
        compiler_params=_params(("arbitrary",), (4 * len(dils) + 4) * tm * w * 4),
        name="dil_merge",
    )(*outs, *lses)


def stick_breaking_mixer(hin, w_qkv, w_o, j):
    qkv = matmul(hin, w_qkv, bm=1024, bn=512, out_dtype=BF16, layer=j)
    o = sb_attention(qkv)
    return matmul(o, w_o, bm=1024, bn=512, out_dtype=H_DTYPE, layer=j)


def mla_mixer(hin, pos_f32, w_dq, g_q, w_uq, w_dkv, g_kv, w_ukv, w_o, j):
    h, nope, rope = MLA_HEADS, MLA_NOPE, MLA_ROPE
    half = rope // 2
    w_dq, w_uq, w_dkv, g_q, g_kv = w_dq[j], w_uq[j], w_dkv[j], g_q[j], g_kv[j]
    kx1 = w_dkv[:, MLA_KV_LORA:MLA_KV_LORA + half]
    kx2 = w_dkv[:, MLA_KV_LORA + half:]
    w_down = jnp.concatenate(
        [w_dq, w_dkv[:, :MLA_KV_LORA], kx1, kx2, -kx2, kx1,
         jnp.zeros((w_dq.shape[0], V7X_LANES), w_dq.dtype)], axis=1)
    y = matmul(hin, w_down[None], bm=1024, bn=V7X_MXU_DIM, out_dtype=F32)
    cq, ckv, kr, cs = mla_prep(y, pos_f32.reshape(-1, 1), g_q, g_kv)
    wq = w_uq.reshape(MLA_Q_LORA, h, nope + rope)
    qx1, qx2 = wq[:, :, nope:nope + half], wq[:, :, nope + half:]
    wq = jnp.concatenate([wq[:, :, :nope], qx1, qx2, -qx2, qx1], axis=2)
    q_scale = (nope + rope) ** -0.5 * float(np.log2(np.e))
    q = matmul(cq, wq.reshape(1, MLA_Q_LORA, h * 2 * V7X_LANES), bm=2048, bn=1024,
               out_dtype=BF16, rope_cs=cs, out_scale=q_scale)
    wkv = w_ukv[j].reshape(MLA_KV_LORA, h, nope + MLA_V)
    w_k = wkv[:, :, :nope].reshape(1, MLA_KV_LORA, h * nope)
    w_vt = wkv[:, :, nope:].reshape(MLA_KV_LORA, h * MLA_V).T
    blk = min(512, hin.shape[0])
    kn = matmul(ckv, w_k, bm=2048, bn=1024, out_dtype=BF16)
    vt = matmul_nt_tiled(w_vt, ckv, bm=blk, bn=1024, out_dtype=BF16)
    o = mla_attention(q, kn, kr, vt, blk=blk)
    return matmul(o, w_o, bm=1024, bn=512, out_dtype=H_DTYPE, layer=j)


def swa_mixer(hin, pos_f32, w_qkv, sinks, w_o, j):
    s = hin.shape[0]
    rep = SWA_HEADS // SWA_KV_HEADS
    sinks = sinks[j]
    qkv = matmul(hin, w_qkv, bm=1024, bn=512, out_dtype=BF16, layer=j)
    o = band_attention(
        qkv,
        lambda g: g,
        lambda g: SWA_HEADS + g,
        lambda g: SWA_HEADS + SWA_KV_HEADS + g,
        pos_f32.reshape(1, s, 1), pos_f32.reshape(1, s // BLK, 1, BLK),
        _alibi_slopes(SWA_HEADS), sinks,
        n_seq=1, n_groups=SWA_KV_HEADS, hps=rep, kv_shared=True, seq_len=s,
        window=SWA_WINDOW - 1, out_dtype=BF16, want_lse=False)
    return matmul(o, w_o, bm=1024, bn=512, out_dtype=H_DTYPE, layer=j)


_DIL_VIEWS = tuple(sorted({dil for _, dil in DIL_PATTERNS if dil > 1}))


def dilated_mixer(hin, hin_views, pos_f32, w_qkv, w_o, j):
    s, d_model = hin.shape
    nh = DIL_HEADS
    n_groups = len(DIL_PATTERNS)
    gw = 3 * nh * DIL_HEAD_DIM
    slopes_all = _alibi_slopes(n_groups * nh)
    views = {1: hin, **dict(zip(_DIL_VIEWS, hin_views))}
    outs, lses = [], []
    for gi, (win, dil) in enumerate(DIL_PATTERNS):
        pos_col, pos_row = _strided_positions(pos_f32, dil)
        qkv = matmul(views[dil], w_qkv, bm=1024, bn=512, out_dtype=BF16,
                     layer=j, n_slice=(gi * gw, gw), row_groups=dil)
        hps = 4
        ng = nh // hps
        o, lse = band_attention(
            qkv,
            lambda g: g,
            lambda g, ng=ng: ng + g,
            lambda g, ng=ng: 2 * ng + g,
            pos_col, pos_row, slopes_all[gi * nh:(gi + 1) * nh], None,
            n_seq=dil, n_groups=ng, hps=hps, kv_shared=False, seq_len=s // dil,
            window=win // dil, out_dtype=F32, want_lse=True)
        outs.append(o)
        lses.append(lse)
    o = dil_merge(outs, lses, [dil for _, dil in DIL_PATTERNS])
    return matmul(o, w_o, bm=1024, bn=512, out_dtype=H_DTYPE, layer=j)


def kernel(x, positions, norm_g, a_w_qkv, a_w_o, b_w_dq, b_g_q, b_w_uq, b_w_dkv, b_g_kv,
           b_w_ukv, b_w_o, c_w_qkv, c_sinks, c_w_o, d_w_qkv, d_w_o, ffn_w_in, ffn_conv_w,
           ffn_conv_b, ffn_w_out):
    batch, s, d = x.shape
    depth = norm_g.shape[0]
    n_mixers = 4
    outs = []
    for bi in range(batch):
        xb = x[bi]
        pos_f32 = positions[bi].astype(F32)
        hin = norm_rows(xb, norm_g[0, 0])
        hin_views = ()
        for i in range(depth):
            mixer, j = i % n_mixers, i // n_mixers
            if mixer == 0:
                h = stick_breaking_mixer(hin, a_w_qkv, a_w_o, j)
            elif mixer == 1:
                h = mla_mixer(hin, pos_f32, b_w_dq, b_g_q, b_w_uq, b_w_dkv, b_g_kv, b_w_ukv,
                              b_w_o, j)
            elif mixer == 2:
                h = swa_mixer(hin, pos_f32, c_w_qkv, c_sinks, c_w_o, j)
            else:
                h = dilated_mixer(hin, hin_views, pos_f32, d_w_qkv, d_w_o, j)
            xb, xn = resid_norm(h, xb, norm_g[i, 1], norm_g[i, 2])
            act, w_out16 = ffn_in(xn, ffn_w_in, i, ffn_conv_w[i], ffn_conv_b[i], ffn_w_out)
            h = ffn_out(act, w_out16)
            if i + 1 == depth:
                xb = resid_norm(h, xb, norm_g[i, 3])
            elif (i + 1) % n_mixers == 3:
                xb, hin, hin_views = resid_norm(h, xb, norm_g[i, 3], norm_g[i + 1, 0],
                                                dils=_DIL_VIEWS)
            else:
                xb, hin = resid_norm(h, xb, norm_g[i, 3], norm_g[i + 1, 0])
        outs.append(xb)
    return jnp.stack(outs, axis=0)
```
